```python
import math
import jax, jax.numpy as jnp
from jax import lax
import numpy as np

D_MODEL = 1024
BATCH = 16
SEQ = 2048
DEPTH = 2

SSM_WIDTH = D_MODEL // 4
SSM_GROUP = 16
SSM_GROUPS = SSM_WIDTH // SSM_GROUP
SSM_STATE = 64
RET_HEAD_DIM = 64
RET_WIDTH = D_MODEL // 4
RET_HEADS = RET_WIDTH // RET_HEAD_DIM
RET_CHUNK = 128
ATT_HEAD_DIM = 64
ATT_WIDTH = D_MODEL // 2
ATT_HEADS = ATT_WIDTH // ATT_HEAD_DIM
DILATED_BRANCHES = ((128, 1), (512, 4), (2048, 16))
ATT_BLOCK = 128
MIX_WIDTH = SSM_WIDTH + RET_WIDTH + ATT_WIDTH
IN_WIDTH = SSM_WIDTH + 4 * RET_WIDTH + 3 * ATT_WIDTH
D_FF = 4 * D_MODEL
DEEPNORM_ALPHA = (2 * DEPTH) ** 0.25
DEEPNORM_BETA = (8 * DEPTH) ** -0.25
LN_EPS = 1e-5
DT_MIN = 1e-3
DT_MAX = 1e-1

kernel_name = "hymba_s5_retnet_longnet_deepnorm"


def layer_norm(x, w, b):
    xf = x.astype(jnp.float32)
    mu = jnp.mean(xf, axis=-1, keepdims=True)
    var = jnp.mean(jnp.square(xf - mu), axis=-1, keepdims=True)
    y = (xf - mu) * lax.rsqrt(var + LN_EPS)
    return (y * w.astype(jnp.float32) + b.astype(jnp.float32)).astype(x.dtype)


def rms_norm(x, w):
    xf = x.astype(jnp.float32)
    y = xf * lax.rsqrt(jnp.mean(jnp.square(xf), axis=-1, keepdims=True) + LN_EPS)
    return (y * w.astype(jnp.float32)).astype(x.dtype)


def cmul(ar, ai, br, bi):
    return ar * br - ai * bi, ar * bi + ai * br


def s5_mixer(u, lam_re, lam_im, b_re, b_im, c_re, c_im, d_skip, log_dt, w_glu, b_glu):
    f32 = jnp.float32
    bsz, s, _ = u.shape
    uf = u.astype(f32).reshape(bsz, s, SSM_GROUPS, SSM_GROUP)
    lr, li = lam_re.astype(f32), lam_im.astype(f32)
    dt = jnp.exp(log_dt.astype(f32))[:, None]
    mag = jnp.exp(lr * dt)
    a_re, a_im = mag * jnp.cos(li * dt), mag * jnp.sin(li * dt)
    den = lr * lr + li * li
    nr, ni = a_re - 1.0, a_im
    f_re = (nr * lr + ni * li) / den
    f_im = (ni * lr - nr * li) / den
    br, bi = b_re.astype(f32), b_im.astype(f32)
    bb_re, bb_im = cmul(f_re[..., None], f_im[..., None], br, bi)
    bu_re = jnp.einsum('bsgh,gph->sbgp', uf, bb_re)
    bu_im = jnp.einsum('bsgh,gph->sbgp', uf, bb_im)
    a_re_t = jnp.broadcast_to(a_re[None, None], (s, 1, SSM_GROUPS, SSM_STATE))
    a_im_t = jnp.broadcast_to(a_im[None, None], (s, 1, SSM_GROUPS, SSM_STATE))

    def combine(e1, e2):
        a1r, a1i, b1r, b1i = e1
        a2r, a2i, b2r, b2i = e2
        ar, ai = cmul(a2r, a2i, a1r, a1i)
        xr, xi = cmul(a2r, a2i, b1r, b1i)
        return ar, ai, xr + b2r, xi + b2i

    _, _, st_re, st_im = lax.associative_scan(combine, (a_re_t, a_im_t, bu_re, bu_im), axis=0)
    y = (jnp.einsum('sbgp,ghp->bsgh', st_re, c_re.astype(f32))
         - jnp.einsum('sbgp,ghp->bsgh', st_im, c_im.astype(f32)))
    y = (y + d_skip.astype(f32).reshape(SSM_GROUPS, SSM_GROUP) * uf).reshape(bsz, s, SSM_WIDTH)
    g = jax.nn.gelu(y)
    out = g * jax.nn.sigmoid(g @ w_glu.astype(f32) + b_glu.astype(f32))
    return out.astype(u.dtype)


def retention_mixer(q, k, v, g, norm_w):
    f32 = jnp.float32
    bsz, s, h, dh = q.shape
    c = RET_CHUNK
    n = s // c
    lg = jnp.log(1.0 - 2.0 ** (-5.0 - jnp.arange(h, dtype=f32)))
    qf = q.astype(f32).reshape(bsz, n, c, h, dh)
    kf = (k.astype(f32) * (dh ** -0.5)).reshape(bsz, n, c, h, dh)
    vf = v.astype(f32).reshape(bsz, n, c, h, dh)
    idx = jnp.arange(c, dtype=f32)
    diff = idx[:, None] - idx[None, :]
    causal = diff >= 0
    dmat = jnp.where(causal[None], jnp.exp(jnp.where(causal, diff, 0.0)[None] * lg[:, None, None]), 0.0)
    scores = jnp.einsum('bnqhd,bnkhd->bnhqk', qf, kf) * dmat
    inner = jnp.einsum('bnhqk,bnkhe->bnqhe', scores, vf)
    zeta = jnp.exp((c - 1.0 - idx)[:, None] * lg[None, :])
    kv = jnp.einsum('bnkhd,bnkhe->nbhde', kf * zeta[None, None, :, :, None], vf)
    g_chunk = jnp.exp(c * lg)[None, :, None, None]

    def step(r, kv_n):
        return g_chunk * r + kv_n, r

    _, r_prev = lax.scan(step, jnp.zeros((bsz, h, dh, dh), f32), kv)
    xi = jnp.exp((idx + 1.0)[:, None] * lg[None, :])
    cross = jnp.einsum('bnqhd,nbhde->bnqhe', qf * xi[None, None, :, :, None], r_prev)
    o = (inner + cross).reshape(bsz, s, h, dh)
    mu = jnp.mean(o, axis=-1, keepdims=True)
    var = jnp.mean(jnp.square(o - mu), axis=-1, keepdims=True)
    o = ((o - mu) * lax.rsqrt(var + LN_EPS)).reshape(bsz, s, h * dh) * norm_w.astype(f32)
    return (jax.nn.silu(g.astype(f32)) * o).astype(q.dtype)


def alibi_slopes(n_heads):
    return 2.0 ** (-8.0 * jnp.arange(1, n_heads + 1, dtype=jnp.float32) / n_heads)


def dilated_branch(q, k, v, window, dilation, slopes):
    f32 = jnp.float32
    bsz, s, h, dh = q.shape
    sub_len = s // dilation
    wsub = window // dilation
    qb = ATT_BLOCK
    nb = -(-sub_len // qb)
    lp = nb * qb

    def to_sub(t):
        t = t.reshape(bsz, sub_len, dilation, h, dh).transpose(0, 2, 3, 1, 4)
        t = jnp.pad(t, ((0, 0), (0, 0), (0, 0), (0, lp - sub_len), (0, 0)))
        return t.reshape(bsz, dilation, h, nb, qb, dh)

    def with_prev(t):
        prev = jnp.concatenate([jnp.zeros_like(t[:, :, :, :1]), t[:, :, :, :-1]], axis=3)
        return jnp.concatenate([prev, t], axis=4)

    qs = to_sub(q)
    ks = with_prev(to_sub(k))
    vs = with_prev(to_sub(v))
    sc = jnp.einsum('brhnqc,brhnkc->brhnqk', qs, ks).astype(f32) * (dh ** -0.5)
    qi = jnp.arange(qb)
    ki = jnp.arange(2 * qb)
    dist = (qb + qi)[:, None] - ki[None, :]
    key_pos = jnp.arange(nb)[:, None] * qb + ki[None, :] - qb
    valid = ((dist >= 0) & (dist <= wsub))[None] & (key_pos >= 0)[:, None, :]
    bias = -slopes[:, None, None, None] * (dilation * dist).astype(f32)[None, None]
    sc = jnp.where(valid, sc + bias, -jnp.inf)
    m = jnp.max(sc, axis=-1, keepdims=True)
    p = jnp.exp(sc - m)
    l = jnp.sum(p, axis=-1)
    o = jnp.einsum('brhnqk,brhnkc->brhnqc', p, vs.astype(f32)) / l[..., None]
    lse = m[..., 0] + jnp.log(l)
    o = o.reshape(bsz, dilation, h, lp, dh)[:, :, :, :sub_len].transpose(0, 3, 1, 2, 4).reshape(bsz, s, h, dh)
    lse = lse.reshape(bsz, dilation, h, lp)[:, :, :, :sub_len].transpose(0, 3, 1, 2).reshape(bsz, s, h)
    return o, lse


def dilated_attention_mixer(q, k, v):
    bsz, s, h, dh = q.shape
    slopes = alibi_slopes(h)
    outs, lses = [], []
    for window, dilation in DILATED_BRANCHES:
        o, lse = dilated_branch(q, k, v, window, dilation, slopes)
        outs.append(o)
        lses.append(lse)
    wts = jax.nn.softmax(jnp.stack(lses, axis=-1), axis=-1)
    o = sum(wts[..., i, None] * outs[i] for i in range(len(outs)))
    return o.reshape(bsz, s, h * dh).astype(q.dtype)


def hybrid_mixer(x, w_in, lam_re, lam_im, b_re, b_im, c_re, c_im, d_skip, log_dt, w_glu, b_glu,
                 ssm_norm, ret_norm, attn_norm, w_out):
    bsz, s, _ = x.shape
    z = x @ w_in
    cuts = np.cumsum([SSM_WIDTH] + [RET_WIDTH] * 4 + [ATT_WIDTH] * 2).tolist()
    u, rq, rk, rv, rg, aq, ak, av = jnp.split(z, cuts, axis=-1)
    y_ssm = rms_norm(s5_mixer(u, lam_re, lam_im, b_re, b_im, c_re, c_im, d_skip, log_dt, w_glu, b_glu), ssm_norm)
    rh = lambda t: t.reshape(bsz, s, RET_HEADS, RET_HEAD_DIM)
    y_ret = retention_mixer(rh(rq), rh(rk), rh(rv), rg, ret_norm)
    ah = lambda t: t.reshape(bsz, s, ATT_HEADS, ATT_HEAD_DIM)
    y_att = rms_norm(dilated_attention_mixer(ah(aq), ah(ak), ah(av)), attn_norm)
    return jnp.concatenate([y_ssm, y_ret, y_att], axis=-1) @ w_out


def setup_inputs(seed: int = 0) -> dict:
    key = jax.random.key(seed)
    ks = jax.random.split(key, 24)
    f32 = jnp.float32
    nrm = lambda k, shp, sc: jax.random.normal(k, shp, f32) * sc
    L, G, P, H = DEPTH, SSM_GROUPS, SSM_STATE, SSM_GROUP
    lam_im = jnp.broadcast_to(math.pi * jnp.arange(P, dtype=f32), (L, G, P)) + nrm(ks[3], (L, G, P), 0.01)
    return {
        "x": nrm(ks[0], (BATCH, SEQ, D_MODEL), 1.0),
        "w_in": nrm(ks[1], (L, D_MODEL, IN_WIDTH), D_MODEL ** -0.5),
        "ssm_lambda_re": -0.5 + nrm(ks[2], (L, G, P), 0.01),
        "ssm_lambda_im": lam_im,
        "ssm_b_re": nrm(ks[4], (L, G, P, H), (2 * H) ** -0.5),
        "ssm_b_im": nrm(ks[5], (L, G, P, H), (2 * H) ** -0.5),
        "ssm_c_re": nrm(ks[6], (L, G, H, P), (2 * P) ** -0.5),
        "ssm_c_im": nrm(ks[7], (L, G, H, P), (2 * P) ** -0.5),
        "ssm_d": nrm(ks[8], (L, SSM_WIDTH), 1.0),
        "ssm_log_dt": jax.random.uniform(ks[9], (L, G), f32, math.log(DT_MIN), math.log(DT_MAX)),
        "ssm_w_glu": nrm(ks[10], (L, SSM_WIDTH, SSM_WIDTH), SSM_WIDTH ** -0.5),
        "ssm_b_glu": nrm(ks[11], (L, SSM_WIDTH), 0.01),
        "ssm_out_norm": 1.0 + nrm(ks[12], (L, SSM_WIDTH), 0.02),
        "ret_out_norm": 1.0 + nrm(ks[13], (L, RET_WIDTH), 0.02),
        "attn_out_norm": 1.0 + nrm(ks[14], (L, ATT_WIDTH), 0.02),
        "w_out": nrm(ks[15], (L, MIX_WIDTH, D_MODEL), MIX_WIDTH ** -0.5 * DEEPNORM_BETA),
        "ln1_w": 1.0 + nrm(ks[16], (L, D_MODEL), 0.02),
        "ln1_b": nrm(ks[17], (L, D_MODEL), 0.02),
        "mlp_w1": nrm(ks[18], (L, D_MODEL, D_FF), D_MODEL ** -0.5),
        "mlp_w2": nrm(ks[19], (L, D_FF, D_MODEL), D_FF ** -0.5 * DEEPNORM_BETA),
        "ln2_w": 1.0 + nrm(ks[20], (L, D_MODEL), 0.02),
        "ln2_b": nrm(ks[21], (L, D_MODEL), 0.02),
    }


def reference(x, w_in, ssm_lambda_re, ssm_lambda_im, ssm_b_re, ssm_b_im, ssm_c_re, ssm_c_im,
              ssm_d, ssm_log_dt, ssm_w_glu, ssm_b_glu, ssm_out_norm, ret_out_norm, attn_out_norm,
              w_out, ln1_w, ln1_b, mlp_w1, mlp_w2, ln2_w, ln2_b):
    for i in range(DEPTH):
        h = hybrid_mixer(x, w_in[i], ssm_lambda_re[i], ssm_lambda_im[i], ssm_b_re[i], ssm_b_im[i],
                         ssm_c_re[i], ssm_c_im[i], ssm_d[i], ssm_log_dt[i], ssm_w_glu[i], ssm_b_glu[i],
                         ssm_out_norm[i], ret_out_norm[i], attn_out_norm[i], w_out[i])
        x = layer_norm(DEEPNORM_ALPHA * x + h, ln1_w[i], ln1_b[i])
        h = jnp.square(jax.nn.relu(x @ mlp_w1[i])) @ mlp_w2[i]
        x = layer_norm(DEEPNORM_ALPHA * x + h, ln2_w[i], ln2_b[i])
    return x
```

```python
import functools
import math

import jax
import jax.numpy as jnp
from jax import lax
from jax.experimental import pallas as pl
from jax.experimental.pallas import tpu as pltpu

F32 = jnp.float32
BF16 = jnp.bfloat16

D_MODEL = 1024
BATCH = 16
SEQ = 2048
DEPTH = 2
SSM_WIDTH = 256
SSM_GROUP = 16
SSM_GROUPS = 16
SSM_STATE = 64
SSM_STATES = SSM_GROUPS * SSM_STATE
RET_HEAD_DIM = 64
RET_WIDTH = 256
RET_CHUNK = 128
ATT_HEAD_DIM = 64
ATT_WIDTH = 512
ATT_HEADS = 8
DILATED_BRANCHES = ((128, 1), (512, 4), (2048, 16))
ATT_BLOCK = 128
IN_WIDTH = SSM_WIDTH + 4 * RET_WIDTH + 3 * ATT_WIDTH
D_FF = 4 * D_MODEL
DEEPNORM_ALPHA = (2 * DEPTH) ** 0.25
LN_EPS = 1e-5

LANES = 128
HEAD_PAIR = LANES // ATT_HEAD_DIM
VMEM_LIMIT_BYTES = 56 * 1024 * 1024
MASKED_SCORE = -1e30

TOKEN_TILE = 512
SSM_TIME_TILE = 32
SSM_COL_CHUNK = 256
FF_CHUNK = 1024
COMBINE_ROWS = 256

assert RET_HEAD_DIM == ATT_HEAD_DIM and HEAD_PAIR == 2
assert all(w // d == ATT_BLOCK for w, d in DILATED_BRANCHES)


def _params(*semantics):
    return pltpu.CompilerParams(dimension_semantics=semantics, vmem_limit_bytes=VMEM_LIMIT_BYTES)


def _sigmoid(x):
    return 1.0 / (1.0 + jnp.exp(-x))


def _layer_norm(r, w, b):
    mu = jnp.mean(r, axis=-1, keepdims=True)
    d = r - mu
    var = jnp.mean(d * d, axis=-1, keepdims=True)
    return d * lax.rsqrt(var + LN_EPS) * w + b


def _in_proj_kernel(x_ref, w_ref, u_ref, zr_ref, za_ref):
    xb = x_ref[...].astype(BF16)

    def proj(lo, hi):
        return jnp.dot(xb, w_ref[:, lo:hi], preferred_element_type=F32).astype(BF16)

    u_ref[...] = proj(0, SSM_WIDTH)
    zr_ref[...] = proj(SSM_WIDTH, SSM_WIDTH + 4 * RET_WIDTH)
    za_ref[...] = proj(SSM_WIDTH + 4 * RET_WIDTH, IN_WIDTH)


def _in_proj(x, w_in):
    tm = TOKEN_TILE
    row = lambda width: pl.BlockSpec((None, tm, width), lambda b, i: (b, i, 0))
    return pl.pallas_call(
        _in_proj_kernel,
        grid=(BATCH, SEQ // tm),
        in_specs=[row(D_MODEL), pl.BlockSpec((D_MODEL, IN_WIDTH), lambda b, i: (0, 0))],
        out_specs=[row(SSM_WIDTH), row(4 * RET_WIDTH), row(3 * ATT_WIDTH)],
        out_shape=[jax.ShapeDtypeStruct((BATCH, SEQ, SSM_WIDTH), BF16),
                   jax.ShapeDtypeStruct((BATCH, SEQ, 4 * RET_WIDTH), BF16),
                   jax.ShapeDtypeStruct((BATCH, SEQ, 3 * ATT_WIDTH), BF16)],
        compiler_params=_params("parallel", "parallel"),
        name="in_proj",
    )(x, w_in)


def _ssm_kernel(u_ref, wb_ref, are_ref, aim_ref, wc_ref, d_ref, wglu_ref, bglu_ref, nw_ref,
                y_ref, bu_ref, st_ref):
    n_st = SSM_STATES

    @pl.when(pl.program_id(0) == 0)
    def _():
        st_ref[...] = jnp.zeros_like(st_ref)

    u = u_ref[...]
    bu_ref[...] = jnp.dot(u, wb_ref[...], preferred_element_type=F32)

    for c in range(0, n_st, SSM_COL_CHUNK):
        re = slice(c, c + SSM_COL_CHUNK)
        im = slice(n_st + c, n_st + c + SSM_COL_CHUNK)
        ar = jnp.broadcast_to(are_ref[:, re], (BATCH, SSM_COL_CHUNK))
        ai = jnp.broadcast_to(aim_ref[:, re], (BATCH, SSM_COL_CHUNK))

        def step(t, carry, re=re, im=im, ar=ar, ai=ai):
            xr, xi = carry
            rows = pl.ds(pl.multiple_of(t * BATCH, BATCH), BATCH)
            nr = ar * xr - ai * xi + bu_ref[rows, re]
            ni = ar * xi + ai * xr + bu_ref[rows, im]
            bu_ref[rows, re] = nr
            bu_ref[rows, im] = ni
            return nr, ni

        xr, xi = lax.fori_loop(0, SSM_TIME_TILE, step, (st_ref[:, re], st_ref[:, im]), unroll=4)
        st_ref[:, re] = xr
        st_ref[:, im] = xi

    y = jnp.dot(bu_ref[...].astype(BF16), wc_ref[...], preferred_element_type=F32)
    y = y + d_ref[...] * u.astype(F32)
    cdf = 0.5 * (1.0 + jnp.tanh(math.sqrt(2.0 / math.pi) * (y + 0.044715 * (y * y * y))))
    g = y * cdf
    gate = jnp.dot(g.astype(BF16), wglu_ref[...], preferred_element_type=F32) + bglu_ref[...]
    out = g * _sigmoid(gate)
    ms = jnp.mean(out * out, axis=-1, keepdims=True)
    y_ref[...] = (out * lax.rsqrt(ms + LN_EPS) * nw_ref[...]).astype(BF16)


def _ssm(u_tm, wb, a_re, a_im, wc, d_skip, w_glu, b_glu, norm_w):
    rows = SSM_TIME_TILE * BATCH
    full = lambda shape: pl.BlockSpec(shape, lambda i: (0, 0))
    return pl.pallas_call(
        _ssm_kernel,
        grid=(SEQ // SSM_TIME_TILE,),
        in_specs=[pl.BlockSpec((rows, SSM_WIDTH), lambda i: (i, 0)),
                  full((SSM_WIDTH, 2 * SSM_STATES)), full((1, SSM_STATES)), full((1, SSM_STATES)),
                  full((2 * SSM_STATES, SSM_WIDTH)), full((1, SSM_WIDTH)),
                  full((SSM_WIDTH, SSM_WIDTH)), full((1, SSM_WIDTH)), full((1, SSM_WIDTH))],
        out_specs=pl.BlockSpec((rows, SSM_WIDTH), lambda i: (i, 0)),
        out_shape=jax.ShapeDtypeStruct((SEQ * BATCH, SSM_WIDTH), BF16),
        scratch_shapes=[pltpu.VMEM((rows, 2 * SSM_STATES), F32),
                        pltpu.VMEM((BATCH, 2 * SSM_STATES), F32)],
        compiler_params=_params("arbitrary"),
        name="s5_mixer",
    )(u_tm, wb, a_re, a_im, wc, d_skip, w_glu, b_glu, norm_w)


def _ssm_weights(lam_re, lam_im, b_re, b_im, c_re, c_im, log_dt):
    g, p, h = SSM_GROUPS, SSM_STATE, SSM_GROUP
    lr, li = lam_re.astype(F32), lam_im.astype(F32)
    dt = jnp.exp(log_dt.astype(F32))[:, None]
    mag = jnp.exp(lr * dt)
    a_re, a_im = mag * jnp.cos(li * dt), mag * jnp.sin(li * dt)
    den = lr * lr + li * li
    nr, ni = a_re - 1.0, a_im
    f_re = ((nr * lr + ni * li) / den)[..., None]
    f_im = ((ni * lr - nr * li) / den)[..., None]
    br, bi = b_re.astype(F32), b_im.astype(F32)
    bb_re = f_re * br - f_im * bi
    bb_im = f_re * bi + f_im * br
    eye = jnp.eye(g, dtype=F32)
    embed_b = lambda bb: jnp.einsum('gph,gk->ghkp', bb, eye).reshape(g * h, g * p)
    embed_c = lambda cc: jnp.einsum('ghp,gk->gpkh', cc, eye).reshape(g * p, g * h)
    wb = jnp.concatenate([embed_b(bb_re), embed_b(bb_im)], axis=1).astype(BF16)
    wc = jnp.concatenate([embed_c(c_re.astype(F32)), -embed_c(c_im.astype(F32))], axis=0).astype(BF16)
    return wb, a_re.reshape(1, g * p), a_im.reshape(1, g * p), wc


def _ret_kernel(q_ref, k_ref, v_ref, g_ref, nw_ref, y_ref, dmat_ref, r_ref):
    c = RET_CHUNK
    dh = RET_HEAD_DIM
    hp = pl.program_id(1)
    lane = lax.broadcasted_iota(jnp.int32, (c, LANES), 1)
    row = lax.broadcasted_iota(jnp.int32, (c, LANES), 0)
    head1 = lane >= dh
    head_of_lane = (HEAD_PAIR * hp).astype(F32) + head1.astype(F32)
    lg = jnp.log(1.0 - jnp.exp2(-5.0 - head_of_lane))
    rowf = row.astype(F32)
    xi = jnp.exp((rowf + 1.0) * lg)
    zeta = jnp.exp((c - 1.0 - rowf) * lg) * (dh ** -0.5)
    g_chunk = jnp.exp(c * lg)
    block_diag = (row >= dh) == head1
    diff = (row - lane).astype(F32)
    for j in range(HEAD_PAIR):
        lg_j = jnp.log(1.0 - jnp.exp2(jnp.zeros((c, c), F32) - 5.0 - (HEAD_PAIR * hp + j).astype(F32)))
        dmat_ref[j] = jnp.where(diff >= 0, jnp.exp(jnp.maximum(diff, 0.0) * lg_j), 0.0) * (dh ** -0.5)
    r_ref[...] = jnp.zeros_like(r_ref)
    nw = nw_ref[...]

    def chunk(n, carry):
        rows = pl.ds(pl.multiple_of(n * c, c), c)
        q, k, v = q_ref[rows, :], k_ref[rows, :], v_ref[rows, :]
        o = jnp.zeros((c, LANES), F32)
        for j in range(HEAD_PAIR):
            mine = head1 if j else jnp.logical_not(head1)
            qj = jnp.where(mine, q, jnp.zeros_like(q))
            s = lax.dot_general(qj, k, (((1,), (1,)), ((), ())), preferred_element_type=F32)
            s = s * dmat_ref[j]
            vj = jnp.where(mine, v, jnp.zeros_like(v))
            o = o + jnp.dot(s.astype(BF16), vj, preferred_element_type=F32)
        r_prev = r_ref[...]
        qx = (q.astype(F32) * xi).astype(BF16)
        o = o + jnp.dot(qx, r_prev.astype(BF16), preferred_element_type=F32)
        kz = (k.astype(F32) * zeta).T.astype(BF16)
        kv = jnp.dot(kz, v, preferred_element_type=F32)
        r_ref[...] = jnp.where(block_diag, g_chunk * r_prev + kv, 0.0)

        def head_mean(t):
            s0 = jnp.sum(jnp.where(head1, 0.0, t), axis=-1, keepdims=True)
            s1 = jnp.sum(jnp.where(head1, t, 0.0), axis=-1, keepdims=True)
            return jnp.where(head1, s1, s0) * (1.0 / dh)

        dlt = o - head_mean(o)
        var = head_mean(dlt * dlt)
        gate = g_ref[rows, :].astype(F32)
        y = dlt * lax.rsqrt(var + LN_EPS) * nw * (gate * _sigmoid(gate))
        y_ref[rows, :] = y.astype(BF16)
        return carry

    lax.fori_loop(0, SEQ // c, chunk, 0)


def _retention(zr, norm_w):
    col = lambda off: pl.BlockSpec((None, SEQ, LANES), lambda b, hp: (b, 0, hp + off))
    pairs = RET_WIDTH // LANES
    return pl.pallas_call(
        _ret_kernel,
        grid=(BATCH, pairs),
        in_specs=[col(0), col(pairs), col(2 * pairs), col(3 * pairs),
                  pl.BlockSpec((1, LANES), lambda b, hp: (0, hp))],
        out_specs=col(0),
        out_shape=jax.ShapeDtypeStruct((BATCH, SEQ, RET_WIDTH), BF16),
        scratch_shapes=[pltpu.VMEM((HEAD_PAIR, RET_CHUNK, RET_CHUNK), F32),
                        pltpu.VMEM((LANES, LANES), F32)],
        compiler_params=_params("parallel", "parallel"),
        name="retention",
    )(zr, zr, zr, zr, norm_w)


def _attn_kernel(q_ref, k_ref, v_ref, y_ref, src_ref, dq_ref, dk_ref, dv_ref,
                 oacc_ref, macc_ref, bias_ref, bias0_ref):
    qb = ATT_BLOCK
    dh = ATT_HEAD_DIM
    hp = pl.program_id(1)
    lane = lax.broadcasted_iota(jnp.int32, (qb, LANES), 1)
    head1 = lane >= dh
    head1_2 = lax.broadcasted_iota(jnp.int32, (2 * qb, LANES), 1) >= dh

    srcs = (q_ref, k_ref, v_ref)
    dsts = (dq_ref, dk_ref, dv_ref)
    for a in range(3):
        src_ref[a] = srcs[a][...].astype(F32)
    for bi in range(1, len(DILATED_BRANCHES)):
        d = DILATED_BRANCHES[bi][1]
        sub = SEQ // d
        for a in range(3):
            for r in range(d):
                dsts[a][bi - 1, r * sub:(r + 1) * sub, :] = (
                    src_ref.at[a][pl.ds(r, sub, stride=d), :].astype(BF16))

    qi2 = lax.broadcasted_iota(jnp.int32, (qb, 2 * qb), 0)
    ki2 = lax.broadcasted_iota(jnp.int32, (qb, 2 * qb), 1)
    dist2 = qb + qi2 - ki2
    qi1 = lax.broadcasted_iota(jnp.int32, (qb, qb), 0)
    ki1 = lax.broadcasted_iota(jnp.int32, (qb, qb), 1)
    dist1 = qi1 - ki1

    for j in range(HEAD_PAIR):
        mine = head1 if j else jnp.logical_not(head1)
        mine2 = head1_2 if j else jnp.logical_not(head1_2)
        head = (HEAD_PAIR * hp + j).astype(F32)
        for bi, (window, d) in enumerate(DILATED_BRANCHES):
            sub = SEQ // d
            nb = sub // qb
            qsrc = q_ref if bi == 0 else dq_ref.at[bi - 1]
            ksrc = k_ref if bi == 0 else dk_ref.at[bi - 1]
            vsrc = v_ref if bi == 0 else dv_ref.at[bi - 1]
            slope2 = jnp.exp2(jnp.zeros((qb, 2 * qb), F32) - (8.0 / ATT_HEADS) * (head + 1.0))
            bias_ref[...] = jnp.where((dist2 >= 0) & (dist2 <= window // d),
                                      -slope2 * (d * dist2).astype(F32), MASKED_SCORE)
            slope1 = jnp.exp2(jnp.zeros((qb, qb), F32) - (8.0 / ATT_HEADS) * (head + 1.0))
            bias0_ref[...] = jnp.where(dist1 >= 0, -slope1 * (d * dist1).astype(F32), MASKED_SCORE)

            def attend(q, k, v, bias, mine_k):
                qj = jnp.where(mine, q, jnp.zeros_like(q)) * jnp.asarray(dh ** -0.5, BF16)
                s = lax.dot_general(qj, k, (((1,), (1,)), ((), ())), preferred_element_type=F32)
                s = s + bias
                m = jnp.max(s, axis=-1, keepdims=True)
                p = jnp.exp(s - m)
                vj = jnp.where(mine_k, v, jnp.ones_like(v))
                o = jnp.dot(p.astype(BF16), vj, preferred_element_type=F32)
                return o, jnp.broadcast_to(m, (qb, LANES))

            def store(o, m, rows, first_token, j=j, bi=bi, d=d):
                if d > 1:
                    rows = pl.ds(first_token, qb, stride=d)
                oacc_ref.at[j, bi][rows, :] = o
                macc_ref.at[j, bi][rows, :] = m

            def first_block(r, carry, sub=sub, qsrc=qsrc, ksrc=ksrc, vsrc=vsrc):
                rows = pl.ds(pl.multiple_of(r * sub, qb), qb)
                o, m = attend(qsrc[rows, :], ksrc[rows, :], vsrc[rows, :], bias0_ref[...], mine)
                store(o, m, rows, r)
                return carry

            lax.fori_loop(0, d, first_block, 0)

            if nb > 1:
                def later_block(idx, carry, sub=sub, nb=nb, qsrc=qsrc, ksrc=ksrc, vsrc=vsrc):
                    r = lax.div(idx, nb - 1)
                    n = lax.rem(idx, nb - 1) + 1
                    base = r * sub + n * qb
                    rows = pl.ds(pl.multiple_of(base, qb), qb)
                    krows = pl.ds(pl.multiple_of(base - qb, qb), 2 * qb)
                    o, m = attend(qsrc[rows, :], ksrc[krows, :], vsrc[krows, :], bias_ref[...], mine2)
                    store(o, m, rows, r + d * qb * n)
                    return carry

                lax.fori_loop(0, d * (nb - 1), later_block, 0)

    lane_c = lax.broadcasted_iota(jnp.int32, (COMBINE_ROWS, LANES), 1)
    head1_c = lane_c >= dh

    def combine(tb, carry):
        rows = pl.ds(pl.multiple_of(tb * COMBINE_ROWS, COMBINE_ROWS), COMBINE_ROWS)
        res = []
        for j in range(HEAD_PAIR):
            ms = [macc_ref[j, bi, rows, :] for bi in range(len(DILATED_BRANCHES))]
            top = functools.reduce(jnp.maximum, ms)
            num = sum(jnp.exp(ms[bi] - top) * oacc_ref[j, bi, rows, :]
                      for bi in range(len(DILATED_BRANCHES)))
            res.append(num / pltpu.roll(num, dh, axis=1))
        y_ref[rows, :] = jnp.where(head1_c, res[1], res[0]).astype(BF16)
        return carry

    lax.fori_loop(0, SEQ // COMBINE_ROWS, combine, 0)


def _attention(za):
    pairs = ATT_WIDTH // LANES
    col = lambda off: pl.BlockSpec((None, SEQ, LANES), lambda b, hp: (b, 0, hp + off))
    n_dil = len(DILATED_BRANCHES) - 1
    return pl.pallas_call(
        _attn_kernel,
        grid=(BATCH, pairs),
        in_specs=[col(0), col(pairs), col(2 * pairs)],
        out_specs=col(0),
        out_shape=jax.ShapeDtypeStruct((BATCH, SEQ, ATT_WIDTH), BF16),
        scratch_shapes=[pltpu.VMEM((3, SEQ, LANES), F32),
                        pltpu.VMEM((n_dil, SEQ, LANES), BF16),
                        pltpu.VMEM((n_dil, SEQ, LANES), BF16),
                        pltpu.VMEM((n_dil, SEQ, LANES), BF16),
                        pltpu.VMEM((HEAD_PAIR, n_dil + 1, SEQ, LANES), F32),
                        pltpu.VMEM((HEAD_PAIR, n_dil + 1, SEQ, LANES), F32),
                        pltpu.VMEM((ATT_BLOCK, 2 * ATT_BLOCK), F32),
                        pltpu.VMEM((ATT_BLOCK, ATT_BLOCK), F32)],
        compiler_params=_params("parallel", "parallel"),
        name="dilated_attention",
    )(za, za, za)


def _out_proj_kernel(ys_ref, yr_ref, ya_ref, x_ref, w_ref, an_ref, lw_ref, lb_ref, o_ref):
    ya = ya_ref[...].astype(F32)
    ms = jnp.mean(ya * ya, axis=-1, keepdims=True)
    ya = (ya * lax.rsqrt(ms + LN_EPS) * an_ref[...]).astype(BF16)
    h = jnp.dot(ys_ref[...], w_ref[0:SSM_WIDTH, :], preferred_element_type=F32)
    h = h + jnp.dot(yr_ref[...], w_ref[SSM_WIDTH:SSM_WIDTH + RET_WIDTH, :], preferred_element_type=F32)
    h = h + jnp.dot(ya, w_ref[SSM_WIDTH + RET_WIDTH:, :], preferred_element_type=F32)
    o_ref[...] = _layer_norm(DEEPNORM_ALPHA * x_ref[...] + h, lw_ref[...], lb_ref[...])


def _out_proj(y_ssm, y_ret, y_att, x, w_out, attn_norm, ln_w, ln_b):
    tm = TOKEN_TILE
    row = lambda width: pl.BlockSpec((None, tm, width), lambda b, i: (b, i, 0))
    full = lambda shape: pl.BlockSpec(shape, lambda b, i: (0, 0))
    return pl.pallas_call(
        _out_proj_kernel,
        grid=(BATCH, SEQ // tm),
        in_specs=[row(SSM_WIDTH), row(RET_WIDTH), row(ATT_WIDTH), row(D_MODEL),
                  full((D_MODEL, D_MODEL)), full((1, ATT_WIDTH)), full((1, D_MODEL)), full((1, D_MODEL))],
        out_specs=row(D_MODEL),
        out_shape=jax.ShapeDtypeStruct((BATCH, SEQ, D_MODEL), F32),
        compiler_params=_params("parallel", "parallel"),
        name="out_proj_ln",
    )(y_ssm, y_ret, y_att, x, w_out, attn_norm, ln_w, ln_b)


def _mlp_kernel(x_ref, w1_ref, w2_ref, lw_ref, lb_ref, o_ref):
    x = x_ref[...]
    xb = x.astype(BF16)
    acc = jnp.zeros(x.shape, F32)
    for c in range(0, D_FF, FF_CHUNK):
        h = jnp.dot(xb, w1_ref[:, c:c + FF_CHUNK], preferred_element_type=F32)
        h = jnp.square(jnp.maximum(h, 0.0)).astype(BF16)
        acc = acc + jnp.dot(h, w2_ref[c:c + FF_CHUNK, :], preferred_element_type=F32)
    o_ref[...] = _layer_norm(DEEPNORM_ALPHA * x + acc, lw_ref[...], lb_ref[...])


def _mlp(x, w1, w2, ln_w, ln_b):
    tm = TOKEN_TILE
    row = pl.BlockSpec((None, tm, D_MODEL), lambda b, i: (b, i, 0))
    full = lambda shape: pl.BlockSpec(shape, lambda b, i: (0, 0))
    return pl.pallas_call(
        _mlp_kernel,
        grid=(BATCH, SEQ // tm),
        in_specs=[row, full((D_MODEL, D_FF)), full((D_FF, D_MODEL)), full((1, D_MODEL)), full((1, D_MODEL))],
        out_specs=row,
        out_shape=jax.ShapeDtypeStruct((BATCH, SEQ, D_MODEL), F32),
        compiler_params=_params("parallel", "parallel"),
        name="mlp_ln",
    )(x, w1, w2, ln_w, ln_b)


def kernel(x, w_in, ssm_lambda_re, ssm_lambda_im, ssm_b_re, ssm_b_im, ssm_c_re, ssm_c_im, ssm_d, ssm_log_dt, ssm_w_glu, ssm_b_glu, ssm_out_norm, ret_out_norm, attn_out_norm, w_out, ln1_w, ln1_b, mlp_w1, mlp_w2, ln2_w, ln2_b):
    vec = lambda p: p.astype(F32).reshape(1, -1)
    x = x.astype(F32)
    for i in range(DEPTH):
        u, zr, za = _in_proj(x, w_in[i].astype(BF16))
        wb, a_re, a_im, wc = _ssm_weights(ssm_lambda_re[i], ssm_lambda_im[i], ssm_b_re[i], ssm_b_im[i],
                                          ssm_c_re[i], ssm_c_im[i], ssm_log_dt[i])
        u_tm = jnp.transpose(u, (1, 0, 2)).reshape(SEQ * BATCH, SSM_WIDTH)
        y_ssm = _ssm(u_tm, wb, a_re, a_im, wc, vec(ssm_d[i]), ssm_w_glu[i].astype(BF16),
                     vec(ssm_b_glu[i]), vec(ssm_out_norm[i]))
        y_ssm = jnp.transpose(y_ssm.reshape(SEQ, BATCH, SSM_WIDTH), (1, 0, 2))
        y_ret = _retention(zr, vec(ret_out_norm[i]))
        y_att = _attention(za)
        x = _out_proj(y_ssm, y_ret, y_att, x, w_out[i].astype(BF16), vec(attn_out_norm[i]),
                      vec(ln1_w[i]), vec(ln1_b[i]))
        x = _mlp(x, mlp_w1[i].astype(BF16), mlp_w2[i].astype(BF16), vec(ln2_w[i]), vec(ln2_b[i]))
    return x
```

```python
import functools
import math

import jax
import jax.numpy as jnp
from jax import lax
from jax.experimental import pallas as pl
from jax.experimental.pallas import tpu as pltpu

F32 = jnp.float32
BF16 = jnp.bfloat16

D_MODEL = 1024
BATCH = 16
SEQ = 2048
DEPTH = 2
SSM_WIDTH = 256
SSM_GROUP = 16
SSM_GROUPS = 16
SSM_STATE = 64
SSM_STATES = SSM_GROUPS * SSM_STATE
RET_HEAD_DIM = 64
RET_WIDTH = 256
RET_CHUNK = 128
ATT_HEAD_DIM = 64
ATT_WIDTH = 512
ATT_HEADS = 8
DILATED_BRANCHES = ((128, 1), (512, 4), (2048, 16))
ATT_BLOCK = 128
IN_WIDTH = SSM_WIDTH + 4 * RET_WIDTH + 3 * ATT_WIDTH
D_FF = 4 * D_MODEL
DEEPNORM_ALPHA = (2 * DEPTH) ** 0.25
LN_EPS = 1e-5

LANES = 128
HEAD_PAIR = LANES // ATT_HEAD_DIM
VMEM_LIMIT_BYTES = 56 * 1024 * 1024
MASKED_SCORE = -1e30

TOKEN_TILE = 512
SSM_TIME_TILE = 32
SSM_COL_CHUNK = 256
FF_CHUNK = 1024
COMBINE_ROWS = 256
ATT_FIRST_UNROLL = 4
ATT_LATER_UNROLL = 3

assert RET_HEAD_DIM == ATT_HEAD_DIM and HEAD_PAIR == 2
assert all(w // d == ATT_BLOCK for w, d in DILATED_BRANCHES)


def _params(*semantics):
    return pltpu.CompilerParams(dimension_semantics=semantics, vmem_limit_bytes=VMEM_LIMIT_BYTES)


def _sigmoid(x):
    return 1.0 / (1.0 + jnp.exp(-x))


def _layer_norm(r, w, b):
    mu = jnp.mean(r, axis=-1, keepdims=True)
    d = r - mu
    var = jnp.mean(d * d, axis=-1, keepdims=True)
    return d * lax.rsqrt(var + LN_EPS) * w + b


def _in_proj_kernel(x_ref, w_ref, u_ref, zr_ref, za_ref):
    xb = x_ref[...].astype(BF16)

    def proj(lo, hi):
        return jnp.dot(xb, w_ref[:, lo:hi], preferred_element_type=F32).astype(BF16)

    u_ref[...] = proj(0, SSM_WIDTH)
    zr_ref[...] = proj(SSM_WIDTH, SSM_WIDTH + 4 * RET_WIDTH)
    za_ref[...] = proj(SSM_WIDTH + 4 * RET_WIDTH, IN_WIDTH)


def _in_proj(x, w_in):
    tm = TOKEN_TILE
    row = lambda width: pl.BlockSpec((None, tm, width), lambda b, i: (b, i, 0))
    return pl.pallas_call(
        _in_proj_kernel,
        grid=(BATCH, SEQ // tm),
        in_specs=[row(D_MODEL), pl.BlockSpec((D_MODEL, IN_WIDTH), lambda b, i: (0, 0))],
        out_specs=[row(SSM_WIDTH), row(4 * RET_WIDTH), row(3 * ATT_WIDTH)],
        out_shape=[jax.ShapeDtypeStruct((BATCH, SEQ, SSM_WIDTH), BF16),
                   jax.ShapeDtypeStruct((BATCH, SEQ, 4 * RET_WIDTH), BF16),
                   jax.ShapeDtypeStruct((BATCH, SEQ, 3 * ATT_WIDTH), BF16)],
        compiler_params=_params("parallel", "parallel"),
        name="in_proj",
    )(x, w_in)


def _ssm_kernel(u_ref, wb_ref, are_ref, aim_ref, wc_ref, d_ref, wglu_ref, bglu_ref, nw_ref,
                y_ref, bu_ref, st_ref):
    n_st = SSM_STATES

    @pl.when(pl.program_id(0) == 0)
    def _():
        st_ref[...] = jnp.zeros_like(st_ref)

    u = u_ref[...]
    bu_ref[...] = jnp.dot(u, wb_ref[...], preferred_element_type=F32)

    for c in range(0, n_st, SSM_COL_CHUNK):
        re = slice(c, c + SSM_COL_CHUNK)
        im = slice(n_st + c, n_st + c + SSM_COL_CHUNK)
        ar = jnp.broadcast_to(are_ref[:, re], (BATCH, SSM_COL_CHUNK))
        ai = jnp.broadcast_to(aim_ref[:, re], (BATCH, SSM_COL_CHUNK))

        def step(t, carry, re=re, im=im, ar=ar, ai=ai):
            xr, xi = carry
            rows = pl.ds(pl.multiple_of(t * BATCH, BATCH), BATCH)
            nr = ar * xr - ai * xi + bu_ref[rows, re]
            ni = ar * xi + ai * xr + bu_ref[rows, im]
            bu_ref[rows, re] = nr
            bu_ref[rows, im] = ni
            return nr, ni

        xr, xi = lax.fori_loop(0, SSM_TIME_TILE, step, (st_ref[:, re], st_ref[:, im]), unroll=4)
        st_ref[:, re] = xr
        st_ref[:, im] = xi

    y = jnp.dot(bu_ref[...].astype(BF16), wc_ref[...], preferred_element_type=F32)
    y = y + d_ref[...] * u.astype(F32)
    cdf = 0.5 * (1.0 + jnp.tanh(math.sqrt(2.0 / math.pi) * (y + 0.044715 * (y * y * y))))
    g = y * cdf
    gate = jnp.dot(g.astype(BF16), wglu_ref[...], preferred_element_type=F32) + bglu_ref[...]
    out = g * _sigmoid(gate)
    ms = jnp.mean(out * out, axis=-1, keepdims=True)
    y_ref[...] = (out * lax.rsqrt(ms + LN_EPS) * nw_ref[...]).astype(BF16)


def _ssm(u_tm, wb, a_re, a_im, wc, d_skip, w_glu, b_glu, norm_w):
    rows = SSM_TIME_TILE * BATCH
    full = lambda shape: pl.BlockSpec(shape, lambda i: (0, 0))
    return pl.pallas_call(
        _ssm_kernel,
        grid=(SEQ // SSM_TIME_TILE,),
        in_specs=[pl.BlockSpec((rows, SSM_WIDTH), lambda i: (i, 0)),
                  full((SSM_WIDTH, 2 * SSM_STATES)), full((1, SSM_STATES)), full((1, SSM_STATES)),
                  full((2 * SSM_STATES, SSM_WIDTH)), full((1, SSM_WIDTH)),
                  full((SSM_WIDTH, SSM_WIDTH)), full((1, SSM_WIDTH)), full((1, SSM_WIDTH))],
        out_specs=pl.BlockSpec((rows, SSM_WIDTH), lambda i: (i, 0)),
        out_shape=jax.ShapeDtypeStruct((SEQ * BATCH, SSM_WIDTH), BF16),
        scratch_shapes=[pltpu.VMEM((rows, 2 * SSM_STATES), F32),
                        pltpu.VMEM((BATCH, 2 * SSM_STATES), F32)],
        compiler_params=_params("arbitrary"),
        name="s5_mixer",
    )(u_tm, wb, a_re, a_im, wc, d_skip, w_glu, b_glu, norm_w)


def _ssm_weights(lam_re, lam_im, b_re, b_im, c_re, c_im, log_dt):
    g, p, h = SSM_GROUPS, SSM_STATE, SSM_GROUP
    lr, li = lam_re.astype(F32), lam_im.astype(F32)
    dt = jnp.exp(log_dt.astype(F32))[:, None]
    mag = jnp.exp(lr * dt)
    a_re, a_im = mag * jnp.cos(li * dt), mag * jnp.sin(li * dt)
    den = lr * lr + li * li
    nr, ni = a_re - 1.0, a_im
    f_re = ((nr * lr + ni * li) / den)[..., None]
    f_im = ((ni * lr - nr * li) / den)[..., None]
    br, bi = b_re.astype(F32), b_im.astype(F32)
    bb_re = f_re * br - f_im * bi
    bb_im = f_re * bi + f_im * br
    eye = jnp.eye(g, dtype=F32)
    embed_b = lambda bb: jnp.einsum('gph,gk->ghkp', bb, eye).reshape(g * h, g * p)
    embed_c = lambda cc: jnp.einsum('ghp,gk->gpkh', cc, eye).reshape(g * p, g * h)
    wb = jnp.concatenate([embed_b(bb_re), embed_b(bb_im)], axis=1).astype(BF16)
    wc = jnp.concatenate([embed_c(c_re.astype(F32)), -embed_c(c_im.astype(F32))], axis=0).astype(BF16)
    return wb, a_re.reshape(1, g * p), a_im.reshape(1, g * p), wc


def _ret_kernel(q_ref, k_ref, v_ref, g_ref, nw_ref, y_ref, dmat_ref, r_ref):
    c = RET_CHUNK
    dh = RET_HEAD_DIM
    hp = pl.program_id(1)
    lane = lax.broadcasted_iota(jnp.int32, (c, LANES), 1)
    row = lax.broadcasted_iota(jnp.int32, (c, LANES), 0)
    head1 = lane >= dh
    head_of_lane = (HEAD_PAIR * hp).astype(F32) + head1.astype(F32)
    lg = jnp.log(1.0 - jnp.exp2(-5.0 - head_of_lane))
    rowf = row.astype(F32)
    xi = jnp.exp((rowf + 1.0) * lg)
    zeta = jnp.exp((c - 1.0 - rowf) * lg) * (dh ** -0.5)
    g_chunk = jnp.exp(c * lg)
    block_diag = (row >= dh) == head1
    diff = (row - lane).astype(F32)
    for j in range(HEAD_PAIR):
        lg_j = jnp.log(1.0 - jnp.exp2(jnp.zeros((c, c), F32) - 5.0 - (HEAD_PAIR * hp + j).astype(F32)))
        dmat_ref[j] = jnp.where(diff >= 0, jnp.exp(jnp.maximum(diff, 0.0) * lg_j), 0.0) * (dh ** -0.5)
    r_ref[...] = jnp.zeros_like(r_ref)
    nw = nw_ref[...]

    def chunk(n, carry):
        rows = pl.ds(pl.multiple_of(n * c, c), c)
        q, k, v = q_ref[rows, :], k_ref[rows, :], v_ref[rows, :]
        o = jnp.zeros((c, LANES), F32)
        for j in range(HEAD_PAIR):
            mine = head1 if j else jnp.logical_not(head1)
            qj = jnp.where(mine, q, jnp.zeros_like(q))
            s = lax.dot_general(qj, k, (((1,), (1,)), ((), ())), preferred_element_type=F32)
            s = s * dmat_ref[j]
            vj = jnp.where(mine, v, jnp.zeros_like(v))
            o = o + jnp.dot(s.astype(BF16), vj, preferred_element_type=F32)
        r_prev = r_ref[...]
        qx = (q.astype(F32) * xi).astype(BF16)
        o = o + jnp.dot(qx, r_prev.astype(BF16), preferred_element_type=F32)
        kz = (k.astype(F32) * zeta).T.astype(BF16)
        kv = jnp.dot(kz, v, preferred_element_type=F32)
        r_ref[...] = jnp.where(block_diag, g_chunk * r_prev + kv, 0.0)

        def head_mean(t):
            s0 = jnp.sum(jnp.where(head1, 0.0, t), axis=-1, keepdims=True)
            s1 = jnp.sum(jnp.where(head1, t, 0.0), axis=-1, keepdims=True)
            return jnp.where(head1, s1, s0) * (1.0 / dh)

        dlt = o - head_mean(o)
        var = head_mean(dlt * dlt)
        gate = g_ref[rows, :].astype(F32)
        y = dlt * lax.rsqrt(var + LN_EPS) * nw * (gate * _sigmoid(gate))
        y_ref[rows, :] = y.astype(BF16)
        return carry

    lax.fori_loop(0, SEQ // c, chunk, 0)


def _retention(zr, norm_w):
    col = lambda off: pl.BlockSpec((None, SEQ, LANES), lambda b, hp: (b, 0, hp + off))
    pairs = RET_WIDTH // LANES
    return pl.pallas_call(
        _ret_kernel,
        grid=(BATCH, pairs),
        in_specs=[col(0), col(pairs), col(2 * pairs), col(3 * pairs),
                  pl.BlockSpec((1, LANES), lambda b, hp: (0, hp))],
        out_specs=col(0),
        out_shape=jax.ShapeDtypeStruct((BATCH, SEQ, RET_WIDTH), BF16),
        scratch_shapes=[pltpu.VMEM((HEAD_PAIR, RET_CHUNK, RET_CHUNK), F32),
                        pltpu.VMEM((LANES, LANES), F32)],
        compiler_params=_params("parallel", "parallel"),
        name="retention",
    )(zr, zr, zr, zr, norm_w)


def _attn_kernel(q_ref, k_ref, v_ref, y_ref, src_ref, dq_ref, dk_ref, dv_ref,
                 oacc_ref, macc_ref, bias_ref, bias0_ref):
    qb = ATT_BLOCK
    dh = ATT_HEAD_DIM
    hp = pl.program_id(1)
    lane = lax.broadcasted_iota(jnp.int32, (qb, LANES), 1)
    head1 = lane >= dh
    head1_2 = lax.broadcasted_iota(jnp.int32, (2 * qb, LANES), 1) >= dh

    srcs = (q_ref, k_ref, v_ref)
    dsts = (dq_ref, dk_ref, dv_ref)
    for a in range(3):
        src_ref[a] = srcs[a][...].astype(F32)
    for bi in range(1, len(DILATED_BRANCHES)):
        d = DILATED_BRANCHES[bi][1]
        sub = SEQ // d
        for a in range(3):
            for r in range(d):
                dsts[a][bi - 1, r * sub:(r + 1) * sub, :] = (
                    src_ref.at[a][pl.ds(r, sub, stride=d), :].astype(BF16))

    row2 = lax.broadcasted_iota(jnp.int32, (HEAD_PAIR * qb, 2 * qb), 0)
    key2 = lax.broadcasted_iota(jnp.int32, (HEAD_PAIR * qb, 2 * qb), 1)
    row1 = lax.broadcasted_iota(jnp.int32, (HEAD_PAIR * qb, qb), 0)
    key1 = lax.broadcasted_iota(jnp.int32, (HEAD_PAIR * qb, qb), 1)

    def head_slope(row):
        head = (HEAD_PAIR * hp).astype(F32) + (row >= qb).astype(F32)
        return jnp.exp2(-(8.0 / ATT_HEADS) * (head + 1.0))

    dist2 = qb + jnp.where(row2 >= qb, row2 - qb, row2) - key2
    dist1 = jnp.where(row1 >= qb, row1 - qb, row1) - key1
    slope2, slope1 = head_slope(row2), head_slope(row1)
    scale = jnp.asarray(dh ** -0.5, BF16)

    def attend(q, k, v, bias):
        zero = jnp.zeros_like(q)
        q2 = jnp.concatenate([jnp.where(head1, zero, q), jnp.where(head1, q, zero)], axis=0) * scale
        s = lax.dot_general(q2, k, (((1,), (1,)), ((), ())), preferred_element_type=F32) + bias
        m = jnp.max(s, axis=-1, keepdims=True)
        p = jnp.exp(s - m).astype(BF16)
        one = jnp.ones_like(v)
        mine1 = head1 if v.shape[0] == qb else head1_2
        o0 = jnp.dot(p[:qb], jnp.where(mine1, one, v), preferred_element_type=F32)
        o1 = jnp.dot(p[qb:], jnp.where(mine1, v, one), preferred_element_type=F32)
        return (o0, o1), (m[:qb], m[qb:])

    def aligned(x):
        return x if isinstance(x, int) else pl.multiple_of(x, qb)

    for bi, (window, d) in enumerate(DILATED_BRANCHES):
        sub = SEQ // d
        nb = sub // qb
        qsrc = q_ref if bi == 0 else dq_ref.at[bi - 1]
        ksrc = k_ref if bi == 0 else dk_ref.at[bi - 1]
        vsrc = v_ref if bi == 0 else dv_ref.at[bi - 1]
        bias_ref[...] = jnp.where((dist2 >= 0) & (dist2 <= window // d),
                                  -slope2 * (d * dist2).astype(F32), MASKED_SCORE)
        bias0_ref[...] = jnp.where(dist1 >= 0, -slope1 * (d * dist1).astype(F32), MASKED_SCORE)

        def block(base, first_token, has_prev, bi=bi, d=d, qsrc=qsrc, ksrc=ksrc, vsrc=vsrc):
            rows = pl.ds(aligned(base), qb)
            krows = pl.ds(aligned(base - qb), 2 * qb) if has_prev else rows
            bias = bias_ref[...] if has_prev else bias0_ref[...]
            os, ms = attend(qsrc[rows, :], ksrc[krows, :], vsrc[krows, :], bias)
            out_rows = rows if d == 1 else pl.ds(first_token, qb, stride=d)
            for j in range(HEAD_PAIR):
                oacc_ref.at[j, bi][out_rows, :] = os[j]
                macc_ref.at[j, bi][out_rows, :] = jnp.broadcast_to(ms[j], (qb, LANES))

        fu = min(d, ATT_FIRST_UNROLL)

        def first_blocks(it, carry, fu=fu, sub=sub, block=block):
            for u in range(fu):
                r = it * fu + u
                block(r * sub, r, False)
            return carry

        if d == fu:
            first_blocks(0, 0)
        else:
            lax.fori_loop(0, d // fu, first_blocks, 0)

        if nb > 1:
            lu = ATT_LATER_UNROLL
            per_res = (nb - 1) // lu

            assert per_res * lu == nb - 1

            def later_blocks(it, carry, lu=lu, per_res=per_res, sub=sub, d=d, block=block):
                if per_res == 1:
                    r, n0 = it, 1
                elif d == 1:
                    r, n0 = 0, it * lu + 1
                else:
                    r, n0 = lax.div(it, per_res), lax.rem(it, per_res) * lu + 1
                for u in range(lu):
                    n = n0 + u
                    block(r * sub + n * qb, r + d * qb * n, True)
                return carry

            lax.fori_loop(0, d * per_res, later_blocks, 0)

    lane_c = lax.broadcasted_iota(jnp.int32, (COMBINE_ROWS, LANES), 1)
    head1_c = lane_c >= dh

    def combine(tb, carry):
        rows = pl.ds(pl.multiple_of(tb * COMBINE_ROWS, COMBINE_ROWS), COMBINE_ROWS)
        res = []
        for j in range(HEAD_PAIR):
            ms = [macc_ref[j, bi, rows, :] for bi in range(len(DILATED_BRANCHES))]
            top = functools.reduce(jnp.maximum, ms)
            num = sum(jnp.exp(ms[bi] - top) * oacc_ref[j, bi, rows, :]
                      for bi in range(len(DILATED_BRANCHES)))
            res.append(num / pltpu.roll(num, dh, axis=1))
        y_ref[rows, :] = jnp.where(head1_c, res[1], res[0]).astype(BF16)
        return carry

    lax.fori_loop(0, SEQ // COMBINE_ROWS, combine, 0)


def _attention(za):
    pairs = ATT_WIDTH // LANES
    col = lambda off: pl.BlockSpec((None, SEQ, LANES), lambda b, hp: (b, 0, hp + off))
    n_dil = len(DILATED_BRANCHES) - 1
    return pl.pallas_call(
        _attn_kernel,
        grid=(BATCH, pairs),
        in_specs=[col(0), col(pairs), col(2 * pairs)],
        out_specs=col(0),
        out_shape=jax.ShapeDtypeStruct((BATCH, SEQ, ATT_WIDTH), BF16),
        scratch_shapes=[pltpu.VMEM((3, SEQ, LANES), F32),
                        pltpu.VMEM((n_dil, SEQ, LANES), BF16),
                        pltpu.VMEM((n_dil, SEQ, LANES), BF16),
                        pltpu.VMEM((n_dil, SEQ, LANES), BF16),
                        pltpu.VMEM((HEAD_PAIR, n_dil + 1, SEQ, LANES), F32),
                        pltpu.VMEM((HEAD_PAIR, n_dil + 1, SEQ, LANES), F32),
                        pltpu.VMEM((HEAD_PAIR * ATT_BLOCK, 2 * ATT_BLOCK), F32),
                        pltpu.VMEM((HEAD_PAIR * ATT_BLOCK, ATT_BLOCK), F32)],
        compiler_params=_params("parallel", "parallel"),
        name="dilated_attention",
    )(za, za, za)


def _out_proj_kernel(ys_ref, yr_ref, ya_ref, x_ref, w_ref, an_ref, lw_ref, lb_ref, o_ref):
    ya = ya_ref[...].astype(F32)
    ms = jnp.mean(ya * ya, axis=-1, keepdims=True)
    ya = (ya * lax.rsqrt(ms + LN_EPS) * an_ref[...]).astype(BF16)
    h = jnp.dot(ys_ref[...], w_ref[0:SSM_WIDTH, :], preferred_element_type=F32)
    h = h + jnp.dot(yr_ref[...], w_ref[SSM_WIDTH:SSM_WIDTH + RET_WIDTH, :], preferred_element_type=F32)
    h = h + jnp.dot(ya, w_ref[SSM_WIDTH + RET_WIDTH:, :], preferred_element_type=F32)
    o_ref[...] = _layer_norm(DEEPNORM_ALPHA * x_ref[...] + h, lw_ref[...], lb_ref[...])


def _out_proj(y_ssm, y_ret, y_att, x, w_out, attn_norm, ln_w, ln_b):
    tm = TOKEN_TILE
    row = lambda width: pl.BlockSpec((None, tm, width), lambda b, i: (b, i, 0))
    full = lambda shape: pl.BlockSpec(shape, lambda b, i: (0, 0))
    return pl.pallas_call(
        _out_proj_kernel,
        grid=(BATCH, SEQ // tm),
        in_specs=[row(SSM_WIDTH), row(RET_WIDTH), row(ATT_WIDTH), row(D_MODEL),
                  full((D_MODEL, D_MODEL)), full((1, ATT_WIDTH)), full((1, D_MODEL)), full((1, D_MODEL))],
        out_specs=row(D_MODEL),
        out_shape=jax.ShapeDtypeStruct((BATCH, SEQ, D_MODEL), F32),
        compiler_params=_params("parallel", "parallel"),
        name="out_proj_ln",
    )(y_ssm, y_ret, y_att, x, w_out, attn_norm, ln_w, ln_b)


def _mlp_kernel(x_ref, w1_ref, w2_ref, lw_ref, lb_ref, o_ref):
    x = x_ref[...]
    xb = x.astype(BF16)
    acc = jnp.zeros(x.shape, F32)
    for c in range(0, D_FF, FF_CHUNK):
        h = jnp.dot(xb, w1_ref[:, c:c + FF_CHUNK], preferred_element_type=F32)
        h = jnp.square(jnp.maximum(h, 0.0)).astype(BF16)
        acc = acc + jnp.dot(h, w2_ref[c:c + FF_CHUNK, :], preferred_element_type=F32)
    o_ref[...] = _layer_norm(DEEPNORM_ALPHA * x + acc, lw_ref[...], lb_ref[...])


def _mlp(x, w1, w2, ln_w, ln_b):
    tm = TOKEN_TILE
    row = pl.BlockSpec((None, tm, D_MODEL), lambda b, i: (b, i, 0))
    full = lambda shape: pl.BlockSpec(shape, lambda b, i: (0, 0))
    return pl.pallas_call(
        _mlp_kernel,
        grid=(BATCH, SEQ // tm),
        in_specs=[row, full((D_MODEL, D_FF)), full((D_FF, D_MODEL)), full((1, D_MODEL)), full((1, D_MODEL))],
        out_specs=row,
        out_shape=jax.ShapeDtypeStruct((BATCH, SEQ, D_MODEL), F32),
        compiler_params=_params("parallel", "parallel"),
        name="mlp_ln",
    )(x, w1, w2, ln_w, ln_b)


def kernel(x, w_in, ssm_lambda_re, ssm_lambda_im, ssm_b_re, ssm_b_im, ssm_c_re, ssm_c_im, ssm_d, ssm_log_dt, ssm_w_glu, ssm_b_glu, ssm_out_norm, ret_out_norm, attn_out_norm, w_out, ln1_w, ln1_b, mlp_w1, mlp_w2, ln2_w, ln2_b):
    vec = lambda p: p.astype(F32).reshape(1, -1)
    x = x.astype(F32)
    for i in range(DEPTH):
        u, zr, za = _in_proj(x, w_in[i].astype(BF16))
        wb, a_re, a_im, wc = _ssm_weights(ssm_lambda_re[i], ssm_lambda_im[i], ssm_b_re[i], ssm_b_im[i],
                                          ssm_c_re[i], ssm_c_im[i], ssm_log_dt[i])
        u_tm = jnp.transpose(u, (1, 0, 2)).reshape(SEQ * BATCH, SSM_WIDTH)
        y_ssm = _ssm(u_tm, wb, a_re, a_im, wc, vec(ssm_d[i]), ssm_w_glu[i].astype(BF16),
                     vec(ssm_b_glu[i]), vec(ssm_out_norm[i]))
        y_ssm = jnp.transpose(y_ssm.reshape(SEQ, BATCH, SSM_WIDTH), (1, 0, 2))
        y_ret = _retention(zr, vec(ret_out_norm[i]))
        y_att = _attention(za)
        x = _out_proj(y_ssm, y_ret, y_att, x, w_out[i].astype(BF16), vec(attn_out_norm[i]),
                      vec(ln1_w[i]), vec(ln1_b[i]))
        x = _mlp(x, mlp_w1[i].astype(BF16), mlp_w2[i].astype(BF16), vec(ln2_w[i]), vec(ln2_b[i]))
    return x
```

```python
import functools
import math

import jax
import jax.numpy as jnp
from jax import lax
from jax.experimental import pallas as pl
from jax.experimental.pallas import tpu as pltpu

F32 = jnp.float32
BF16 = jnp.bfloat16

D_MODEL = 1024
BATCH = 16
SEQ = 2048
DEPTH = 2
SSM_WIDTH = 256
SSM_GROUP = 16
SSM_GROUPS = 16
SSM_STATE = 64
SSM_STATES = SSM_GROUPS * SSM_STATE
RET_HEAD_DIM = 64
RET_WIDTH = 256
RET_CHUNK = 128
ATT_HEAD_DIM = 64
ATT_WIDTH = 512
ATT_HEADS = 8
DILATED_BRANCHES = ((128, 1), (512, 4), (2048, 16))
ATT_BLOCK = 128
IN_WIDTH = SSM_WIDTH + 4 * RET_WIDTH + 3 * ATT_WIDTH
D_FF = 4 * D_MODEL
DEEPNORM_ALPHA = (2 * DEPTH) ** 0.25
LN_EPS = 1e-5

LANES = 128
HEAD_PAIR = LANES // ATT_HEAD_DIM
VMEM_LIMIT_BYTES = 56 * 1024 * 1024
MASKED_SCORE = -1e30

TOKEN_TILE = 512
SSM_TIME_TILE = 32
SSM_COL_CHUNK = 256
FF_CHUNK = 1024
COMBINE_ROWS = 256
ATT_FIRST_UNROLL = 4
ATT_LATER_UNROLL = 3
RET_UNROLL = 8

assert RET_HEAD_DIM == ATT_HEAD_DIM and HEAD_PAIR == 2
assert all(w // d == ATT_BLOCK for w, d in DILATED_BRANCHES)
assert DILATED_BRANCHES[0][1] == 1 and all(
    b[1] % a[1] == 0 for a, b in zip(DILATED_BRANCHES, DILATED_BRANCHES[1:]))


def _params(*semantics):
    return pltpu.CompilerParams(dimension_semantics=semantics, vmem_limit_bytes=VMEM_LIMIT_BYTES)


def _sigmoid(x):
    return 1.0 / (1.0 + jnp.exp(-x))


def _layer_norm(r, w, b):
    mu = jnp.mean(r, axis=-1, keepdims=True)
    d = r - mu
    var = jnp.mean(d * d, axis=-1, keepdims=True)
    return d * lax.rsqrt(var + LN_EPS) * w + b


def _in_proj_kernel(x_ref, w_ref, u_ref, zr_ref, za_ref):
    xb = x_ref[...].astype(BF16)

    def proj(lo, hi):
        return jnp.dot(xb, w_ref[:, lo:hi], preferred_element_type=F32).astype(BF16)

    u_ref[...] = proj(0, SSM_WIDTH)
    zr_ref[...] = proj(SSM_WIDTH, SSM_WIDTH + 4 * RET_WIDTH)
    za_ref[...] = proj(SSM_WIDTH + 4 * RET_WIDTH, IN_WIDTH)


def _in_proj(x, w_in):
    tm = TOKEN_TILE
    row = lambda width: pl.BlockSpec((None, tm, width), lambda b, i: (b, i, 0))
    return pl.pallas_call(
        _in_proj_kernel,
        grid=(BATCH, SEQ // tm),
        in_specs=[row(D_MODEL), pl.BlockSpec((D_MODEL, IN_WIDTH), lambda b, i: (0, 0))],
        out_specs=[row(SSM_WIDTH), row(4 * RET_WIDTH), row(3 * ATT_WIDTH)],
        out_shape=[jax.ShapeDtypeStruct((BATCH, SEQ, SSM_WIDTH), BF16),
                   jax.ShapeDtypeStruct((BATCH, SEQ, 4 * RET_WIDTH), BF16),
                   jax.ShapeDtypeStruct((BATCH, SEQ, 3 * ATT_WIDTH), BF16)],
        compiler_params=_params("parallel", "parallel"),
        name="in_proj",
    )(x, w_in)


def _ssm_kernel(u_ref, wb_ref, are_ref, aim_ref, wc_ref, d_ref, wglu_ref, bglu_ref, nw_ref,
                y_ref, bu_ref, st_ref):
    n_st = SSM_STATES

    @pl.when(pl.program_id(0) == 0)
    def _():
        st_ref[...] = jnp.zeros_like(st_ref)

    u = u_ref[...]
    bu_ref[...] = jnp.dot(u, wb_ref[...], preferred_element_type=F32)

    for c in range(0, n_st, SSM_COL_CHUNK):
        re = slice(c, c + SSM_COL_CHUNK)
        im = slice(n_st + c, n_st + c + SSM_COL_CHUNK)
        ar = jnp.broadcast_to(are_ref[:, re], (BATCH, SSM_COL_CHUNK))
        ai = jnp.broadcast_to(aim_ref[:, re], (BATCH, SSM_COL_CHUNK))

        def step(t, carry, re=re, im=im, ar=ar, ai=ai):
            xr, xi = carry
            rows = pl.ds(pl.multiple_of(t * BATCH, BATCH), BATCH)
            nr = ar * xr - ai * xi + bu_ref[rows, re]
            ni = ar * xi + ai * xr + bu_ref[rows, im]
            bu_ref[rows, re] = nr
            bu_ref[rows, im] = ni
            return nr, ni

        xr, xi = lax.fori_loop(0, SSM_TIME_TILE, step, (st_ref[:, re], st_ref[:, im]), unroll=4)
        st_ref[:, re] = xr
        st_ref[:, im] = xi

    y = jnp.dot(bu_ref[...].astype(BF16), wc_ref[...], preferred_element_type=F32)
    y = y + d_ref[...] * u.astype(F32)
    cdf = 0.5 * (1.0 + jnp.tanh(math.sqrt(2.0 / math.pi) * (y + 0.044715 * (y * y * y))))
    g = y * cdf
    gate = jnp.dot(g.astype(BF16), wglu_ref[...], preferred_element_type=F32) + bglu_ref[...]
    out = g * _sigmoid(gate)
    ms = jnp.mean(out * out, axis=-1, keepdims=True)
    y_ref[...] = (out * lax.rsqrt(ms + LN_EPS) * nw_ref[...]).astype(BF16)


def _ssm(u_tm, wb, a_re, a_im, wc, d_skip, w_glu, b_glu, norm_w):
    rows = SSM_TIME_TILE * BATCH
    full = lambda shape: pl.BlockSpec(shape, lambda i: (0, 0))
    return pl.pallas_call(
        _ssm_kernel,
        grid=(SEQ // SSM_TIME_TILE,),
        in_specs=[pl.BlockSpec((rows, SSM_WIDTH), lambda i: (i, 0)),
                  full((SSM_WIDTH, 2 * SSM_STATES)), full((1, SSM_STATES)), full((1, SSM_STATES)),
                  full((2 * SSM_STATES, SSM_WIDTH)), full((1, SSM_WIDTH)),
                  full((SSM_WIDTH, SSM_WIDTH)), full((1, SSM_WIDTH)), full((1, SSM_WIDTH))],
        out_specs=pl.BlockSpec((rows, SSM_WIDTH), lambda i: (i, 0)),
        out_shape=jax.ShapeDtypeStruct((SEQ * BATCH, SSM_WIDTH), BF16),
        scratch_shapes=[pltpu.VMEM((rows, 2 * SSM_STATES), F32),
                        pltpu.VMEM((BATCH, 2 * SSM_STATES), F32)],
        compiler_params=_params("arbitrary"),
        name="s5_mixer",
    )(u_tm, wb, a_re, a_im, wc, d_skip, w_glu, b_glu, norm_w)


def _ssm_weights(lam_re, lam_im, b_re, b_im, c_re, c_im, log_dt):
    g, p, h = SSM_GROUPS, SSM_STATE, SSM_GROUP
    lr, li = lam_re.astype(F32), lam_im.astype(F32)
    dt = jnp.exp(log_dt.astype(F32))[:, None]
    mag = jnp.exp(lr * dt)
    a_re, a_im = mag * jnp.cos(li * dt), mag * jnp.sin(li * dt)
    den = lr * lr + li * li
    nr, ni = a_re - 1.0, a_im
    f_re = ((nr * lr + ni * li) / den)[..., None]
    f_im = ((ni * lr - nr * li) / den)[..., None]
    br, bi = b_re.astype(F32), b_im.astype(F32)
    bb_re = f_re * br - f_im * bi
    bb_im = f_re * bi + f_im * br
    eye = jnp.eye(g, dtype=F32)
    embed_b = lambda bb: jnp.einsum('gph,gk->ghkp', bb, eye).reshape(g * h, g * p)
    embed_c = lambda cc: jnp.einsum('ghp,gk->gpkh', cc, eye).reshape(g * p, g * h)
    wb = jnp.concatenate([embed_b(bb_re), embed_b(bb_im)], axis=1).astype(BF16)
    wc = jnp.concatenate([embed_c(c_re.astype(F32)), -embed_c(c_im.astype(F32))], axis=0).astype(BF16)
    return wb, a_re.reshape(1, g * p), a_im.reshape(1, g * p), wc


def _ret_kernel(q_ref, k_ref, v_ref, g_ref, nw_ref, y_ref, dmat_ref):
    c = RET_CHUNK
    dh = RET_HEAD_DIM
    hp = pl.program_id(1)
    lane = lax.broadcasted_iota(jnp.int32, (c, LANES), 1)
    row = lax.broadcasted_iota(jnp.int32, (c, LANES), 0)
    head1 = lane >= dh
    head_of_lane = (HEAD_PAIR * hp).astype(F32) + head1.astype(F32)
    lg = jnp.log(1.0 - jnp.exp2(-5.0 - head_of_lane))
    rowf = row.astype(F32)
    xi = jnp.exp((rowf + 1.0) * lg)
    zeta = jnp.exp((c - 1.0 - rowf) * lg) * (dh ** -0.5)
    g_chunk = jnp.exp(c * lg)
    block_diag = (row >= dh) == head1
    diff = (row - lane).astype(F32)
    for j in range(HEAD_PAIR):
        lg_j = jnp.log(1.0 - jnp.exp2(jnp.zeros((c, c), F32) - 5.0 - (HEAD_PAIR * hp + j).astype(F32)))
        dmat_ref[j] = jnp.where(diff >= 0, jnp.exp(jnp.maximum(diff, 0.0) * lg_j), 0.0) * (dh ** -0.5)
    nw = nw_ref[...]

    def head_mean(t):
        s0 = jnp.sum(jnp.where(head1, 0.0, t), axis=-1, keepdims=True)
        s1 = jnp.sum(jnp.where(head1, t, 0.0), axis=-1, keepdims=True)
        return jnp.where(head1, s1, s0) * (1.0 / dh)

    def chunk(n, r_prev):
        rows = pl.ds(pl.multiple_of(n * c, c), c)
        q, k, v = q_ref[rows, :], k_ref[rows, :], v_ref[rows, :]
        o = jnp.zeros((c, LANES), F32)
        for j in range(HEAD_PAIR):
            mine = head1 if j else jnp.logical_not(head1)
            qj = jnp.where(mine, q, jnp.zeros_like(q))
            s = lax.dot_general(qj, k, (((1,), (1,)), ((), ())), preferred_element_type=F32)
            s = s * dmat_ref[j]
            vj = jnp.where(mine, v, jnp.zeros_like(v))
            o = o + jnp.dot(s.astype(BF16), vj, preferred_element_type=F32)
        qx = (q.astype(F32) * xi).astype(BF16)
        o = o + jnp.dot(qx, r_prev.astype(BF16), preferred_element_type=F32)
        kz = (k.astype(F32) * zeta).T.astype(BF16)
        kv = jnp.dot(kz, v, preferred_element_type=F32)
        dlt = o - head_mean(o)
        var = head_mean(dlt * dlt)
        gate = g_ref[rows, :].astype(F32)
        y = dlt * lax.rsqrt(var + LN_EPS) * nw * (gate * _sigmoid(gate))
        y_ref[rows, :] = y.astype(BF16)
        return jnp.where(block_diag, g_chunk * r_prev + kv, 0.0)

    def chunks(it, r):
        for u in range(RET_UNROLL):
            r = chunk(it * RET_UNROLL + u, r)
        return r

    lax.fori_loop(0, SEQ // (c * RET_UNROLL), chunks, jnp.zeros((LANES, LANES), F32))


def _retention(zr, norm_w):
    col = lambda off: pl.BlockSpec((None, SEQ, LANES), lambda b, hp: (b, 0, hp + off))
    pairs = RET_WIDTH // LANES
    return pl.pallas_call(
        _ret_kernel,
        grid=(BATCH, pairs),
        in_specs=[col(0), col(pairs), col(2 * pairs), col(3 * pairs),
                  pl.BlockSpec((1, LANES), lambda b, hp: (0, hp))],
        out_specs=col(0),
        out_shape=jax.ShapeDtypeStruct((BATCH, SEQ, RET_WIDTH), BF16),
        scratch_shapes=[pltpu.VMEM((HEAD_PAIR, RET_CHUNK, RET_CHUNK), F32)],
        compiler_params=_params("parallel", "parallel"),
        name="retention",
    )(zr, zr, zr, zr, norm_w)


def _attn_kernel(q_ref, k_ref, v_ref, y_ref, src_ref, dq_ref, dk_ref, dv_ref,
                 oacc_ref, macc_ref, bias_ref, bias0_ref, p_ref):
    qb = ATT_BLOCK
    dh = ATT_HEAD_DIM
    hp = pl.program_id(1)
    lane = lax.broadcasted_iota(jnp.int32, (qb, LANES), 1)
    head1 = lane >= dh
    head1_2 = lax.broadcasted_iota(jnp.int32, (2 * qb, LANES), 1) >= dh

    for src, dst in ((q_ref, dq_ref), (k_ref, dk_ref), (v_ref, dv_ref)):
        src_ref[0] = src[...].astype(F32)
        d_prev = 1
        for bi in range(1, len(DILATED_BRANCHES)):
            d = DILATED_BRANCHES[bi][1]
            step = d // d_prev
            sub_prev, sub = SEQ // d_prev, SEQ // d
            cur, nxt = (bi - 1) % 2, bi % 2
            for r_prev in range(d_prev):
                for t in range(step):
                    r = r_prev + d_prev * t
                    blk = src_ref.at[cur][pl.ds(r_prev * sub_prev + t, sub, stride=step), :]
                    dst[bi - 1, r * sub:(r + 1) * sub, :] = blk.astype(BF16)
                    if bi + 1 < len(DILATED_BRANCHES):
                        src_ref[nxt, r * sub:(r + 1) * sub, :] = blk
            d_prev = d

    row2 = lax.broadcasted_iota(jnp.int32, (HEAD_PAIR * qb, 2 * qb), 0)
    key2 = lax.broadcasted_iota(jnp.int32, (HEAD_PAIR * qb, 2 * qb), 1)
    row1 = lax.broadcasted_iota(jnp.int32, (HEAD_PAIR * qb, qb), 0)
    key1 = lax.broadcasted_iota(jnp.int32, (HEAD_PAIR * qb, qb), 1)

    def head_slope(row):
        head = (HEAD_PAIR * hp).astype(F32) + (row >= qb).astype(F32)
        return jnp.exp2(-(8.0 / ATT_HEADS) * (head + 1.0))

    dist2 = qb + jnp.where(row2 >= qb, row2 - qb, row2) - key2
    dist1 = jnp.where(row1 >= qb, row1 - qb, row1) - key1
    slope2, slope1 = head_slope(row2), head_slope(row1)
    scale = jnp.asarray(dh ** -0.5, BF16)

    one_k = {qb: jnp.ones((qb, LANES), BF16), 2 * qb: jnp.ones((2 * qb, LANES), BF16)}
    head1_k = {qb: head1, 2 * qb: head1_2}

    def aligned(x):
        return x if isinstance(x, int) else pl.multiple_of(x, qb)

    for bi, (window, d) in enumerate(DILATED_BRANCHES):
        sub = SEQ // d
        nb = sub // qb
        qsrc = q_ref if bi == 0 else dq_ref.at[bi - 1]
        ksrc = k_ref if bi == 0 else dk_ref.at[bi - 1]
        vsrc = v_ref if bi == 0 else dv_ref.at[bi - 1]
        bias_ref[...] = jnp.where((dist2 >= 0) & (dist2 <= window // d),
                                  -slope2 * (d * dist2).astype(F32), MASKED_SCORE)
        bias0_ref[...] = jnp.where(dist1 >= 0, -slope1 * (d * dist1).astype(F32), MASKED_SCORE)

        def rows_of(blk, d=d):
            base, first_token, has_prev, _ = blk
            rows = pl.ds(aligned(base), qb)
            krows = pl.ds(aligned(base - qb), 2 * qb) if has_prev else rows
            out_rows = rows if d == 1 else pl.ds(first_token, qb, stride=d)
            return rows, krows, out_rows, (2 * qb if has_prev else qb)

        def scores(blk, bi=bi, qsrc=qsrc, ksrc=ksrc, rows_of=rows_of):
            rows, krows, out_rows, nk = rows_of(blk)
            q = qsrc[rows, :]
            zero = jnp.zeros_like(q)
            q2 = jnp.concatenate([jnp.where(head1, zero, q), jnp.where(head1, q, zero)], axis=0) * scale
            s = lax.dot_general(q2, ksrc[krows, :], (((1,), (1,)), ((), ())), preferred_element_type=F32)
            s = s + (bias_ref[...] if blk[2] else bias0_ref[...])
            m = jnp.max(s, axis=-1, keepdims=True)
            p_ref[blk[3], :, 0:nk] = jnp.exp(s - m).astype(BF16)
            for j in range(HEAD_PAIR):
                macc_ref.at[j, bi][out_rows, :] = jnp.broadcast_to(m[j * qb:(j + 1) * qb], (qb, LANES))

        def values(blk, bi=bi, vsrc=vsrc, rows_of=rows_of):
            rows, krows, out_rows, nk = rows_of(blk)
            v = vsrc[krows, :]
            p = p_ref[blk[3], :, 0:nk]
            vs = (jnp.where(head1_k[nk], one_k[nk], v), jnp.where(head1_k[nk], v, one_k[nk]))
            for j in range(HEAD_PAIR):
                oacc_ref.at[j, bi][out_rows, :] = jnp.dot(p[j * qb:(j + 1) * qb], vs[j],
                                                         preferred_element_type=F32)

        def pipelined(n_groups, group, scores=scores, values=values):
            for blk in group(0):
                scores(blk)
            if n_groups > 1:
                def body(it, carry):
                    for blk in group(it - 1):
                        values(blk)
                    for blk in group(it):
                        scores(blk)
                    return carry

                lax.fori_loop(1, n_groups, body, 0)
            for blk in group(n_groups - 1):
                values(blk)

        fu = min(d, ATT_FIRST_UNROLL)
        pipelined(d // fu, lambda it, fu=fu, sub=sub: [
            ((it * fu + u) * sub, it * fu + u, False, it * fu + u) for u in range(fu)])

        if nb > 1:
            lu = ATT_LATER_UNROLL
            per_res = (nb - 1) // lu
            assert per_res * lu == nb - 1

            def later_group(it, lu=lu, per_res=per_res, sub=sub, d=d):
                if per_res == 1:
                    r, n0 = it, 1
                elif d == 1:
                    r, n0 = 0, it * lu + 1
                elif isinstance(it, int):
                    r, n0 = it // per_res, (it % per_res) * lu + 1
                else:
                    r, n0 = lax.div(it, per_res), lax.rem(it, per_res) * lu + 1
                return [(r * sub + (n0 + u) * qb, r + d * qb * (n0 + u), True, d + it * lu + u)
                        for u in range(lu)]

            pipelined(d * per_res, later_group)

    lane_c = lax.broadcasted_iota(jnp.int32, (COMBINE_ROWS, LANES), 1)
    head1_c = lane_c >= dh

    def combine(tb, carry):
        rows = pl.ds(pl.multiple_of(tb * COMBINE_ROWS, COMBINE_ROWS), COMBINE_ROWS)
        res = []
        for j in range(HEAD_PAIR):
            ms = [macc_ref[j, bi, rows, :] for bi in range(len(DILATED_BRANCHES))]
            top = functools.reduce(jnp.maximum, ms)
            num = sum(jnp.exp(ms[bi] - top) * oacc_ref[j, bi, rows, :]
                      for bi in range(len(DILATED_BRANCHES)))
            res.append(num / pltpu.roll(num, dh, axis=1))
        y_ref[rows, :] = jnp.where(head1_c, res[1], res[0]).astype(BF16)
        return carry

    lax.fori_loop(0, SEQ // COMBINE_ROWS, combine, 0)


def _attention(za):
    pairs = ATT_WIDTH // LANES
    col = lambda off: pl.BlockSpec((None, SEQ, LANES), lambda b, hp: (b, 0, hp + off))
    n_dil = len(DILATED_BRANCHES) - 1
    return pl.pallas_call(
        _attn_kernel,
        grid=(BATCH, pairs),
        in_specs=[col(0), col(pairs), col(2 * pairs)],
        out_specs=col(0),
        out_shape=jax.ShapeDtypeStruct((BATCH, SEQ, ATT_WIDTH), BF16),
        scratch_shapes=[pltpu.VMEM((2, SEQ, LANES), F32),
                        pltpu.VMEM((n_dil, SEQ, LANES), BF16),
                        pltpu.VMEM((n_dil, SEQ, LANES), BF16),
                        pltpu.VMEM((n_dil, SEQ, LANES), BF16),
                        pltpu.VMEM((HEAD_PAIR, n_dil + 1, SEQ, LANES), F32),
                        pltpu.VMEM((HEAD_PAIR, n_dil + 1, SEQ, LANES), F32),
                        pltpu.VMEM((HEAD_PAIR * ATT_BLOCK, 2 * ATT_BLOCK), F32),
                        pltpu.VMEM((HEAD_PAIR * ATT_BLOCK, ATT_BLOCK), F32),
                        pltpu.VMEM((SEQ // ATT_BLOCK, HEAD_PAIR * ATT_BLOCK, 2 * ATT_BLOCK), BF16)],
        compiler_params=_params("parallel", "parallel"),
        name="dilated_attention",
    )(za, za, za)


def _out_proj_kernel(ys_ref, yr_ref, ya_ref, x_ref, w_ref, an_ref, lw_ref, lb_ref, o_ref):
    ya = ya_ref[...].astype(F32)
    ms = jnp.mean(ya * ya, axis=-1, keepdims=True)
    ya = (ya * lax.rsqrt(ms + LN_EPS) * an_ref[...]).astype(BF16)
    h = jnp.dot(ys_ref[...], w_ref[0:SSM_WIDTH, :], preferred_element_type=F32)
    h = h + jnp.dot(yr_ref[...], w_ref[SSM_WIDTH:SSM_WIDTH + RET_WIDTH, :], preferred_element_type=F32)
    h = h + jnp.dot(ya, w_ref[SSM_WIDTH + RET_WIDTH:, :], preferred_element_type=F32)
    o_ref[...] = _layer_norm(DEEPNORM_ALPHA * x_ref[...] + h, lw_ref[...], lb_ref[...])


def _out_proj(y_ssm, y_ret, y_att, x, w_out, attn_norm, ln_w, ln_b):
    tm = TOKEN_TILE
    row = lambda width: pl.BlockSpec((None, tm, width), lambda b, i: (b, i, 0))
    full = lambda shape: pl.BlockSpec(shape, lambda b, i: (0, 0))
    return pl.pallas_call(
        _out_proj_kernel,
        grid=(BATCH, SEQ // tm),
        in_specs=[row(SSM_WIDTH), row(RET_WIDTH), row(ATT_WIDTH), row(D_MODEL),
                  full((D_MODEL, D_MODEL)), full((1, ATT_WIDTH)), full((1, D_MODEL)), full((1, D_MODEL))],
        out_specs=row(D_MODEL),
        out_shape=jax.ShapeDtypeStruct((BATCH, SEQ, D_MODEL), F32),
        compiler_params=_params("parallel", "parallel"),
        name="out_proj_ln",
    )(y_ssm, y_ret, y_att, x, w_out, attn_norm, ln_w, ln_b)


def _mlp_kernel(x_ref, w1_ref, w2_ref, lw_ref, lb_ref, o_ref):
    x = x_ref[...]
    xb = x.astype(BF16)
    acc = jnp.zeros(x.shape, F32)
    for c in range(0, D_FF, FF_CHUNK):
        h = jnp.dot(xb, w1_ref[:, c:c + FF_CHUNK], preferred_element_type=F32)
        h = jnp.square(jnp.maximum(h, 0.0)).astype(BF16)
        acc = acc + jnp.dot(h, w2_ref[c:c + FF_CHUNK, :], preferred_element_type=F32)
    o_ref[...] = _layer_norm(DEEPNORM_ALPHA * x + acc, lw_ref[...], lb_ref[...])


def _mlp(x, w1, w2, ln_w, ln_b):
    tm = TOKEN_TILE
    row = pl.BlockSpec((None, tm, D_MODEL), lambda b, i: (b, i, 0))
    full = lambda shape: pl.BlockSpec(shape, lambda b, i: (0, 0))
    return pl.pallas_call(
        _mlp_kernel,
        grid=(BATCH, SEQ // tm),
        in_specs=[row, full((D_MODEL, D_FF)), full((D_FF, D_MODEL)), full((1, D_MODEL)), full((1, D_MODEL))],
        out_specs=row,
        out_shape=jax.ShapeDtypeStruct((BATCH, SEQ, D_MODEL), F32),
        compiler_params=_params("parallel", "parallel"),
        name="mlp_ln",
    )(x, w1, w2, ln_w, ln_b)


def kernel(x, w_in, ssm_lambda_re, ssm_lambda_im, ssm_b_re, ssm_b_im, ssm_c_re, ssm_c_im, ssm_d, ssm_log_dt, ssm_w_glu, ssm_b_glu, ssm_out_norm, ret_out_norm, attn_out_norm, w_out, ln1_w, ln1_b, mlp_w1, mlp_w2, ln2_w, ln2_b):
    vec = lambda p: p.astype(F32).reshape(1, -1)
    x = x.astype(F32)
    for i in range(DEPTH):
        u, zr, za = _in_proj(x, w_in[i].astype(BF16))
        wb, a_re, a_im, wc = _ssm_weights(ssm_lambda_re[i], ssm_lambda_im[i], ssm_b_re[i], ssm_b_im[i],
                                          ssm_c_re[i], ssm_c_im[i], ssm_log_dt[i])
        u_tm = jnp.transpose(u, (1, 0, 2)).reshape(SEQ * BATCH, SSM_WIDTH)
        y_ssm = _ssm(u_tm, wb, a_re, a_im, wc, vec(ssm_d[i]), ssm_w_glu[i].astype(BF16),
                     vec(ssm_b_glu[i]), vec(ssm_out_norm[i]))
        y_ssm = jnp.transpose(y_ssm.reshape(SEQ, BATCH, SSM_WIDTH), (1, 0, 2))
        y_ret = _retention(zr, vec(ret_out_norm[i]))
        y_att = _attention(za)
        x = _out_proj(y_ssm, y_ret, y_att, x, w_out[i].astype(BF16), vec(attn_out_norm[i]),
                      vec(ln1_w[i]), vec(ln1_b[i]))
        x = _mlp(x, mlp_w1[i].astype(BF16), mlp_w2[i].astype(BF16), vec(ln2_w[i]), vec(ln2_b[i]))
    return x
```

```python
import functools
import math

import jax
import jax.numpy as jnp
from jax import lax
from jax.experimental import pallas as pl
from jax.experimental.pallas import tpu as pltpu

F32 = jnp.float32
BF16 = jnp.bfloat16

D_MODEL = 1024
BATCH = 16
SEQ = 2048
DEPTH = 2
SSM_WIDTH = 256
SSM_GROUP = 16
SSM_GROUPS = 16
SSM_STATE = 64
SSM_STATES = SSM_GROUPS * SSM_STATE
RET_HEAD_DIM = 64
RET_WIDTH = 256
RET_CHUNK = 128
ATT_HEAD_DIM = 64
ATT_WIDTH = 512
ATT_HEADS = 8
DILATED_BRANCHES = ((128, 1), (512, 4), (2048, 16))
ATT_BLOCK = 128
IN_WIDTH = SSM_WIDTH + 4 * RET_WIDTH + 3 * ATT_WIDTH
D_FF = 4 * D_MODEL
DEEPNORM_ALPHA = (2 * DEPTH) ** 0.25
LN_EPS = 1e-5

LANES = 128
HEAD_PAIR = LANES // ATT_HEAD_DIM
VMEM_LIMIT_BYTES = 56 * 1024 * 1024
MASKED_SCORE = -1e30

TOKEN_TILE = 512
SSM_TIME_TILE = 32
SSM_COL_CHUNK = 256
FF_CHUNK = 1024
COMBINE_ROWS = 256
PREP_ROWS = 128
ATT_FIRST_UNROLL = 4
ATT_LATER_UNROLL = 3
RET_UNROLL = 8

assert RET_HEAD_DIM == ATT_HEAD_DIM and HEAD_PAIR == 2
assert all(w // d == ATT_BLOCK for w, d in DILATED_BRANCHES)
assert DILATED_BRANCHES[0][1] == 1 and all(
    b[1] % a[1] == 0 for a, b in zip(DILATED_BRANCHES, DILATED_BRANCHES[1:]))


def _params(*semantics):
    return pltpu.CompilerParams(dimension_semantics=semantics, vmem_limit_bytes=VMEM_LIMIT_BYTES)


def _sigmoid(x):
    return 1.0 / (1.0 + jnp.exp(-x))


def _layer_norm(r, w, b):
    mu = jnp.mean(r, axis=-1, keepdims=True)
    d = r - mu
    var = jnp.mean(d * d, axis=-1, keepdims=True)
    return d * lax.rsqrt(var + LN_EPS) * w + b


def _in_proj_kernel(x_ref, w_ref, u_ref, zr_ref, za_ref):
    xb = x_ref[...].astype(BF16)

    def proj(lo, hi):
        return jnp.dot(xb, w_ref[:, lo:hi], preferred_element_type=F32).astype(BF16)

    u_ref[...] = proj(0, SSM_WIDTH)
    zr_ref[...] = proj(SSM_WIDTH, SSM_WIDTH + 4 * RET_WIDTH)
    za_ref[...] = proj(SSM_WIDTH + 4 * RET_WIDTH, IN_WIDTH)


def _in_proj(x, w_in):
    tm = TOKEN_TILE
    row = lambda width: pl.BlockSpec((None, tm, width), lambda b, i: (b, i, 0))
    return pl.pallas_call(
        _in_proj_kernel,
        grid=(BATCH, SEQ // tm),
        in_specs=[row(D_MODEL), pl.BlockSpec((D_MODEL, IN_WIDTH), lambda b, i: (0, 0))],
        out_specs=[row(SSM_WIDTH), row(4 * RET_WIDTH), row(3 * ATT_WIDTH)],
        out_shape=[jax.ShapeDtypeStruct((BATCH, SEQ, SSM_WIDTH), BF16),
                   jax.ShapeDtypeStruct((BATCH, SEQ, 4 * RET_WIDTH), BF16),
                   jax.ShapeDtypeStruct((BATCH, SEQ, 3 * ATT_WIDTH), BF16)],
        compiler_params=_params("parallel", "parallel"),
        name="in_proj",
    )(x, w_in)


def _ssm_kernel(u_ref, wb_ref, are_ref, aim_ref, wc_ref, d_ref, wglu_ref, bglu_ref, nw_ref,
                y_ref, bu_ref, st_ref):
    n_st = SSM_STATES

    @pl.when(pl.program_id(0) == 0)
    def _():
        st_ref[...] = jnp.zeros_like(st_ref)

    u = u_ref[...]
    bu_ref[...] = jnp.dot(u, wb_ref[...], preferred_element_type=F32)

    for c in range(0, n_st, SSM_COL_CHUNK):
        re = slice(c, c + SSM_COL_CHUNK)
        im = slice(n_st + c, n_st + c + SSM_COL_CHUNK)
        ar = jnp.broadcast_to(are_ref[:, re], (BATCH, SSM_COL_CHUNK))
        ai = jnp.broadcast_to(aim_ref[:, re], (BATCH, SSM_COL_CHUNK))

        def step(t, carry, re=re, im=im, ar=ar, ai=ai):
            xr, xi = carry
            rows = pl.ds(pl.multiple_of(t * BATCH, BATCH), BATCH)
            nr = ar * xr - ai * xi + bu_ref[rows, re]
            ni = ar * xi + ai * xr + bu_ref[rows, im]
            bu_ref[rows, re] = nr
            bu_ref[rows, im] = ni
            return nr, ni

        xr, xi = lax.fori_loop(0, SSM_TIME_TILE, step, (st_ref[:, re], st_ref[:, im]), unroll=4)
        st_ref[:, re] = xr
        st_ref[:, im] = xi

    y = jnp.dot(bu_ref[...].astype(BF16), wc_ref[...], preferred_element_type=F32)
    y = y + d_ref[...] * u.astype(F32)
    cdf = 0.5 * (1.0 + jnp.tanh(math.sqrt(2.0 / math.pi) * (y + 0.044715 * (y * y * y))))
    g = y * cdf
    gate = jnp.dot(g.astype(BF16), wglu_ref[...], preferred_element_type=F32) + bglu_ref[...]
    out = g * _sigmoid(gate)
    ms = jnp.mean(out * out, axis=-1, keepdims=True)
    y_ref[...] = (out * lax.rsqrt(ms + LN_EPS) * nw_ref[...]).astype(BF16)


def _ssm(u_tm, wb, a_re, a_im, wc, d_skip, w_glu, b_glu, norm_w):
    rows = SSM_TIME_TILE * BATCH
    full = lambda shape: pl.BlockSpec(shape, lambda i: (0, 0))
    return pl.pallas_call(
        _ssm_kernel,
        grid=(SEQ // SSM_TIME_TILE,),
        in_specs=[pl.BlockSpec((rows, SSM_WIDTH), lambda i: (i, 0)),
                  full((SSM_WIDTH, 2 * SSM_STATES)), full((1, SSM_STATES)), full((1, SSM_STATES)),
                  full((2 * SSM_STATES, SSM_WIDTH)), full((1, SSM_WIDTH)),
                  full((SSM_WIDTH, SSM_WIDTH)), full((1, SSM_WIDTH)), full((1, SSM_WIDTH))],
        out_specs=pl.BlockSpec((rows, SSM_WIDTH), lambda i: (i, 0)),
        out_shape=jax.ShapeDtypeStruct((SEQ * BATCH, SSM_WIDTH), BF16),
        scratch_shapes=[pltpu.VMEM((rows, 2 * SSM_STATES), F32),
                        pltpu.VMEM((BATCH, 2 * SSM_STATES), F32)],
        compiler_params=_params("arbitrary"),
        name="s5_mixer",
    )(u_tm, wb, a_re, a_im, wc, d_skip, w_glu, b_glu, norm_w)


def _ssm_weights(lam_re, lam_im, b_re, b_im, c_re, c_im, log_dt):
    g, p, h = SSM_GROUPS, SSM_STATE, SSM_GROUP
    lr, li = lam_re.astype(F32), lam_im.astype(F32)
    dt = jnp.exp(log_dt.astype(F32))[:, None]
    mag = jnp.exp(lr * dt)
    a_re, a_im = mag * jnp.cos(li * dt), mag * jnp.sin(li * dt)
    den = lr * lr + li * li
    nr, ni = a_re - 1.0, a_im
    f_re = ((nr * lr + ni * li) / den)[..., None]
    f_im = ((ni * lr - nr * li) / den)[..., None]
    br, bi = b_re.astype(F32), b_im.astype(F32)
    bb_re = f_re * br - f_im * bi
    bb_im = f_re * bi + f_im * br
    eye = jnp.eye(g, dtype=F32)
    embed_b = lambda bb: jnp.einsum('gph,gk->ghkp', bb, eye).reshape(g * h, g * p)
    embed_c = lambda cc: jnp.einsum('ghp,gk->gpkh', cc, eye).reshape(g * p, g * h)
    wb = jnp.concatenate([embed_b(bb_re), embed_b(bb_im)], axis=1).astype(BF16)
    wc = jnp.concatenate([embed_c(c_re.astype(F32)), -embed_c(c_im.astype(F32))], axis=0).astype(BF16)
    return wb, a_re.reshape(1, g * p), a_im.reshape(1, g * p), wc


def _ret_kernel(q_ref, k_ref, v_ref, g_ref, nw_ref, y_ref, dmat_ref):
    c = RET_CHUNK
    dh = RET_HEAD_DIM
    hp = pl.program_id(1)
    lane = lax.broadcasted_iota(jnp.int32, (c, LANES), 1)
    row = lax.broadcasted_iota(jnp.int32, (c, LANES), 0)
    head1 = lane >= dh
    head_of_lane = (HEAD_PAIR * hp).astype(F32) + head1.astype(F32)
    lg = jnp.log(1.0 - jnp.exp2(-5.0 - head_of_lane))
    rowf = row.astype(F32)
    xi = jnp.exp((rowf + 1.0) * lg)
    zeta = jnp.exp((c - 1.0 - rowf) * lg) * (dh ** -0.5)
    g_chunk = jnp.exp(c * lg)
    block_diag = (row >= dh) == head1
    diff = (row - lane).astype(F32)
    for j in range(HEAD_PAIR):
        lg_j = jnp.log(1.0 - jnp.exp2(jnp.zeros((c, c), F32) - 5.0 - (HEAD_PAIR * hp + j).astype(F32)))
        dmat_ref[j] = jnp.where(diff >= 0, jnp.exp(jnp.maximum(diff, 0.0) * lg_j), 0.0) * (dh ** -0.5)
    nw = nw_ref[...]

    def head_mean(t):
        s0 = jnp.sum(jnp.where(head1, 0.0, t), axis=-1, keepdims=True)
        s1 = jnp.sum(jnp.where(head1, t, 0.0), axis=-1, keepdims=True)
        return jnp.where(head1, s1, s0) * (1.0 / dh)

    def chunk(n, r_prev):
        rows = pl.ds(pl.multiple_of(n * c, c), c)
        q, k, v = q_ref[rows, :], k_ref[rows, :], v_ref[rows, :]
        o = jnp.zeros((c, LANES), F32)
        for j in range(HEAD_PAIR):
            mine = head1 if j else jnp.logical_not(head1)
            qj = jnp.where(mine, q, jnp.zeros_like(q))
            s = lax.dot_general(qj, k, (((1,), (1,)), ((), ())), preferred_element_type=F32)
            s = s * dmat_ref[j]
            vj = jnp.where(mine, v, jnp.zeros_like(v))
            o = o + jnp.dot(s.astype(BF16), vj, preferred_element_type=F32)
        qx = (q.astype(F32) * xi).astype(BF16)
        o = o + jnp.dot(qx, r_prev.astype(BF16), preferred_element_type=F32)
        kz = (k.astype(F32) * zeta).T.astype(BF16)
        kv = jnp.dot(kz, v, preferred_element_type=F32)
        dlt = o - head_mean(o)
        var = head_mean(dlt * dlt)
        gate = g_ref[rows, :].astype(F32)
        y = dlt * lax.rsqrt(var + LN_EPS) * nw * (gate * _sigmoid(gate))
        y_ref[rows, :] = y.astype(BF16)
        return jnp.where(block_diag, g_chunk * r_prev + kv, 0.0)

    def chunks(it, r):
        for u in range(RET_UNROLL):
            r = chunk(it * RET_UNROLL + u, r)
        return r

    lax.fori_loop(0, SEQ // (c * RET_UNROLL), chunks, jnp.zeros((LANES, LANES), F32))


def _retention(zr, norm_w):
    col = lambda off: pl.BlockSpec((None, SEQ, LANES), lambda b, hp: (b, 0, hp + off))
    pairs = RET_WIDTH // LANES
    return pl.pallas_call(
        _ret_kernel,
        grid=(BATCH, pairs),
        in_specs=[col(0), col(pairs), col(2 * pairs), col(3 * pairs),
                  pl.BlockSpec((1, LANES), lambda b, hp: (0, hp))],
        out_specs=col(0),
        out_shape=jax.ShapeDtypeStruct((BATCH, SEQ, RET_WIDTH), BF16),
        scratch_shapes=[pltpu.VMEM((HEAD_PAIR, RET_CHUNK, RET_CHUNK), F32)],
        compiler_params=_params("parallel", "parallel"),
        name="retention",
    )(zr, zr, zr, zr, norm_w)


def _attn_kernel(q_ref, k_ref, v_ref, y_ref, src_ref, dq_ref, dk_ref, dv_ref,
                 oacc_ref, macc_ref, bias_ref, bias0_ref, p_ref, s_ref):
    qb = ATT_BLOCK
    dh = ATT_HEAD_DIM
    hp = pl.program_id(1)

    def emit_q(bi, rows, blk):
        other = lax.broadcasted_iota(jnp.int32, blk.shape, 1) >= dh
        blk = blk * (dh ** -0.5)
        dq_ref[bi, 0, rows, :] = jnp.where(other, 0.0, blk).astype(BF16)
        dq_ref[bi, 1, rows, :] = jnp.where(other, blk, 0.0).astype(BF16)

    def emit_k(bi, rows, blk):
        if bi > 0:
            dk_ref[bi - 1, rows, :] = blk.astype(BF16)

    def emit_v(bi, rows, blk):
        other = lax.broadcasted_iota(jnp.int32, blk.shape, 1) >= dh
        dv_ref[bi, 0, rows, :] = jnp.where(other, 1.0, blk).astype(BF16)
        dv_ref[bi, 1, rows, :] = jnp.where(other, blk, 1.0).astype(BF16)

    piece = PREP_ROWS
    for src, emit in ((q_ref, emit_q), (k_ref, emit_k), (v_ref, emit_v)):
        for c0 in range(0, SEQ, piece):
            rows = slice(c0, c0 + piece)
            natural = src[rows, :].astype(F32)
            src_ref[0, rows, :] = natural
            emit(0, rows, natural)
        d_prev = 1
        for bi in range(1, len(DILATED_BRANCHES)):
            d = DILATED_BRANCHES[bi][1]
            step = d // d_prev
            sub_prev, sub = SEQ // d_prev, SEQ // d
            cur, nxt = (bi - 1) % 2, bi % 2
            for r_prev in range(d_prev):
                for t in range(step):
                    r = r_prev + d_prev * t
                    for c0 in range(0, sub, piece):
                        n = min(piece, sub - c0)
                        rows = slice(r * sub + c0, r * sub + c0 + n)
                        blk = src_ref.at[cur][pl.ds(r_prev * sub_prev + t + step * c0, n, stride=step), :]
                        emit(bi, rows, blk)
                        if bi + 1 < len(DILATED_BRANCHES):
                            src_ref[nxt, rows, :] = blk
            d_prev = d

    row2 = lax.broadcasted_iota(jnp.int32, (HEAD_PAIR * qb, 2 * qb), 0)
    key2 = lax.broadcasted_iota(jnp.int32, (HEAD_PAIR * qb, 2 * qb), 1)
    row1 = lax.broadcasted_iota(jnp.int32, (HEAD_PAIR * qb, qb), 0)
    key1 = lax.broadcasted_iota(jnp.int32, (HEAD_PAIR * qb, qb), 1)

    def head_slope(row):
        head = (HEAD_PAIR * hp).astype(F32) + (row >= qb).astype(F32)
        return jnp.exp2(-(8.0 / ATT_HEADS) * (head + 1.0))

    dist2 = qb + jnp.where(row2 >= qb, row2 - qb, row2) - key2
    dist1 = jnp.where(row1 >= qb, row1 - qb, row1) - key1
    slope2, slope1 = head_slope(row2), head_slope(row1)

    def aligned(x):
        return x if isinstance(x, int) else pl.multiple_of(x, qb)

    for bi, (window, d) in enumerate(DILATED_BRANCHES):
        sub = SEQ // d
        nb = sub // qb
        ksrc = k_ref if bi == 0 else dk_ref.at[bi - 1]
        bias_ref[...] = jnp.where((dist2 >= 0) & (dist2 <= window // d),
                                  -slope2 * (d * dist2).astype(F32), MASKED_SCORE)
        bias0_ref[...] = jnp.where(dist1 >= 0, -slope1 * (d * dist1).astype(F32), MASKED_SCORE)

        def rows_of(blk, d=d):
            base, first_token, has_prev, _ = blk
            rows = pl.ds(aligned(base), qb)
            krows = pl.ds(aligned(base - qb), 2 * qb) if has_prev else rows
            out_rows = rows if d == 1 else pl.ds(first_token, qb, stride=d)
            return rows, krows, out_rows, (2 * qb if has_prev else qb)

        def scores(blk, bi=bi, ksrc=ksrc, rows_of=rows_of):
            rows, krows, out_rows, nk = rows_of(blk)
            q2 = jnp.concatenate([dq_ref[bi, j, rows, :] for j in range(HEAD_PAIR)], axis=0)
            s_ref[blk[3], :, 0:nk] = lax.dot_general(
                q2, ksrc[krows, :], (((1,), (1,)), ((), ())), preferred_element_type=F32)

        def softmax(blk, bi=bi, rows_of=rows_of):
            rows, krows, out_rows, nk = rows_of(blk)
            s = s_ref[blk[3], :, 0:nk] + (bias_ref[...] if blk[2] else bias0_ref[...])
            m = jnp.max(s, axis=-1, keepdims=True)
            p_ref[blk[3], :, 0:nk] = jnp.exp(s - m).astype(BF16)
            for j in range(HEAD_PAIR):
                macc_ref.at[j, bi][out_rows, :] = jnp.broadcast_to(m[j * qb:(j + 1) * qb], (qb, LANES))

        def values(blk, bi=bi, rows_of=rows_of):
            rows, krows, out_rows, nk = rows_of(blk)
            for j in range(HEAD_PAIR):
                oacc_ref.at[j, bi][out_rows, :] = jnp.dot(
                    p_ref[blk[3], j * qb:(j + 1) * qb, 0:nk], dv_ref[bi, j, krows, :],
                    preferred_element_type=F32)

        def pipelined(n_groups, group, stages=(scores, softmax, values)):
            def step(t, valid):
                for k in reversed(range(len(stages))):
                    if valid(t - k):
                        for blk in group(t - k):
                            stages[k](blk)

            depth = len(stages) - 1
            head_steps = min(depth, n_groups)
            for t in range(head_steps):
                step(t, lambda g: 0 <= g < n_groups)
            if n_groups > depth:
                def body(t, carry):
                    step(t, lambda g: True)
                    return carry

                lax.fori_loop(depth, n_groups, body, 0)
            for t in range(max(n_groups, head_steps), n_groups + depth):
                step(t, lambda g: 0 <= g < n_groups)

        fu = min(d, ATT_FIRST_UNROLL)
        pipelined(d // fu, lambda it, fu=fu, sub=sub: [
            ((it * fu + u) * sub, it * fu + u, False, it * fu + u) for u in range(fu)])

        if nb > 1:
            lu = ATT_LATER_UNROLL
            per_res = (nb - 1) // lu
            assert per_res * lu == nb - 1

            def later_group(it, lu=lu, per_res=per_res, sub=sub, d=d):
                if per_res == 1:
                    r, n0 = it, 1
                elif d == 1:
                    r, n0 = 0, it * lu + 1
                elif isinstance(it, int):
                    r, n0 = it // per_res, (it % per_res) * lu + 1
                else:
                    r, n0 = lax.div(it, per_res), lax.rem(it, per_res) * lu + 1
                return [(r * sub + (n0 + u) * qb, r + d * qb * (n0 + u), True, d + it * lu + u)
                        for u in range(lu)]

            pipelined(d * per_res, later_group)

    lane_c = lax.broadcasted_iota(jnp.int32, (COMBINE_ROWS, LANES), 1)
    head1_c = lane_c >= dh

    def combine(tb, carry):
        rows = pl.ds(pl.multiple_of(tb * COMBINE_ROWS, COMBINE_ROWS), COMBINE_ROWS)
        res = []
        for j in range(HEAD_PAIR):
            ms = [macc_ref[j, bi, rows, :] for bi in range(len(DILATED_BRANCHES))]
            top = functools.reduce(jnp.maximum, ms)
            num = sum(jnp.exp(ms[bi] - top) * oacc_ref[j, bi, rows, :]
                      for bi in range(len(DILATED_BRANCHES)))
            res.append(num / pltpu.roll(num, dh, axis=1))
        y_ref[rows, :] = jnp.where(head1_c, res[1], res[0]).astype(BF16)
        return carry

    lax.fori_loop(0, SEQ // COMBINE_ROWS, combine, 0)


def _attention(za):
    pairs = ATT_WIDTH // LANES
    col = lambda off: pl.BlockSpec((None, SEQ, LANES), lambda b, hp: (b, 0, hp + off))
    n_dil = len(DILATED_BRANCHES) - 1
    return pl.pallas_call(
        _attn_kernel,
        grid=(BATCH, pairs),
        in_specs=[col(0), col(pairs), col(2 * pairs)],
        out_specs=col(0),
        out_shape=jax.ShapeDtypeStruct((BATCH, SEQ, ATT_WIDTH), BF16),
        scratch_shapes=[pltpu.VMEM((2, SEQ, LANES), F32),
                        pltpu.VMEM((n_dil + 1, HEAD_PAIR, SEQ, LANES), BF16),
                        pltpu.VMEM((n_dil, SEQ, LANES), BF16),
                        pltpu.VMEM((n_dil + 1, HEAD_PAIR, SEQ, LANES), BF16),
                        pltpu.VMEM((HEAD_PAIR, n_dil + 1, SEQ, LANES), F32),
                        pltpu.VMEM((HEAD_PAIR, n_dil + 1, SEQ, LANES), F32),
                        pltpu.VMEM((HEAD_PAIR * ATT_BLOCK, 2 * ATT_BLOCK), F32),
                        pltpu.VMEM((HEAD_PAIR * ATT_BLOCK, ATT_BLOCK), F32),
                        pltpu.VMEM((SEQ // ATT_BLOCK, HEAD_PAIR * ATT_BLOCK, 2 * ATT_BLOCK), BF16),
                        pltpu.VMEM((SEQ // ATT_BLOCK, HEAD_PAIR * ATT_BLOCK, 2 * ATT_BLOCK), F32)],
        compiler_params=_params("parallel", "parallel"),
        name="dilated_attention",
    )(za, za, za)


def _post_kernel(ys_ref, yr_ref, ya_ref, x_ref, wo_ref, an_ref, l1w_ref, l1b_ref,
                 w1_ref, w2_ref, l2w_ref, l2b_ref, o_ref):
    ya = ya_ref[...].astype(F32)
    ms = jnp.mean(ya * ya, axis=-1, keepdims=True)
    ya = (ya * lax.rsqrt(ms + LN_EPS) * an_ref[...]).astype(BF16)
    h = jnp.dot(ys_ref[...], wo_ref[0:SSM_WIDTH, :], preferred_element_type=F32)
    h = h + jnp.dot(yr_ref[...], wo_ref[SSM_WIDTH:SSM_WIDTH + RET_WIDTH, :], preferred_element_type=F32)
    h = h + jnp.dot(ya, wo_ref[SSM_WIDTH + RET_WIDTH:, :], preferred_element_type=F32)
    x1 = _layer_norm(DEEPNORM_ALPHA * x_ref[...] + h, l1w_ref[...], l1b_ref[...])
    xb = x1.astype(BF16)
    acc = jnp.zeros(x1.shape, F32)
    for c in range(0, D_FF, FF_CHUNK):
        hid = jnp.dot(xb, w1_ref[:, c:c + FF_CHUNK], preferred_element_type=F32)
        hid = jnp.square(jnp.maximum(hid, 0.0)).astype(BF16)
        acc = acc + jnp.dot(hid, w2_ref[c:c + FF_CHUNK, :], preferred_element_type=F32)
    o_ref[...] = _layer_norm(DEEPNORM_ALPHA * x1 + acc, l2w_ref[...], l2b_ref[...])


def _post(y_ssm, y_ret, y_att, x, w_out, attn_norm, ln1_w, ln1_b, w1, w2, ln2_w, ln2_b):
    tm = TOKEN_TILE
    row = lambda width: pl.BlockSpec((None, tm, width), lambda b, i: (b, i, 0))
    full = lambda shape: pl.BlockSpec(shape, lambda b, i: (0, 0), pipeline_mode=pl.Buffered(1))
    return pl.pallas_call(
        _post_kernel,
        grid=(BATCH, SEQ // tm),
        in_specs=[row(SSM_WIDTH), row(RET_WIDTH), row(ATT_WIDTH), row(D_MODEL),
                  full((D_MODEL, D_MODEL)), full((1, ATT_WIDTH)), full((1, D_MODEL)), full((1, D_MODEL)),
                  full((D_MODEL, D_FF)), full((D_FF, D_MODEL)), full((1, D_MODEL)), full((1, D_MODEL))],
        out_specs=row(D_MODEL),
        out_shape=jax.ShapeDtypeStruct((BATCH, SEQ, D_MODEL), F32),
        compiler_params=_params("parallel", "parallel"),
        name="out_proj_mlp",
    )(y_ssm, y_ret, y_att, x, w_out, attn_norm, ln1_w, ln1_b, w1, w2, ln2_w, ln2_b)


def kernel(x, w_in, ssm_lambda_re, ssm_lambda_im, ssm_b_re, ssm_b_im, ssm_c_re, ssm_c_im, ssm_d, ssm_log_dt, ssm_w_glu, ssm_b_glu, ssm_out_norm, ret_out_norm, attn_out_norm, w_out, ln1_w, ln1_b, mlp_w1, mlp_w2, ln2_w, ln2_b):
    vec = lambda p: p.astype(F32).reshape(1, -1)
    x = x.astype(F32)
    for i in range(DEPTH):
        u, zr, za = _in_proj(x, w_in[i].astype(BF16))
        wb, a_re, a_im, wc = _ssm_weights(ssm_lambda_re[i], ssm_lambda_im[i], ssm_b_re[i], ssm_b_im[i],
                                          ssm_c_re[i], ssm_c_im[i], ssm_log_dt[i])
        u_tm = jnp.transpose(u, (1, 0, 2)).reshape(SEQ * BATCH, SSM_WIDTH)
        y_ssm = _ssm(u_tm, wb, a_re, a_im, wc, vec(ssm_d[i]), ssm_w_glu[i].astype(BF16),
                     vec(ssm_b_glu[i]), vec(ssm_out_norm[i]))
        y_ssm = jnp.transpose(y_ssm.reshape(SEQ, BATCH, SSM_WIDTH), (1, 0, 2))
        y_ret = _retention(zr, vec(ret_out_norm[i]))
        y_att = _attention(za)
        x = _post(y_ssm, y_ret, y_att, x, w_out[i].astype(BF16), vec(attn_out_norm[i]),
                  vec(ln1_w[i]), vec(ln1_b[i]), mlp_w1[i].astype(BF16), mlp_w2[i].astype(BF16),
                  vec(ln2_w[i]), vec(ln2_b[i]))
    return x
```

```python
import functools
import math

import jax
import jax.numpy as jnp
from jax import lax
from jax.experimental import pallas as pl
from jax.experimental.pallas import tpu as pltpu

F32 = jnp.float32
BF16 = jnp.bfloat16

D_MODEL = 1024
BATCH = 16
SEQ = 2048
DEPTH = 2
SSM_WIDTH = 256
SSM_GROUP = 16
SSM_GROUPS = 16
SSM_STATE = 64
SSM_STATES = SSM_GROUPS * SSM_STATE
RET_HEAD_DIM = 64
RET_WIDTH = 256
RET_CHUNK = 128
ATT_HEAD_DIM = 64
ATT_WIDTH = 512
ATT_HEADS = 8
DILATED_BRANCHES = ((128, 1), (512, 4), (2048, 16))
ATT_BLOCK = 128
IN_WIDTH = SSM_WIDTH + 4 * RET_WIDTH + 3 * ATT_WIDTH
D_FF = 4 * D_MODEL
DEEPNORM_ALPHA = (2 * DEPTH) ** 0.25
LN_EPS = 1e-5

LANES = 128
HEAD_PAIR = LANES // ATT_HEAD_DIM
VMEM_LIMIT_BYTES = 56 * 1024 * 1024
MASKED_SCORE = -1e30

TOKEN_TILE = 512
POST_TILE = 1024
SSM_TIME_TILE = 32
SSM_COL_CHUNK = 256
FF_CHUNK = 1024
COMBINE_ROWS = 256
PREP_ROWS = 128
ATT_FIRST_UNROLL = 4
ATT_LATER_UNROLL = 3
RET_UNROLL = 8

assert RET_HEAD_DIM == ATT_HEAD_DIM and HEAD_PAIR == 2
assert all(w // d == ATT_BLOCK for w, d in DILATED_BRANCHES)
assert DILATED_BRANCHES[0][1] == 1 and all(
    b[1] % a[1] == 0 for a, b in zip(DILATED_BRANCHES, DILATED_BRANCHES[1:]))


def _params(*semantics):
    return pltpu.CompilerParams(dimension_semantics=semantics, vmem_limit_bytes=VMEM_LIMIT_BYTES)


def _sigmoid(x):
    return 1.0 / (1.0 + jnp.exp(-x))


def _layer_norm(r, w, b):
    mu = jnp.mean(r, axis=-1, keepdims=True)
    d = r - mu
    var = jnp.mean(d * d, axis=-1, keepdims=True)
    return d * lax.rsqrt(var + LN_EPS) * w + b


def _in_proj_kernel(x_ref, w_ref, u_ref, zr_ref, za_ref):
    xb = x_ref[...].astype(BF16)

    def proj(lo, hi):
        return jnp.dot(xb, w_ref[:, lo:hi], preferred_element_type=F32).astype(BF16)

    u_ref[...] = proj(0, SSM_WIDTH)
    zr_ref[...] = proj(SSM_WIDTH, SSM_WIDTH + 4 * RET_WIDTH)
    za_ref[...] = proj(SSM_WIDTH + 4 * RET_WIDTH, IN_WIDTH)


def _in_proj(x, w_in):
    tm = TOKEN_TILE
    row = lambda width: pl.BlockSpec((None, tm, width), lambda b, i: (b, i, 0))
    return pl.pallas_call(
        _in_proj_kernel,
        grid=(BATCH, SEQ // tm),
        in_specs=[row(D_MODEL), pl.BlockSpec((D_MODEL, IN_WIDTH), lambda b, i: (0, 0))],
        out_specs=[row(SSM_WIDTH), row(4 * RET_WIDTH), row(3 * ATT_WIDTH)],
        out_shape=[jax.ShapeDtypeStruct((BATCH, SEQ, SSM_WIDTH), BF16),
                   jax.ShapeDtypeStruct((BATCH, SEQ, 4 * RET_WIDTH), BF16),
                   jax.ShapeDtypeStruct((BATCH, SEQ, 3 * ATT_WIDTH), BF16)],
        compiler_params=_params("parallel", "parallel"),
        name="in_proj",
    )(x, w_in)


def _ssm_kernel(u_ref, wb_ref, are_ref, aim_ref, wc_ref, d_ref, wglu_ref, bglu_ref, nw_ref,
                y_ref, bu_ref, st_ref):
    n_st = SSM_STATES

    @pl.when(pl.program_id(0) == 0)
    def _():
        st_ref[...] = jnp.zeros_like(st_ref)

    u = u_ref[...]
    bu_ref[...] = jnp.dot(u, wb_ref[...], preferred_element_type=F32)

    for c in range(0, n_st, SSM_COL_CHUNK):
        re = slice(c, c + SSM_COL_CHUNK)
        im = slice(n_st + c, n_st + c + SSM_COL_CHUNK)
        ar = jnp.broadcast_to(are_ref[:, re], (BATCH, SSM_COL_CHUNK))
        ai = jnp.broadcast_to(aim_ref[:, re], (BATCH, SSM_COL_CHUNK))

        xr, xi = st_ref[:, re], st_ref[:, im]
        for t in range(SSM_TIME_TILE):
            rows = slice(t * BATCH, (t + 1) * BATCH)
            xr, xi = (ar * xr - ai * xi + bu_ref[rows, re],
                      ar * xi + ai * xr + bu_ref[rows, im])
            bu_ref[rows, re] = xr
            bu_ref[rows, im] = xi
        st_ref[:, re] = xr
        st_ref[:, im] = xi

    y = jnp.dot(bu_ref[...].astype(BF16), wc_ref[...], preferred_element_type=F32)
    y = y + d_ref[...] * u.astype(F32)
    cdf = 0.5 * (1.0 + jnp.tanh(math.sqrt(2.0 / math.pi) * (y + 0.044715 * (y * y * y))))
    g = y * cdf
    gate = jnp.dot(g.astype(BF16), wglu_ref[...], preferred_element_type=F32) + bglu_ref[...]
    out = g * _sigmoid(gate)
    ms = jnp.mean(out * out, axis=-1, keepdims=True)
    y_ref[...] = (out * lax.rsqrt(ms + LN_EPS) * nw_ref[...]).astype(BF16)


def _ssm(u_tm, wb, a_re, a_im, wc, d_skip, w_glu, b_glu, norm_w):
    rows = SSM_TIME_TILE * BATCH
    full = lambda shape: pl.BlockSpec(shape, lambda i: (0, 0))
    return pl.pallas_call(
        _ssm_kernel,
        grid=(SEQ // SSM_TIME_TILE,),
        in_specs=[pl.BlockSpec((rows, SSM_WIDTH), lambda i: (i, 0)),
                  full((SSM_WIDTH, 2 * SSM_STATES)), full((1, SSM_STATES)), full((1, SSM_STATES)),
                  full((2 * SSM_STATES, SSM_WIDTH)), full((1, SSM_WIDTH)),
                  full((SSM_WIDTH, SSM_WIDTH)), full((1, SSM_WIDTH)), full((1, SSM_WIDTH))],
        out_specs=pl.BlockSpec((rows, SSM_WIDTH), lambda i: (i, 0)),
        out_shape=jax.ShapeDtypeStruct((SEQ * BATCH, SSM_WIDTH), BF16),
        scratch_shapes=[pltpu.VMEM((rows, 2 * SSM_STATES), F32),
                        pltpu.VMEM((BATCH, 2 * SSM_STATES), F32)],
        compiler_params=_params("arbitrary"),
        name="s5_mixer",
    )(u_tm, wb, a_re, a_im, wc, d_skip, w_glu, b_glu, norm_w)


def _ssm_weights(lam_re, lam_im, b_re, b_im, c_re, c_im, log_dt):
    g, p, h = SSM_GROUPS, SSM_STATE, SSM_GROUP
    lr, li = lam_re.astype(F32), lam_im.astype(F32)
    dt = jnp.exp(log_dt.astype(F32))[:, None]
    mag = jnp.exp(lr * dt)
    a_re, a_im = mag * jnp.cos(li * dt), mag * jnp.sin(li * dt)
    den = lr * lr + li * li
    nr, ni = a_re - 1.0, a_im
    f_re = ((nr * lr + ni * li) / den)[..., None]
    f_im = ((ni * lr - nr * li) / den)[..., None]
    br, bi = b_re.astype(F32), b_im.astype(F32)
    bb_re = f_re * br - f_im * bi
    bb_im = f_re * bi + f_im * br
    eye = jnp.eye(g, dtype=F32)
    embed_b = lambda bb: jnp.einsum('gph,gk->ghkp', bb, eye).reshape(g * h, g * p)
    embed_c = lambda cc: jnp.einsum('ghp,gk->gpkh', cc, eye).reshape(g * p, g * h)
    wb = jnp.concatenate([embed_b(bb_re), embed_b(bb_im)], axis=1).astype(BF16)
    wc = jnp.concatenate([embed_c(c_re.astype(F32)), -embed_c(c_im.astype(F32))], axis=0).astype(BF16)
    return wb, a_re.reshape(1, g * p), a_im.reshape(1, g * p), wc


def _ret_kernel(q_ref, k_ref, v_ref, g_ref, nw_ref, y_ref, dmat_ref):
    c = RET_CHUNK
    dh = RET_HEAD_DIM
    hp = pl.program_id(1)
    lane = lax.broadcasted_iota(jnp.int32, (c, LANES), 1)
    row = lax.broadcasted_iota(jnp.int32, (c, LANES), 0)
    head1 = lane >= dh
    head_of_lane = (HEAD_PAIR * hp).astype(F32) + head1.astype(F32)
    lg = jnp.log(1.0 - jnp.exp2(-5.0 - head_of_lane))
    rowf = row.astype(F32)
    xi = jnp.exp((rowf + 1.0) * lg)
    zeta = jnp.exp((c - 1.0 - rowf) * lg) * (dh ** -0.5)
    g_chunk = jnp.exp(c * lg)
    block_diag = (row >= dh) == head1
    diff = (row - lane).astype(F32)
    for j in range(HEAD_PAIR):
        lg_j = jnp.log(1.0 - jnp.exp2(jnp.zeros((c, c), F32) - 5.0 - (HEAD_PAIR * hp + j).astype(F32)))
        dmat_ref[j] = jnp.where(diff >= 0, jnp.exp(jnp.maximum(diff, 0.0) * lg_j), 0.0) * (dh ** -0.5)
    nw = nw_ref[...]

    def head_mean(t):
        s0 = jnp.sum(jnp.where(head1, 0.0, t), axis=-1, keepdims=True)
        s1 = jnp.sum(jnp.where(head1, t, 0.0), axis=-1, keepdims=True)
        return jnp.where(head1, s1, s0) * (1.0 / dh)

    def chunk(n, r_prev):
        rows = pl.ds(pl.multiple_of(n * c, c), c)
        q, k, v = q_ref[rows, :], k_ref[rows, :], v_ref[rows, :]
        o = jnp.zeros((c, LANES), F32)
        for j in range(HEAD_PAIR):
            mine = head1 if j else jnp.logical_not(head1)
            qj = jnp.where(mine, q, jnp.zeros_like(q))
            s = lax.dot_general(qj, k, (((1,), (1,)), ((), ())), preferred_element_type=F32)
            s = s * dmat_ref[j]
            vj = jnp.where(mine, v, jnp.zeros_like(v))
            o = o + jnp.dot(s.astype(BF16), vj, preferred_element_type=F32)
        qx = (q.astype(F32) * xi).astype(BF16)
        o = o + jnp.dot(qx, r_prev.astype(BF16), preferred_element_type=F32)
        kz = (k.astype(F32) * zeta).T.astype(BF16)
        kv = jnp.dot(kz, v, preferred_element_type=F32)
        dlt = o - head_mean(o)
        var = head_mean(dlt * dlt)
        gate = g_ref[rows, :].astype(F32)
        y = dlt * lax.rsqrt(var + LN_EPS) * nw * (gate * _sigmoid(gate))
        y_ref[rows, :] = y.astype(BF16)
        return jnp.where(block_diag, g_chunk * r_prev + kv, 0.0)

    def chunks(it, r):
        for u in range(RET_UNROLL):
            r = chunk(it * RET_UNROLL + u, r)
        return r

    lax.fori_loop(0, SEQ // (c * RET_UNROLL), chunks, jnp.zeros((LANES, LANES), F32))


def _retention(zr, norm_w):
    col = lambda off: pl.BlockSpec((None, SEQ, LANES), lambda b, hp: (b, 0, hp + off))
    pairs = RET_WIDTH // LANES
    return pl.pallas_call(
        _ret_kernel,
        grid=(BATCH, pairs),
        in_specs=[col(0), col(pairs), col(2 * pairs), col(3 * pairs),
                  pl.BlockSpec((1, LANES), lambda b, hp: (0, hp))],
        out_specs=col(0),
        out_shape=jax.ShapeDtypeStruct((BATCH, SEQ, RET_WIDTH), BF16),
        scratch_shapes=[pltpu.VMEM((HEAD_PAIR, RET_CHUNK, RET_CHUNK), F32)],
        compiler_params=_params("parallel", "parallel"),
        name="retention",
    )(zr, zr, zr, zr, norm_w)


def _attn_kernel(q_ref, k_ref, v_ref, y_ref, src_ref, dq_ref, dk_ref, dv_ref,
                 oacc_ref, macc_ref, bias_ref, bias0_ref, p_ref, s_ref):
    qb = ATT_BLOCK
    dh = ATT_HEAD_DIM
    hp = pl.program_id(1)

    def emit_q(bi, rows, blk):
        other = lax.broadcasted_iota(jnp.int32, blk.shape, 1) >= dh
        blk = blk * (dh ** -0.5)
        dq_ref[bi, 0, rows, :] = jnp.where(other, 0.0, blk).astype(BF16)
        dq_ref[bi, 1, rows, :] = jnp.where(other, blk, 0.0).astype(BF16)

    def emit_k(bi, rows, blk):
        if bi > 0:
            dk_ref[bi - 1, rows, :] = blk.astype(BF16)

    def emit_v(bi, rows, blk):
        other = lax.broadcasted_iota(jnp.int32, blk.shape, 1) >= dh
        dv_ref[bi, 0, rows, :] = jnp.where(other, 1.0, blk).astype(BF16)
        dv_ref[bi, 1, rows, :] = jnp.where(other, blk, 1.0).astype(BF16)

    piece = PREP_ROWS
    for src, emit in ((q_ref, emit_q), (k_ref, emit_k), (v_ref, emit_v)):
        for c0 in range(0, SEQ, piece):
            rows = slice(c0, c0 + piece)
            natural = src[rows, :].astype(F32)
            src_ref[0, rows, :] = natural
            emit(0, rows, natural)
        d_prev = 1
        for bi in range(1, len(DILATED_BRANCHES)):
            d = DILATED_BRANCHES[bi][1]
            step = d // d_prev
            sub_prev, sub = SEQ // d_prev, SEQ // d
            cur, nxt = (bi - 1) % 2, bi % 2
            for r_prev in range(d_prev):
                for t in range(step):
                    r = r_prev + d_prev * t
                    for c0 in range(0, sub, piece):
                        n = min(piece, sub - c0)
                        rows = slice(r * sub + c0, r * sub + c0 + n)
                        blk = src_ref.at[cur][pl.ds(r_prev * sub_prev + t + step * c0, n, stride=step), :]
                        emit(bi, rows, blk)
                        if bi + 1 < len(DILATED_BRANCHES):
                            src_ref[nxt, rows, :] = blk
            d_prev = d

    row2 = lax.broadcasted_iota(jnp.int32, (HEAD_PAIR * qb, 2 * qb), 0)
    key2 = lax.broadcasted_iota(jnp.int32, (HEAD_PAIR * qb, 2 * qb), 1)
    row1 = lax.broadcasted_iota(jnp.int32, (HEAD_PAIR * qb, qb), 0)
    key1 = lax.broadcasted_iota(jnp.int32, (HEAD_PAIR * qb, qb), 1)

    def head_slope(row):
        head = (HEAD_PAIR * hp).astype(F32) + (row >= qb).astype(F32)
        return jnp.exp2(-(8.0 / ATT_HEADS) * (head + 1.0))

    dist2 = qb + jnp.where(row2 >= qb, row2 - qb, row2) - key2
    dist1 = jnp.where(row1 >= qb, row1 - qb, row1) - key1
    slope2, slope1 = head_slope(row2), head_slope(row1)

    def aligned(x):
        return x if isinstance(x, int) else pl.multiple_of(x, qb)

    for bi, (window, d) in enumerate(DILATED_BRANCHES):
        sub = SEQ // d
        nb = sub // qb
        ksrc = k_ref if bi == 0 else dk_ref.at[bi - 1]
        bias_ref[...] = jnp.where((dist2 >= 0) & (dist2 <= window // d),
                                  -slope2 * (d * dist2).astype(F32), MASKED_SCORE)
        bias0_ref[...] = jnp.where(dist1 >= 0, -slope1 * (d * dist1).astype(F32), MASKED_SCORE)

        def rows_of(blk, d=d):
            base, first_token, has_prev, _ = blk
            rows = pl.ds(aligned(base), qb)
            krows = pl.ds(aligned(base - qb), 2 * qb) if has_prev else rows
            out_rows = rows if d == 1 else pl.ds(first_token, qb, stride=d)
            return rows, krows, out_rows, (2 * qb if has_prev else qb)

        def scores(blk, bi=bi, ksrc=ksrc, rows_of=rows_of):
            rows, krows, out_rows, nk = rows_of(blk)
            q2 = jnp.concatenate([dq_ref[bi, j, rows, :] for j in range(HEAD_PAIR)], axis=0)
            s_ref[blk[3], :, 0:nk] = lax.dot_general(
                q2, ksrc[krows, :], (((1,), (1,)), ((), ())), preferred_element_type=F32)

        def softmax(blk, bi=bi, rows_of=rows_of):
            rows, krows, out_rows, nk = rows_of(blk)
            s = s_ref[blk[3], :, 0:nk] + (bias_ref[...] if blk[2] else bias0_ref[...])
            m = jnp.max(s, axis=-1, keepdims=True)
            p_ref[blk[3], :, 0:nk] = jnp.exp(s - m).astype(BF16)
            for j in range(HEAD_PAIR):
                macc_ref.at[j, bi][out_rows, :] = jnp.broadcast_to(m[j * qb:(j + 1) * qb], (qb, LANES))

        def values(blk, bi=bi, rows_of=rows_of):
            rows, krows, out_rows, nk = rows_of(blk)
            for j in range(HEAD_PAIR):
                oacc_ref.at[j, bi][out_rows, :] = jnp.dot(
                    p_ref[blk[3], j * qb:(j + 1) * qb, 0:nk], dv_ref[bi, j, krows, :],
                    preferred_element_type=F32)

        def pipelined(n_groups, group, stages=(scores, softmax, values)):
            def step(t, valid):
                for k in reversed(range(len(stages))):
                    if valid(t - k):
                        for blk in group(t - k):
                            stages[k](blk)

            depth = len(stages) - 1
            head_steps = min(depth, n_groups)
            for t in range(head_steps):
                step(t, lambda g: 0 <= g < n_groups)
            if n_groups > depth:
                def body(t, carry):
                    step(t, lambda g: True)
                    return carry

                lax.fori_loop(depth, n_groups, body, 0)
            for t in range(max(n_groups, head_steps), n_groups + depth):
                step(t, lambda g: 0 <= g < n_groups)

        fu = min(d, ATT_FIRST_UNROLL)
        pipelined(d // fu, lambda it, fu=fu, sub=sub: [
            ((it * fu + u) * sub, it * fu + u, False, it * fu + u) for u in range(fu)])

        if nb > 1:
            lu = ATT_LATER_UNROLL
            per_res = (nb - 1) // lu
            assert per_res * lu == nb - 1

            def later_group(it, lu=lu, per_res=per_res, sub=sub, d=d):
                if per_res == 1:
                    r, n0 = it, 1
                elif d == 1:
                    r, n0 = 0, it * lu + 1
                elif isinstance(it, int):
                    r, n0 = it // per_res, (it % per_res) * lu + 1
                else:
                    r, n0 = lax.div(it, per_res), lax.rem(it, per_res) * lu + 1
                return [(r * sub + (n0 + u) * qb, r + d * qb * (n0 + u), True, d + it * lu + u)
                        for u in range(lu)]

            pipelined(d * per_res, later_group)

    lane_c = lax.broadcasted_iota(jnp.int32, (COMBINE_ROWS, LANES), 1)
    head1_c = lane_c >= dh

    def combine(tb, carry):
        rows = pl.ds(pl.multiple_of(tb * COMBINE_ROWS, COMBINE_ROWS), COMBINE_ROWS)
        res = []
        for j in range(HEAD_PAIR):
            ms = [macc_ref[j, bi, rows, :] for bi in range(len(DILATED_BRANCHES))]
            top = functools.reduce(jnp.maximum, ms)
            num = sum(jnp.exp(ms[bi] - top) * oacc_ref[j, bi, rows, :]
                      for bi in range(len(DILATED_BRANCHES)))
            res.append(num / pltpu.roll(num, dh, axis=1))
        y_ref[rows, :] = jnp.where(head1_c, res[1], res[0]).astype(BF16)
        return carry

    lax.fori_loop(0, SEQ // COMBINE_ROWS, combine, 0)


def _attention(za):
    pairs = ATT_WIDTH // LANES
    col = lambda off: pl.BlockSpec((None, SEQ, LANES), lambda b, hp: (b, 0, hp + off))
    n_dil = len(DILATED_BRANCHES) - 1
    return pl.pallas_call(
        _attn_kernel,
        grid=(BATCH, pairs),
        in_specs=[col(0), col(pairs), col(2 * pairs)],
        out_specs=col(0),
        out_shape=jax.ShapeDtypeStruct((BATCH, SEQ, ATT_WIDTH), BF16),
        scratch_shapes=[pltpu.VMEM((2, SEQ, LANES), F32),
                        pltpu.VMEM((n_dil + 1, HEAD_PAIR, SEQ, LANES), BF16),
                        pltpu.VMEM((n_dil, SEQ, LANES), BF16),
                        pltpu.VMEM((n_dil + 1, HEAD_PAIR, SEQ, LANES), BF16),
                        pltpu.VMEM((HEAD_PAIR, n_dil + 1, SEQ, LANES), F32),
                        pltpu.VMEM((HEAD_PAIR, n_dil + 1, SEQ, LANES), F32),
                        pltpu.VMEM((HEAD_PAIR * ATT_BLOCK, 2 * ATT_BLOCK), F32),
                        pltpu.VMEM((HEAD_PAIR * ATT_BLOCK, ATT_BLOCK), F32),
                        pltpu.VMEM((SEQ // ATT_BLOCK, HEAD_PAIR * ATT_BLOCK, 2 * ATT_BLOCK), BF16),
                        pltpu.VMEM((SEQ // ATT_BLOCK, HEAD_PAIR * ATT_BLOCK, 2 * ATT_BLOCK), F32)],
        compiler_params=_params("parallel", "parallel"),
        name="dilated_attention",
    )(za, za, za)


def _post_kernel(ys_ref, yr_ref, ya_ref, x_ref, wo_ref, an_ref, l1w_ref, l1b_ref,
                 w1_ref, w2_ref, l2w_ref, l2b_ref, o_ref, x1_ref, acc_ref):
    half = POST_TILE // 2
    n_chunks = D_FF // FF_CHUNK
    slab = half // n_chunks
    halves = (slice(0, half), slice(half, POST_TILE))

    def out_proj(rows):
        ya = ya_ref[rows, :].astype(F32)
        ms = jnp.mean(ya * ya, axis=-1, keepdims=True)
        ya = (ya * lax.rsqrt(ms + LN_EPS) * an_ref[...]).astype(BF16)
        h = jnp.dot(ys_ref[rows, :], wo_ref[0:SSM_WIDTH, :], preferred_element_type=F32)
        h = h + jnp.dot(yr_ref[rows, :], wo_ref[SSM_WIDTH:SSM_WIDTH + RET_WIDTH, :],
                        preferred_element_type=F32)
        h = h + jnp.dot(ya, wo_ref[SSM_WIDTH + RET_WIDTH:, :], preferred_element_type=F32)
        o_ref[rows, :] = DEEPNORM_ALPHA * x_ref[rows, :] + h

    def ln1(rows):
        x1_ref[rows, :] = _layer_norm(o_ref[rows, :], l1w_ref[...], l1b_ref[...])

    def ln2(rows):
        o_ref[rows, :] = _layer_norm(DEEPNORM_ALPHA * x1_ref[rows, :] + acc_ref[rows, :],
                                     l2w_ref[...], l2b_ref[...])

    def mlp(rows, between):
        xb = x1_ref[rows, :].astype(BF16)
        acc = jnp.zeros((half, D_MODEL), F32)
        for k in range(n_chunks):
            c = k * FF_CHUNK
            hid = jnp.dot(xb, w1_ref[:, c:c + FF_CHUNK], preferred_element_type=F32)
            hid = jnp.square(jnp.maximum(hid, 0.0)).astype(BF16)
            acc = acc + jnp.dot(hid, w2_ref[c:c + FF_CHUNK, :], preferred_element_type=F32)
            between(k)
        acc_ref[rows, :] = acc

    slab_of = lambda rows, k: slice(rows.start + k * slab, rows.start + (k + 1) * slab)
    out_proj(halves[0])
    out_proj(halves[1])
    ln1(halves[0])
    mlp(halves[0], lambda k: ln1(slab_of(halves[1], k)))
    mlp(halves[1], lambda k: ln2(slab_of(halves[0], k)))
    ln2(halves[1])


def _post(y_ssm, y_ret, y_att, x, w_out, attn_norm, ln1_w, ln1_b, w1, w2, ln2_w, ln2_b):
    tm = POST_TILE
    row = lambda width: pl.BlockSpec((None, tm, width), lambda b, i: (b, i, 0))
    full = lambda shape: pl.BlockSpec(shape, lambda b, i: (0, 0), pipeline_mode=pl.Buffered(1))
    return pl.pallas_call(
        _post_kernel,
        grid=(BATCH, SEQ // tm),
        in_specs=[row(SSM_WIDTH), row(RET_WIDTH), row(ATT_WIDTH), row(D_MODEL),
                  full((D_MODEL, D_MODEL)), full((1, ATT_WIDTH)), full((1, D_MODEL)), full((1, D_MODEL)),
                  full((D_MODEL, D_FF)), full((D_FF, D_MODEL)), full((1, D_MODEL)), full((1, D_MODEL))],
        out_specs=row(D_MODEL),
        out_shape=jax.ShapeDtypeStruct((BATCH, SEQ, D_MODEL), F32),
        scratch_shapes=[pltpu.VMEM((tm, D_MODEL), F32), pltpu.VMEM((tm, D_MODEL), F32)],
        compiler_params=_params("parallel", "parallel"),
        name="out_proj_mlp",
    )(y_ssm, y_ret, y_att, x, w_out, attn_norm, ln1_w, ln1_b, w1, w2, ln2_w, ln2_b)


def kernel(x, w_in, ssm_lambda_re, ssm_lambda_im, ssm_b_re, ssm_b_im, ssm_c_re, ssm_c_im, ssm_d, ssm_log_dt, ssm_w_glu, ssm_b_glu, ssm_out_norm, ret_out_norm, attn_out_norm, w_out, ln1_w, ln1_b, mlp_w1, mlp_w2, ln2_w, ln2_b):
    vec = lambda p: p.astype(F32).reshape(1, -1)
    x = x.astype(F32)
    for i in range(DEPTH):
        u, zr, za = _in_proj(x, w_in[i].astype(BF16))
        wb, a_re, a_im, wc = _ssm_weights(ssm_lambda_re[i], ssm_lambda_im[i], ssm_b_re[i], ssm_b_im[i],
                                          ssm_c_re[i], ssm_c_im[i], ssm_log_dt[i])
        u_tm = jnp.transpose(u, (1, 0, 2)).reshape(SEQ * BATCH, SSM_WIDTH)
        y_ssm = _ssm(u_tm, wb, a_re, a_im, wc, vec(ssm_d[i]), ssm_w_glu[i].astype(BF16),
                     vec(ssm_b_glu[i]), vec(ssm_out_norm[i]))
        y_ssm = jnp.transpose(y_ssm.reshape(SEQ, BATCH, SSM_WIDTH), (1, 0, 2))
        y_ret = _retention(zr, vec(ret_out_norm[i]))
        y_att = _attention(za)
        x = _post(y_ssm, y_ret, y_att, x, w_out[i].astype(BF16), vec(attn_out_norm[i]),
                  vec(ln1_w[i]), vec(ln1_b[i]), mlp_w1[i].astype(BF16), mlp_w2[i].astype(BF16),
                  vec(ln2_w[i]), vec(ln2_b[i]))
    return x
```

```python
import functools
import math

import jax
import jax.numpy as jnp
from jax import lax
from jax.experimental import pallas as pl
from jax.experimental.pallas import tpu as pltpu

F32 = jnp.float32
BF16 = jnp.bfloat16

D_MODEL = 1024
BATCH = 16
SEQ = 2048
DEPTH = 2
SSM_WIDTH = 256
SSM_GROUP = 16
SSM_GROUPS = 16
SSM_STATE = 64
SSM_STATES = SSM_GROUPS * SSM_STATE
RET_HEAD_DIM = 64
RET_WIDTH = 256
RET_CHUNK = 128
ATT_HEAD_DIM = 64
ATT_WIDTH = 512
ATT_HEADS = 8
DILATED_BRANCHES = ((128, 1), (512, 4), (2048, 16))
ATT_BLOCK = 128
IN_WIDTH = SSM_WIDTH + 4 * RET_WIDTH + 3 * ATT_WIDTH
D_FF = 4 * D_MODEL
DEEPNORM_ALPHA = (2 * DEPTH) ** 0.25
LN_EPS = 1e-5

LANES = 128
HEAD_PAIR = LANES // ATT_HEAD_DIM
VMEM_LIMIT_BYTES = 56 * 1024 * 1024
MASKED_SCORE = -1e30

TOKEN_TILE = 512
POST_TILE = 1024
SSM_TIME_TILE = 64
SSM_BATCH_HALF = BATCH // 2
SSM_COL_CHUNK = 512
FF_CHUNK = 1024
COMBINE_ROWS = 256
PREP_ROWS = 128
ATT_FIRST_UNROLL = 4
ATT_LATER_UNROLL = 3
RET_UNROLL = 8

assert RET_HEAD_DIM == ATT_HEAD_DIM and HEAD_PAIR == 2
assert all(w // d == ATT_BLOCK for w, d in DILATED_BRANCHES)
assert DILATED_BRANCHES[0][1] == 1 and all(
    b[1] % a[1] == 0 for a, b in zip(DILATED_BRANCHES, DILATED_BRANCHES[1:]))


def _params(*semantics):
    return pltpu.CompilerParams(dimension_semantics=semantics, vmem_limit_bytes=VMEM_LIMIT_BYTES)


def _sigmoid(x):
    return 1.0 / (1.0 + jnp.exp(-x))


def _layer_norm(r, w, b):
    mu = jnp.mean(r, axis=-1, keepdims=True)
    d = r - mu
    var = jnp.mean(d * d, axis=-1, keepdims=True)
    return d * lax.rsqrt(var + LN_EPS) * w + b


def _in_proj_kernel(x_ref, w_ref, u_ref, zr_ref, za_ref):
    xb = x_ref[...].astype(BF16)

    def proj(lo, hi):
        return jnp.dot(xb, w_ref[:, lo:hi], preferred_element_type=F32).astype(BF16)

    u_ref[...] = proj(0, SSM_WIDTH)
    zr_ref[...] = proj(SSM_WIDTH, SSM_WIDTH + 4 * RET_WIDTH)
    za_ref[...] = proj(SSM_WIDTH + 4 * RET_WIDTH, IN_WIDTH)


def _in_proj(x, w_in):
    tm = TOKEN_TILE
    row = lambda width: pl.BlockSpec((None, tm, width), lambda b, i: (b, i, 0))
    return pl.pallas_call(
        _in_proj_kernel,
        grid=(BATCH, SEQ // tm),
        in_specs=[row(D_MODEL), pl.BlockSpec((D_MODEL, IN_WIDTH), lambda b, i: (0, 0))],
        out_specs=[row(SSM_WIDTH), row(4 * RET_WIDTH), row(3 * ATT_WIDTH)],
        out_shape=[jax.ShapeDtypeStruct((BATCH, SEQ, SSM_WIDTH), BF16),
                   jax.ShapeDtypeStruct((BATCH, SEQ, 4 * RET_WIDTH), BF16),
                   jax.ShapeDtypeStruct((BATCH, SEQ, 3 * ATT_WIDTH), BF16)],
        compiler_params=_params("parallel", "parallel"),
        name="in_proj",
    )(x, w_in)


def _ssm_kernel(u_ref, wb_ref, are_ref, aim_ref, wc_ref, d_ref, wglu_ref, bglu_ref, nw_ref,
                y_ref, bu_ref, st_ref):
    n_st = SSM_STATES
    bh = SSM_BATCH_HALF
    chunks = [(slice(c, c + SSM_COL_CHUNK), slice(n_st + c, n_st + c + SSM_COL_CHUNK))
              for c in range(0, n_st, SSM_COL_CHUNK)]

    @pl.when(pl.program_id(0) == 0)
    def _():
        st_ref[...] = jnp.zeros_like(st_ref)

    def project(h, cols):
        for sl in cols:
            bu_ref[h, :, sl] = jnp.dot(u_ref[h], wb_ref[:, sl], preferred_element_type=F32)

    def scan(h, cols):
        re, im = cols
        ar = jnp.broadcast_to(are_ref[:, re], (bh, SSM_COL_CHUNK))
        ai = jnp.broadcast_to(aim_ref[:, re], (bh, SSM_COL_CHUNK))
        xr, xi = st_ref[h, :, re], st_ref[h, :, im]
        for t in range(SSM_TIME_TILE):
            rows = slice(t * bh, (t + 1) * bh)
            xr, xi = (ar * xr - ai * xi + bu_ref[h, rows, re],
                      ar * xi + ai * xr + bu_ref[h, rows, im])
            bu_ref[h, rows, re] = xr
            bu_ref[h, rows, im] = xi
        st_ref[h, :, re] = xr
        st_ref[h, :, im] = xi

    def readout(h, cols, acc):
        for sl in cols:
            acc = acc + jnp.dot(bu_ref[h, :, sl].astype(BF16), wc_ref[sl, :], preferred_element_type=F32)
        return acc

    def finish(h, y):
        y = y + d_ref[...] * u_ref[h].astype(F32)
        cdf = 0.5 * (1.0 + jnp.tanh(math.sqrt(2.0 / math.pi) * (y + 0.044715 * (y * y * y))))
        g = y * cdf
        gate = jnp.dot(g.astype(BF16), wglu_ref[...], preferred_element_type=F32) + bglu_ref[...]
        out = g * _sigmoid(gate)
        ms = jnp.mean(out * out, axis=-1, keepdims=True)
        y_ref[h] = (out * lax.rsqrt(ms + LN_EPS) * nw_ref[...]).astype(BF16)

    zero = jnp.zeros((SSM_TIME_TILE * bh, SSM_WIDTH), F32)
    for cols in chunks:
        project(0, cols)
    for cols in chunks:
        project(1, cols)
        scan(0, cols)
    y0 = zero
    for cols in chunks:
        y0 = readout(0, cols, y0)
        scan(1, cols)
    y1 = zero
    for k, cols in enumerate(chunks):
        y1 = readout(1, cols, y1)
        if k == 0:
            finish(0, y0)
    finish(1, y1)


def _ssm(u_tm, wb, a_re, a_im, wc, d_skip, w_glu, b_glu, norm_w):
    rows = SSM_TIME_TILE * SSM_BATCH_HALF
    full = lambda shape: pl.BlockSpec(shape, lambda i: (0, 0))
    halves = pl.BlockSpec((2, rows, SSM_WIDTH), lambda i: (0, i, 0))
    return pl.pallas_call(
        _ssm_kernel,
        grid=(SEQ // SSM_TIME_TILE,),
        in_specs=[halves,
                  full((SSM_WIDTH, 2 * SSM_STATES)), full((1, SSM_STATES)), full((1, SSM_STATES)),
                  full((2 * SSM_STATES, SSM_WIDTH)), full((1, SSM_WIDTH)),
                  full((SSM_WIDTH, SSM_WIDTH)), full((1, SSM_WIDTH)), full((1, SSM_WIDTH))],
        out_specs=halves,
        out_shape=jax.ShapeDtypeStruct((2, SEQ * SSM_BATCH_HALF, SSM_WIDTH), BF16),
        scratch_shapes=[pltpu.VMEM((2, rows, 2 * SSM_STATES), F32),
                        pltpu.VMEM((2, SSM_BATCH_HALF, 2 * SSM_STATES), F32)],
        compiler_params=_params("arbitrary"),
        name="s5_mixer",
    )(u_tm, wb, a_re, a_im, wc, d_skip, w_glu, b_glu, norm_w)


def _ssm_weights(lam_re, lam_im, b_re, b_im, c_re, c_im, log_dt):
    g, p, h = SSM_GROUPS, SSM_STATE, SSM_GROUP
    lr, li = lam_re.astype(F32), lam_im.astype(F32)
    dt = jnp.exp(log_dt.astype(F32))[:, None]
    mag = jnp.exp(lr * dt)
    a_re, a_im = mag * jnp.cos(li * dt), mag * jnp.sin(li * dt)
    den = lr * lr + li * li
    nr, ni = a_re - 1.0, a_im
    f_re = ((nr * lr + ni * li) / den)[..., None]
    f_im = ((ni * lr - nr * li) / den)[..., None]
    br, bi = b_re.astype(F32), b_im.astype(F32)
    bb_re = f_re * br - f_im * bi
    bb_im = f_re * bi + f_im * br
    eye = jnp.eye(g, dtype=F32)
    embed_b = lambda bb: jnp.einsum('gph,gk->ghkp', bb, eye).reshape(g * h, g * p)
    embed_c = lambda cc: jnp.einsum('ghp,gk->gpkh', cc, eye).reshape(g * p, g * h)
    wb = jnp.concatenate([embed_b(bb_re), embed_b(bb_im)], axis=1).astype(BF16)
    wc = jnp.concatenate([embed_c(c_re.astype(F32)), -embed_c(c_im.astype(F32))], axis=0).astype(BF16)
    return wb, a_re.reshape(1, g * p), a_im.reshape(1, g * p), wc


def _ret_kernel(q_ref, k_ref, v_ref, g_ref, nw_ref, y_ref, dmat_ref):
    c = RET_CHUNK
    dh = RET_HEAD_DIM
    hp = pl.program_id(1)
    lane = lax.broadcasted_iota(jnp.int32, (c, LANES), 1)
    row = lax.broadcasted_iota(jnp.int32, (c, LANES), 0)
    head1 = lane >= dh
    head_of_lane = (HEAD_PAIR * hp).astype(F32) + head1.astype(F32)
    lg = jnp.log(1.0 - jnp.exp2(-5.0 - head_of_lane))
    rowf = row.astype(F32)
    xi = jnp.exp((rowf + 1.0) * lg)
    zeta = jnp.exp((c - 1.0 - rowf) * lg) * (dh ** -0.5)
    g_chunk = jnp.exp(c * lg)
    block_diag = (row >= dh) == head1
    diff = (row - lane).astype(F32)
    for j in range(HEAD_PAIR):
        lg_j = jnp.log(1.0 - jnp.exp2(jnp.zeros((c, c), F32) - 5.0 - (HEAD_PAIR * hp + j).astype(F32)))
        dmat_ref[j] = jnp.where(diff >= 0, jnp.exp(jnp.maximum(diff, 0.0) * lg_j), 0.0) * (dh ** -0.5)
    nw = nw_ref[...]

    def head_mean(t):
        s0 = jnp.sum(jnp.where(head1, 0.0, t), axis=-1, keepdims=True)
        s1 = jnp.sum(jnp.where(head1, t, 0.0), axis=-1, keepdims=True)
        return jnp.where(head1, s1, s0) * (1.0 / dh)

    def chunk(n, r_prev):
        rows = pl.ds(pl.multiple_of(n * c, c), c)
        q, k, v = q_ref[rows, :], k_ref[rows, :], v_ref[rows, :]
        o = jnp.zeros((c, LANES), F32)
        for j in range(HEAD_PAIR):
            mine = head1 if j else jnp.logical_not(head1)
            qj = jnp.where(mine, q, jnp.zeros_like(q))
            s = lax.dot_general(qj, k, (((1,), (1,)), ((), ())), preferred_element_type=F32)
            s = s * dmat_ref[j]
            vj = jnp.where(mine, v, jnp.zeros_like(v))
            o = o + jnp.dot(s.astype(BF16), vj, preferred_element_type=F32)
        qx = (q.astype(F32) * xi).astype(BF16)
        o = o + jnp.dot(qx, r_prev.astype(BF16), preferred_element_type=F32)
        kz = (k.astype(F32) * zeta).T.astype(BF16)
        kv = jnp.dot(kz, v, preferred_element_type=F32)
        dlt = o - head_mean(o)
        var = head_mean(dlt * dlt)
        gate = g_ref[rows, :].astype(F32)
        y = dlt * lax.rsqrt(var + LN_EPS) * nw * (gate * _sigmoid(gate))
        y_ref[rows, :] = y.astype(BF16)
        return jnp.where(block_diag, g_chunk * r_prev + kv, 0.0)

    def chunks(it, r):
        for u in range(RET_UNROLL):
            r = chunk(it * RET_UNROLL + u, r)
        return r

    lax.fori_loop(0, SEQ // (c * RET_UNROLL), chunks, jnp.zeros((LANES, LANES), F32))


def _retention(zr, norm_w):
    col = lambda off: pl.BlockSpec((None, SEQ, LANES), lambda b, hp: (b, 0, hp + off))
    pairs = RET_WIDTH // LANES
    return pl.pallas_call(
        _ret_kernel,
        grid=(BATCH, pairs),
        in_specs=[col(0), col(pairs), col(2 * pairs), col(3 * pairs),
                  pl.BlockSpec((1, LANES), lambda b, hp: (0, hp))],
        out_specs=col(0),
        out_shape=jax.ShapeDtypeStruct((BATCH, SEQ, RET_WIDTH), BF16),
        scratch_shapes=[pltpu.VMEM((HEAD_PAIR, RET_CHUNK, RET_CHUNK), F32)],
        compiler_params=_params("parallel", "parallel"),
        name="retention",
    )(zr, zr, zr, zr, norm_w)


def _attn_kernel(q_ref, k_ref, v_ref, y_ref, src_ref, dq_ref, dk_ref, dv_ref,
                 oacc_ref, macc_ref, bias_ref, bias0_ref, p_ref, s_ref):
    qb = ATT_BLOCK
    dh = ATT_HEAD_DIM
    hp = pl.program_id(1)

    def emit_q(bi, rows, blk):
        other = lax.broadcasted_iota(jnp.int32, blk.shape, 1) >= dh
        blk = blk * (dh ** -0.5)
        dq_ref[bi, 0, rows, :] = jnp.where(other, 0.0, blk).astype(BF16)
        dq_ref[bi, 1, rows, :] = jnp.where(other, blk, 0.0).astype(BF16)

    def emit_k(bi, rows, blk):
        if bi > 0:
            dk_ref[bi - 1, rows, :] = blk.astype(BF16)

    def emit_v(bi, rows, blk):
        other = lax.broadcasted_iota(jnp.int32, blk.shape, 1) >= dh
        dv_ref[bi, 0, rows, :] = jnp.where(other, 1.0, blk).astype(BF16)
        dv_ref[bi, 1, rows, :] = jnp.where(other, blk, 1.0).astype(BF16)

    piece = PREP_ROWS
    for src, emit in ((q_ref, emit_q), (k_ref, emit_k), (v_ref, emit_v)):
        for c0 in range(0, SEQ, piece):
            rows = slice(c0, c0 + piece)
            natural = src[rows, :].astype(F32)
            src_ref[0, rows, :] = natural
            emit(0, rows, natural)
        d_prev = 1
        for bi in range(1, len(DILATED_BRANCHES)):
            d = DILATED_BRANCHES[bi][1]
            step = d // d_prev
            sub_prev, sub = SEQ // d_prev, SEQ // d
            cur, nxt = (bi - 1) % 2, bi % 2
            for r_prev in range(d_prev):
                for t in range(step):
                    r = r_prev + d_prev * t
                    for c0 in range(0, sub, piece):
                        n = min(piece, sub - c0)
                        rows = slice(r * sub + c0, r * sub + c0 + n)
                        blk = src_ref.at[cur][pl.ds(r_prev * sub_prev + t + step * c0, n, stride=step), :]
                        emit(bi, rows, blk)
                        if bi + 1 < len(DILATED_BRANCHES):
                            src_ref[nxt, rows, :] = blk
            d_prev = d

    row2 = lax.broadcasted_iota(jnp.int32, (HEAD_PAIR * qb, 2 * qb), 0)
    key2 = lax.broadcasted_iota(jnp.int32, (HEAD_PAIR * qb, 2 * qb), 1)
    row1 = lax.broadcasted_iota(jnp.int32, (HEAD_PAIR * qb, qb), 0)
    key1 = lax.broadcasted_iota(jnp.int32, (HEAD_PAIR * qb, qb), 1)

    def head_slope(row):
        head = (HEAD_PAIR * hp).astype(F32) + (row >= qb).astype(F32)
        return jnp.exp2(-(8.0 / ATT_HEADS) * (head + 1.0))

    dist2 = qb + jnp.where(row2 >= qb, row2 - qb, row2) - key2
    dist1 = jnp.where(row1 >= qb, row1 - qb, row1) - key1
    slope2, slope1 = head_slope(row2), head_slope(row1)

    def aligned(x):
        return x if isinstance(x, int) else pl.multiple_of(x, qb)

    for bi, (window, d) in enumerate(DILATED_BRANCHES):
        sub = SEQ // d
        nb = sub // qb
        ksrc = k_ref if bi == 0 else dk_ref.at[bi - 1]
        bias_ref[...] = jnp.where((dist2 >= 0) & (dist2 <= window // d),
                                  -slope2 * (d * dist2).astype(F32), MASKED_SCORE)
        bias0_ref[...] = jnp.where(dist1 >= 0, -slope1 * (d * dist1).astype(F32), MASKED_SCORE)

        def rows_of(blk, d=d):
            base, first_token, has_prev, _ = blk
            rows = pl.ds(aligned(base), qb)
            krows = pl.ds(aligned(base - qb), 2 * qb) if has_prev else rows
            out_rows = rows if d == 1 else pl.ds(first_token, qb, stride=d)
            return rows, krows, out_rows, (2 * qb if has_prev else qb)

        def scores(blk, bi=bi, ksrc=ksrc, rows_of=rows_of):
            rows, krows, out_rows, nk = rows_of(blk)
            q2 = jnp.concatenate([dq_ref[bi, j, rows, :] for j in range(HEAD_PAIR)], axis=0)
            s_ref[blk[3], :, 0:nk] = lax.dot_general(
                q2, ksrc[krows, :], (((1,), (1,)), ((), ())), preferred_element_type=F32)

        def softmax(blk, bi=bi, rows_of=rows_of):
            rows, krows, out_rows, nk = rows_of(blk)
            s = s_ref[blk[3], :, 0:nk] + (bias_ref[...] if blk[2] else bias0_ref[...])
            m = jnp.max(s, axis=-1, keepdims=True)
            p_ref[blk[3], :, 0:nk] = jnp.exp(s - m).astype(BF16)
            for j in range(HEAD_PAIR):
                macc_ref.at[j, bi][out_rows, :] = jnp.broadcast_to(m[j * qb:(j + 1) * qb], (qb, LANES))

        def values(blk, bi=bi, rows_of=rows_of):
            rows, krows, out_rows, nk = rows_of(blk)
            for j in range(HEAD_PAIR):
                oacc_ref.at[j, bi][out_rows, :] = jnp.dot(
                    p_ref[blk[3], j * qb:(j + 1) * qb, 0:nk], dv_ref[bi, j, krows, :],
                    preferred_element_type=F32)

        def pipelined(n_groups, group, stages=(scores, softmax, values)):
            def step(t, valid):
                for k in reversed(range(len(stages))):
                    if valid(t - k):
                        for blk in group(t - k):
                            stages[k](blk)

            depth = len(stages) - 1
            head_steps = min(depth, n_groups)
            for t in range(head_steps):
                step(t, lambda g: 0 <= g < n_groups)
            if n_groups > depth:
                def body(t, carry):
                    step(t, lambda g: True)
                    return carry

                lax.fori_loop(depth, n_groups, body, 0)
            for t in range(max(n_groups, head_steps), n_groups + depth):
                step(t, lambda g: 0 <= g < n_groups)

        fu = min(d, ATT_FIRST_UNROLL)
        pipelined(d // fu, lambda it, fu=fu, sub=sub: [
            ((it * fu + u) * sub, it * fu + u, False, it * fu + u) for u in range(fu)])

        if nb > 1:
            lu = ATT_LATER_UNROLL
            per_res = (nb - 1) // lu
            assert per_res * lu == nb - 1

            def later_group(it, lu=lu, per_res=per_res, sub=sub, d=d):
                if per_res == 1:
                    r, n0 = it, 1
                elif d == 1:
                    r, n0 = 0, it * lu + 1
                elif isinstance(it, int):
                    r, n0 = it // per_res, (it % per_res) * lu + 1
                else:
                    r, n0 = lax.div(it, per_res), lax.rem(it, per_res) * lu + 1
                return [(r * sub + (n0 + u) * qb, r + d * qb * (n0 + u), True, d + it * lu + u)
                        for u in range(lu)]

            pipelined(d * per_res, later_group)

    lane_c = lax.broadcasted_iota(jnp.int32, (COMBINE_ROWS, LANES), 1)
    head1_c = lane_c >= dh

    def combine(tb, carry):
        rows = pl.ds(pl.multiple_of(tb * COMBINE_ROWS, COMBINE_ROWS), COMBINE_ROWS)
        res = []
        for j in range(HEAD_PAIR):
            ms = [macc_ref[j, bi, rows, :] for bi in range(len(DILATED_BRANCHES))]
            top = functools.reduce(jnp.maximum, ms)
            num = sum(jnp.exp(ms[bi] - top) * oacc_ref[j, bi, rows, :]
                      for bi in range(len(DILATED_BRANCHES)))
            res.append(num / pltpu.roll(num, dh, axis=1))
        y_ref[rows, :] = jnp.where(head1_c, res[1], res[0]).astype(BF16)
        return carry

    lax.fori_loop(0, SEQ // COMBINE_ROWS, combine, 0)


def _attention(za):
    pairs = ATT_WIDTH // LANES
    col = lambda off: pl.BlockSpec((None, SEQ, LANES), lambda b, hp: (b, 0, hp + off))
    n_dil = len(DILATED_BRANCHES) - 1
    return pl.pallas_call(
        _attn_kernel,
        grid=(BATCH, pairs),
        in_specs=[col(0), col(pairs), col(2 * pairs)],
        out_specs=col(0),
        out_shape=jax.ShapeDtypeStruct((BATCH, SEQ, ATT_WIDTH), BF16),
        scratch_shapes=[pltpu.VMEM((2, SEQ, LANES), F32),
                        pltpu.VMEM((n_dil + 1, HEAD_PAIR, SEQ, LANES), BF16),
                        pltpu.VMEM((n_dil, SEQ, LANES), BF16),
                        pltpu.VMEM((n_dil + 1, HEAD_PAIR, SEQ, LANES), BF16),
                        pltpu.VMEM((HEAD_PAIR, n_dil + 1, SEQ, LANES), F32),
                        pltpu.VMEM((HEAD_PAIR, n_dil + 1, SEQ, LANES), F32),
                        pltpu.VMEM((HEAD_PAIR * ATT_BLOCK, 2 * ATT_BLOCK), F32),
                        pltpu.VMEM((HEAD_PAIR * ATT_BLOCK, ATT_BLOCK), F32),
                        pltpu.VMEM((SEQ // ATT_BLOCK, HEAD_PAIR * ATT_BLOCK, 2 * ATT_BLOCK), BF16),
                        pltpu.VMEM((SEQ // ATT_BLOCK, HEAD_PAIR * ATT_BLOCK, 2 * ATT_BLOCK), F32)],
        compiler_params=_params("parallel", "parallel"),
        name="dilated_attention",
    )(za, za, za)


def _post_kernel(ys_ref, yr_ref, ya_ref, x_ref, wo_ref, an_ref, l1w_ref, l1b_ref,
                 w1_ref, w2_ref, l2w_ref, l2b_ref, o_ref, x1_ref, acc_ref):
    half = POST_TILE // 2
    n_chunks = D_FF // FF_CHUNK
    slab = half // n_chunks
    halves = (slice(0, half), slice(half, POST_TILE))

    def out_proj(rows):
        ya = ya_ref[rows, :].astype(F32)
        ms = jnp.mean(ya * ya, axis=-1, keepdims=True)
        ya = (ya * lax.rsqrt(ms + LN_EPS) * an_ref[...]).astype(BF16)
        h = jnp.dot(ys_ref[rows, :], wo_ref[0:SSM_WIDTH, :], preferred_element_type=F32)
        h = h + jnp.dot(yr_ref[rows, :], wo_ref[SSM_WIDTH:SSM_WIDTH + RET_WIDTH, :],
                        preferred_element_type=F32)
        h = h + jnp.dot(ya, wo_ref[SSM_WIDTH + RET_WIDTH:, :], preferred_element_type=F32)
        o_ref[rows, :] = DEEPNORM_ALPHA * x_ref[rows, :] + h

    def ln1(rows):
        x1_ref[rows, :] = _layer_norm(o_ref[rows, :], l1w_ref[...], l1b_ref[...])

    def ln2(rows):
        o_ref[rows, :] = _layer_norm(DEEPNORM_ALPHA * x1_ref[rows, :] + acc_ref[rows, :],
                                     l2w_ref[...], l2b_ref[...])

    def mlp(rows, between):
        xb = x1_ref[rows, :].astype(BF16)
        acc = jnp.zeros((half, D_MODEL), F32)
        for k in range(n_chunks):
            c = k * FF_CHUNK
            hid = jnp.dot(xb, w1_ref[:, c:c + FF_CHUNK], preferred_element_type=F32)
            hid = jnp.square(jnp.maximum(hid, 0.0)).astype(BF16)
            acc = acc + jnp.dot(hid, w2_ref[c:c + FF_CHUNK, :], preferred_element_type=F32)
            between(k)
        acc_ref[rows, :] = acc

    slab_of = lambda rows, k: slice(rows.start + k * slab, rows.start + (k + 1) * slab)
    out_proj(halves[0])
    out_proj(halves[1])
    ln1(halves[0])
    mlp(halves[0], lambda k: ln1(slab_of(halves[1], k)))
    mlp(halves[1], lambda k: ln2(slab_of(halves[0], k)))
    ln2(halves[1])


def _post(y_ssm, y_ret, y_att, x, w_out, attn_norm, ln1_w, ln1_b, w1, w2, ln2_w, ln2_b):
    tm = POST_TILE
    row = lambda width: pl.BlockSpec((None, tm, width), lambda b, i: (b, i, 0))
    full = lambda shape: pl.BlockSpec(shape, lambda b, i: (0, 0), pipeline_mode=pl.Buffered(1))
    return pl.pallas_call(
        _post_kernel,
        grid=(BATCH, SEQ // tm),
        in_specs=[row(SSM_WIDTH), row(RET_WIDTH), row(ATT_WIDTH), row(D_MODEL),
                  full((D_MODEL, D_MODEL)), full((1, ATT_WIDTH)), full((1, D_MODEL)), full((1, D_MODEL)),
                  full((D_MODEL, D_FF)), full((D_FF, D_MODEL)), full((1, D_MODEL)), full((1, D_MODEL))],
        out_specs=row(D_MODEL),
        out_shape=jax.ShapeDtypeStruct((BATCH, SEQ, D_MODEL), F32),
        scratch_shapes=[pltpu.VMEM((tm, D_MODEL), F32), pltpu.VMEM((tm, D_MODEL), F32)],
        compiler_params=_params("parallel", "parallel"),
        name="out_proj_mlp",
    )(y_ssm, y_ret, y_att, x, w_out, attn_norm, ln1_w, ln1_b, w1, w2, ln2_w, ln2_b)


def kernel(x, w_in, ssm_lambda_re, ssm_lambda_im, ssm_b_re, ssm_b_im, ssm_c_re, ssm_c_im, ssm_d, ssm_log_dt, ssm_w_glu, ssm_b_glu, ssm_out_norm, ret_out_norm, attn_out_norm, w_out, ln1_w, ln1_b, mlp_w1, mlp_w2, ln2_w, ln2_b):
    vec = lambda p: p.astype(F32).reshape(1, -1)
    x = x.astype(F32)
    for i in range(DEPTH):
        u, zr, za = _in_proj(x, w_in[i].astype(BF16))
        wb, a_re, a_im, wc = _ssm_weights(ssm_lambda_re[i], ssm_lambda_im[i], ssm_b_re[i], ssm_b_im[i],
                                          ssm_c_re[i], ssm_c_im[i], ssm_log_dt[i])
        u_tm = u.reshape(2, SSM_BATCH_HALF, SEQ, SSM_WIDTH).transpose(0, 2, 1, 3).reshape(
            2, SEQ * SSM_BATCH_HALF, SSM_WIDTH)
        y_ssm = _ssm(u_tm, wb, a_re, a_im, wc, vec(ssm_d[i]), ssm_w_glu[i].astype(BF16),
                     vec(ssm_b_glu[i]), vec(ssm_out_norm[i]))
        y_ssm = y_ssm.reshape(2, SEQ, SSM_BATCH_HALF, SSM_WIDTH).transpose(0, 2, 1, 3).reshape(
            BATCH, SEQ, SSM_WIDTH)
        y_ret = _retention(zr, vec(ret_out_norm[i]))
        y_att = _attention(za)
        x = _post(y_ssm, y_ret, y_att, x, w_out[i].astype(BF16), vec(attn_out_norm[i]),
                  vec(ln1_w[i]), vec(ln1_b[i]), mlp_w1[i].astype(BF16), mlp_w2[i].astype(BF16),
                  vec(ln2_w[i]), vec(ln2_b[i]))
    return x
```

```python
import functools
import math

import jax
import jax.numpy as jnp
from jax import lax
from jax.experimental import pallas as pl
from jax.experimental.pallas import tpu as pltpu

F32 = jnp.float32
BF16 = jnp.bfloat16

D_MODEL = 1024
BATCH = 16
SEQ = 2048
DEPTH = 2
SSM_WIDTH = 256
SSM_GROUP = 16
SSM_GROUPS = 16
SSM_STATE = 64
SSM_STATES = SSM_GROUPS * SSM_STATE
RET_HEAD_DIM = 64
RET_WIDTH = 256
RET_CHUNK = 128
ATT_HEAD_DIM = 64
ATT_WIDTH = 512
ATT_HEADS = 8
DILATED_BRANCHES = ((128, 1), (512, 4), (2048, 16))
ATT_BLOCK = 128
IN_WIDTH = SSM_WIDTH + 4 * RET_WIDTH + 3 * ATT_WIDTH
D_FF = 4 * D_MODEL
DEEPNORM_ALPHA = (2 * DEPTH) ** 0.25
LN_EPS = 1e-5

LANES = 128
HEAD_PAIR = LANES // ATT_HEAD_DIM
VMEM_LIMIT_BYTES = 56 * 1024 * 1024
MASKED_SCORE = -1e30

TOKEN_TILE = 512
POST_TILE = 1024
SSM_TIME_TILE = 64
SSM_BATCH_HALF = BATCH // 2
SSM_COL_CHUNK = 512
SSM_STAGE_PITCH = SSM_TIME_TILE + 8
FF_CHUNK = 1024
COMBINE_ROWS = 256
PREP_ROWS = 128
ATT_FIRST_UNROLL = 4
ATT_LATER_UNROLL = 3
RET_UNROLL = 8

assert RET_HEAD_DIM == ATT_HEAD_DIM and HEAD_PAIR == 2
assert all(w // d == ATT_BLOCK for w, d in DILATED_BRANCHES)
assert DILATED_BRANCHES[0][1] == 1 and all(
    b[1] % a[1] == 0 for a, b in zip(DILATED_BRANCHES, DILATED_BRANCHES[1:]))


def _params(*semantics):
    return pltpu.CompilerParams(dimension_semantics=semantics, vmem_limit_bytes=VMEM_LIMIT_BYTES)


def _sigmoid(x):
    return 1.0 / (1.0 + jnp.exp(-x))


def _layer_norm(r, w, b):
    mu = jnp.mean(r, axis=-1, keepdims=True)
    d = r - mu
    var = jnp.mean(d * d, axis=-1, keepdims=True)
    return d * lax.rsqrt(var + LN_EPS) * w + b


def _in_proj_kernel(x_ref, w_ref, u_ref, zr_ref, za_ref):
    xb = x_ref[...].astype(BF16)

    def proj(lo, hi):
        return jnp.dot(xb, w_ref[:, lo:hi], preferred_element_type=F32).astype(BF16)

    u_ref[...] = proj(0, SSM_WIDTH)
    zr_ref[...] = proj(SSM_WIDTH, SSM_WIDTH + 4 * RET_WIDTH)
    za_ref[...] = proj(SSM_WIDTH + 4 * RET_WIDTH, IN_WIDTH)


def _in_proj(x, w_in):
    tm = TOKEN_TILE
    row = lambda width: pl.BlockSpec((None, tm, width), lambda b, i: (b, i, 0))
    return pl.pallas_call(
        _in_proj_kernel,
        grid=(BATCH, SEQ // tm),
        in_specs=[row(D_MODEL), pl.BlockSpec((D_MODEL, IN_WIDTH), lambda b, i: (0, 0))],
        out_specs=[row(SSM_WIDTH), row(4 * RET_WIDTH), row(3 * ATT_WIDTH)],
        out_shape=[jax.ShapeDtypeStruct((BATCH, SEQ, SSM_WIDTH), BF16),
                   jax.ShapeDtypeStruct((BATCH, SEQ, 4 * RET_WIDTH), BF16),
                   jax.ShapeDtypeStruct((BATCH, SEQ, 3 * ATT_WIDTH), BF16)],
        compiler_params=_params("parallel", "parallel"),
        name="in_proj",
    )(x, w_in)


def _ssm_kernel(u_ref, wb_ref, are_ref, aim_ref, wc_ref, d_ref, wglu_ref, bglu_ref, nw_ref,
                y_ref, bu_ref, st_ref, ut_ref, stage_ref):
    n_st = SSM_STATES
    bh = SSM_BATCH_HALF
    lt = SSM_TIME_TILE
    pitch = SSM_STAGE_PITCH
    lane_slabs = SSM_WIDTH // LANES
    chunks = [(slice(c, c + SSM_COL_CHUNK), slice(n_st + c, n_st + c + SSM_COL_CHUNK))
              for c in range(0, n_st, SSM_COL_CHUNK)]

    @pl.when(pl.program_id(0) == 0)
    def _():
        st_ref[...] = jnp.zeros_like(st_ref)

    def stage_rows(h, t):
        return pl.ds(h * bh * pitch + t, bh, stride=pitch)

    def gather(h):
        for b in range(h * bh, (h + 1) * bh):
            for s in range(lane_slabs):
                stage_ref[s, b * pitch:b * pitch + lt, :] = (
                    u_ref[b, :, s * LANES:(s + 1) * LANES].astype(F32))
        for t in range(0, lt, 2):
            for s in range(lane_slabs):
                pair = jnp.concatenate([stage_ref.at[s][stage_rows(h, t + k), :] for k in range(2)], axis=0)
                ut_ref[h, t * bh:(t + 2) * bh, s * LANES:(s + 1) * LANES] = pair.astype(BF16)

    def scatter(h, y):
        for t in range(lt):
            for s in range(lane_slabs):
                stage_ref.at[s][stage_rows(h, t), :] = y[t * bh:(t + 1) * bh, s * LANES:(s + 1) * LANES]
        for b in range(h * bh, (h + 1) * bh):
            for s in range(lane_slabs):
                y_ref[b, :, s * LANES:(s + 1) * LANES] = (
                    stage_ref[s, b * pitch:b * pitch + lt, :].astype(BF16))

    def project(h, cols):
        for sl in cols:
            bu_ref[h, :, sl] = jnp.dot(ut_ref[h], wb_ref[:, sl], preferred_element_type=F32)

    def scan(h, cols):
        re, im = cols
        ar = jnp.broadcast_to(are_ref[:, re], (bh, SSM_COL_CHUNK))
        ai = jnp.broadcast_to(aim_ref[:, re], (bh, SSM_COL_CHUNK))
        xr, xi = st_ref[h, :, re], st_ref[h, :, im]
        for t in range(SSM_TIME_TILE):
            rows = slice(t * bh, (t + 1) * bh)
            xr, xi = (ar * xr - ai * xi + bu_ref[h, rows, re],
                      ar * xi + ai * xr + bu_ref[h, rows, im])
            bu_ref[h, rows, re] = xr
            bu_ref[h, rows, im] = xi
        st_ref[h, :, re] = xr
        st_ref[h, :, im] = xi

    def readout(h, cols, acc):
        for sl in cols:
            acc = acc + jnp.dot(bu_ref[h, :, sl].astype(BF16), wc_ref[sl, :], preferred_element_type=F32)
        return acc

    def finish(h, y):
        y = y + d_ref[...] * ut_ref[h].astype(F32)
        cdf = 0.5 * (1.0 + jnp.tanh(math.sqrt(2.0 / math.pi) * (y + 0.044715 * (y * y * y))))
        g = y * cdf
        gate = jnp.dot(g.astype(BF16), wglu_ref[...], preferred_element_type=F32) + bglu_ref[...]
        out = g * _sigmoid(gate)
        ms = jnp.mean(out * out, axis=-1, keepdims=True)
        scatter(h, out * lax.rsqrt(ms + LN_EPS) * nw_ref[...])

    zero = jnp.zeros((SSM_TIME_TILE * bh, SSM_WIDTH), F32)
    gather(0)
    for k, cols in enumerate(chunks):
        project(0, cols)
        if k == 0:
            gather(1)
    for cols in chunks:
        project(1, cols)
        scan(0, cols)
    y0 = zero
    for cols in chunks:
        y0 = readout(0, cols, y0)
        scan(1, cols)
    y1 = zero
    for k, cols in enumerate(chunks):
        y1 = readout(1, cols, y1)
        if k == 0:
            finish(0, y0)
    finish(1, y1)


def _ssm(u, wb, a_re, a_im, wc, d_skip, w_glu, b_glu, norm_w):
    rows = SSM_TIME_TILE * SSM_BATCH_HALF
    full = lambda shape: pl.BlockSpec(shape, lambda i: (0, 0))
    steps = pl.BlockSpec((BATCH, SSM_TIME_TILE, SSM_WIDTH), lambda i: (0, i, 0))
    return pl.pallas_call(
        _ssm_kernel,
        grid=(SEQ // SSM_TIME_TILE,),
        in_specs=[steps,
                  full((SSM_WIDTH, 2 * SSM_STATES)), full((1, SSM_STATES)), full((1, SSM_STATES)),
                  full((2 * SSM_STATES, SSM_WIDTH)), full((1, SSM_WIDTH)),
                  full((SSM_WIDTH, SSM_WIDTH)), full((1, SSM_WIDTH)), full((1, SSM_WIDTH))],
        out_specs=steps,
        out_shape=jax.ShapeDtypeStruct((BATCH, SEQ, SSM_WIDTH), BF16),
        scratch_shapes=[pltpu.VMEM((2, rows, 2 * SSM_STATES), F32),
                        pltpu.VMEM((2, SSM_BATCH_HALF, 2 * SSM_STATES), F32),
                        pltpu.VMEM((2, rows, SSM_WIDTH), BF16),
                        pltpu.VMEM((SSM_WIDTH // LANES, BATCH * SSM_STAGE_PITCH, LANES), F32)],
        compiler_params=_params("arbitrary"),
        name="s5_mixer",
    )(u, wb, a_re, a_im, wc, d_skip, w_glu, b_glu, norm_w)


def _ssm_weights(lam_re, lam_im, b_re, b_im, c_re, c_im, log_dt):
    g, p, h = SSM_GROUPS, SSM_STATE, SSM_GROUP
    lr, li = lam_re.astype(F32), lam_im.astype(F32)
    dt = jnp.exp(log_dt.astype(F32))[:, None]
    mag = jnp.exp(lr * dt)
    a_re, a_im = mag * jnp.cos(li * dt), mag * jnp.sin(li * dt)
    den = lr * lr + li * li
    nr, ni = a_re - 1.0, a_im
    f_re = ((nr * lr + ni * li) / den)[..., None]
    f_im = ((ni * lr - nr * li) / den)[..., None]
    br, bi = b_re.astype(F32), b_im.astype(F32)
    bb_re = f_re * br - f_im * bi
    bb_im = f_re * bi + f_im * br
    eye = jnp.eye(g, dtype=F32)
    embed_b = lambda bb: jnp.einsum('gph,gk->ghkp', bb, eye).reshape(g * h, g * p)
    embed_c = lambda cc: jnp.einsum('ghp,gk->gpkh', cc, eye).reshape(g * p, g * h)
    wb = jnp.concatenate([embed_b(bb_re), embed_b(bb_im)], axis=1).astype(BF16)
    wc = jnp.concatenate([embed_c(c_re.astype(F32)), -embed_c(c_im.astype(F32))], axis=0).astype(BF16)
    return wb, a_re.reshape(1, g * p), a_im.reshape(1, g * p), wc


def _ret_kernel(q_ref, k_ref, v_ref, g_ref, nw_ref, y_ref, dmat_ref):
    c = RET_CHUNK
    dh = RET_HEAD_DIM
    hp = pl.program_id(1)
    lane = lax.broadcasted_iota(jnp.int32, (c, LANES), 1)
    row = lax.broadcasted_iota(jnp.int32, (c, LANES), 0)
    head1 = lane >= dh
    head_of_lane = (HEAD_PAIR * hp).astype(F32) + head1.astype(F32)
    lg = jnp.log(1.0 - jnp.exp2(-5.0 - head_of_lane))
    rowf = row.astype(F32)
    xi = jnp.exp((rowf + 1.0) * lg)
    zeta = jnp.exp((c - 1.0 - rowf) * lg) * (dh ** -0.5)
    g_chunk = jnp.exp(c * lg)
    block_diag = (row >= dh) == head1
    diff = (row - lane).astype(F32)
    for j in range(HEAD_PAIR):
        lg_j = jnp.log(1.0 - jnp.exp2(jnp.zeros((c, c), F32) - 5.0 - (HEAD_PAIR * hp + j).astype(F32)))
        dmat_ref[j] = jnp.where(diff >= 0, jnp.exp(jnp.maximum(diff, 0.0) * lg_j), 0.0) * (dh ** -0.5)
    nw = nw_ref[...]

    def head_mean(t):
        s0 = jnp.sum(jnp.where(head1, 0.0, t), axis=-1, keepdims=True)
        s1 = jnp.sum(jnp.where(head1, t, 0.0), axis=-1, keepdims=True)
        return jnp.where(head1, s1, s0) * (1.0 / dh)

    def chunk(n, r_prev):
        rows = pl.ds(pl.multiple_of(n * c, c), c)
        q, k, v = q_ref[rows, :], k_ref[rows, :], v_ref[rows, :]
        o = jnp.zeros((c, LANES), F32)
        for j in range(HEAD_PAIR):
            mine = head1 if j else jnp.logical_not(head1)
            qj = jnp.where(mine, q, jnp.zeros_like(q))
            s = lax.dot_general(qj, k, (((1,), (1,)), ((), ())), preferred_element_type=F32)
            s = s * dmat_ref[j]
            vj = jnp.where(mine, v, jnp.zeros_like(v))
            o = o + jnp.dot(s.astype(BF16), vj, preferred_element_type=F32)
        qx = (q.astype(F32) * xi).astype(BF16)
        o = o + jnp.dot(qx, r_prev.astype(BF16), preferred_element_type=F32)
        kz = (k.astype(F32) * zeta).T.astype(BF16)
        kv = jnp.dot(kz, v, preferred_element_type=F32)
        dlt = o - head_mean(o)
        var = head_mean(dlt * dlt)
        gate = g_ref[rows, :].astype(F32)
        y = dlt * lax.rsqrt(var + LN_EPS) * nw * (gate * _sigmoid(gate))
        y_ref[rows, :] = y.astype(BF16)
        return jnp.where(block_diag, g_chunk * r_prev + kv, 0.0)

    def chunks(it, r):
        for u in range(RET_UNROLL):
            r = chunk(it * RET_UNROLL + u, r)
        return r

    lax.fori_loop(0, SEQ // (c * RET_UNROLL), chunks, jnp.zeros((LANES, LANES), F32))


def _retention(zr, norm_w):
    col = lambda off: pl.BlockSpec((None, SEQ, LANES), lambda b, hp: (b, 0, hp + off))
    pairs = RET_WIDTH // LANES
    return pl.pallas_call(
        _ret_kernel,
        grid=(BATCH, pairs),
        in_specs=[col(0), col(pairs), col(2 * pairs), col(3 * pairs),
                  pl.BlockSpec((1, LANES), lambda b, hp: (0, hp))],
        out_specs=col(0),
        out_shape=jax.ShapeDtypeStruct((BATCH, SEQ, RET_WIDTH), BF16),
        scratch_shapes=[pltpu.VMEM((HEAD_PAIR, RET_CHUNK, RET_CHUNK), F32)],
        compiler_params=_params("parallel", "parallel"),
        name="retention",
    )(zr, zr, zr, zr, norm_w)


def _attn_kernel(q_ref, k_ref, v_ref, y_ref, src_ref, dq_ref, dk_ref, dv_ref,
                 oacc_ref, lacc_ref, bias_ref, bias0_ref, p_ref, s_ref):
    qb = ATT_BLOCK
    dh = ATT_HEAD_DIM
    hp = pl.program_id(1)
    head1 = lax.broadcasted_iota(jnp.int32, (qb, LANES), 1) >= dh

    def emit_q(bi, rows, blk):
        other = lax.broadcasted_iota(jnp.int32, blk.shape, 1) >= dh
        blk = blk * (dh ** -0.5)
        dq_ref[bi, 0, rows, :] = jnp.where(other, 0.0, blk).astype(BF16)
        dq_ref[bi, 1, rows, :] = jnp.where(other, blk, 0.0).astype(BF16)

    def emit_k(bi, rows, blk):
        if bi > 0:
            dk_ref[bi - 1, rows, :] = blk.astype(BF16)

    def emit_v(bi, rows, blk):
        if bi > 0:
            dv_ref[bi - 1, rows, :] = blk.astype(BF16)

    piece = PREP_ROWS
    for src, emit in ((q_ref, emit_q), (k_ref, emit_k), (v_ref, emit_v)):
        for c0 in range(0, SEQ, piece):
            rows = slice(c0, c0 + piece)
            natural = src[rows, :].astype(F32)
            src_ref[0, rows, :] = natural
            emit(0, rows, natural)
        d_prev = 1
        for bi in range(1, len(DILATED_BRANCHES)):
            d = DILATED_BRANCHES[bi][1]
            step = d // d_prev
            sub_prev, sub = SEQ // d_prev, SEQ // d
            cur, nxt = (bi - 1) % 2, bi % 2
            for r_prev in range(d_prev):
                for t in range(step):
                    r = r_prev + d_prev * t
                    for c0 in range(0, sub, piece):
                        n = min(piece, sub - c0)
                        rows = slice(r * sub + c0, r * sub + c0 + n)
                        blk = src_ref.at[cur][pl.ds(r_prev * sub_prev + t + step * c0, n, stride=step), :]
                        emit(bi, rows, blk)
                        if bi + 1 < len(DILATED_BRANCHES):
                            src_ref[nxt, rows, :] = blk
            d_prev = d

    row2 = lax.broadcasted_iota(jnp.int32, (HEAD_PAIR * qb, 2 * qb), 0)
    key2 = lax.broadcasted_iota(jnp.int32, (HEAD_PAIR * qb, 2 * qb), 1)
    row1 = lax.broadcasted_iota(jnp.int32, (HEAD_PAIR * qb, qb), 0)
    key1 = lax.broadcasted_iota(jnp.int32, (HEAD_PAIR * qb, qb), 1)

    def head_slope(row):
        head = (HEAD_PAIR * hp).astype(F32) + (row >= qb).astype(F32)
        return jnp.exp2(-(8.0 / ATT_HEADS) * (head + 1.0))

    dist2 = qb + jnp.where(row2 >= qb, row2 - qb, row2) - key2
    dist1 = jnp.where(row1 >= qb, row1 - qb, row1) - key1
    slope2, slope1 = head_slope(row2), head_slope(row1)

    def aligned(x):
        return x if isinstance(x, int) else pl.multiple_of(x, qb)

    for bi, (window, d) in enumerate(DILATED_BRANCHES):
        sub = SEQ // d
        nb = sub // qb
        ksrc = k_ref if bi == 0 else dk_ref.at[bi - 1]
        vsrc = v_ref if bi == 0 else dv_ref.at[bi - 1]
        bias_ref[...] = jnp.where((dist2 >= 0) & (dist2 <= window // d),
                                  -slope2 * (d * dist2).astype(F32), MASKED_SCORE)
        bias0_ref[...] = jnp.where(dist1 >= 0, -slope1 * (d * dist1).astype(F32), MASKED_SCORE)

        def rows_of(blk, d=d):
            base, first_token, has_prev, _ = blk
            rows = pl.ds(aligned(base), qb)
            krows = pl.ds(aligned(base - qb), 2 * qb) if has_prev else rows
            out_rows = rows if d == 1 else pl.ds(first_token, qb, stride=d)
            return rows, krows, out_rows, (2 * qb if has_prev else qb)

        def scores(blk, bi=bi, ksrc=ksrc, rows_of=rows_of):
            rows, krows, out_rows, nk = rows_of(blk)
            q2 = jnp.concatenate([dq_ref[bi, j, rows, :] for j in range(HEAD_PAIR)], axis=0)
            s_ref[blk[3], :, 0:nk] = lax.dot_general(
                q2, ksrc[krows, :], (((1,), (1,)), ((), ())), preferred_element_type=F32)

        def softmax(blk, bi=bi, rows_of=rows_of):
            rows, krows, out_rows, nk = rows_of(blk)
            s = s_ref[blk[3], :, 0:nk] + (bias_ref[...] if blk[2] else bias0_ref[...])
            m = jnp.max(s, axis=-1, keepdims=True)
            e = jnp.exp(s - m)
            l = jnp.sum(e, axis=-1, keepdims=True)
            p_ref[blk[3], :, 0:nk] = (e * (1.0 / l)).astype(BF16)
            lse = m + jnp.log(l)
            lacc_ref.at[bi][out_rows, :] = jnp.where(
                head1, jnp.broadcast_to(lse[qb:], (qb, LANES)), jnp.broadcast_to(lse[:qb], (qb, LANES)))

        def values(blk, bi=bi, vsrc=vsrc, rows_of=rows_of):
            rows, krows, out_rows, nk = rows_of(blk)
            v = vsrc[krows, :]
            o = [jnp.dot(p_ref[blk[3], j * qb:(j + 1) * qb, 0:nk], v, preferred_element_type=F32)
                 for j in range(HEAD_PAIR)]
            oacc_ref.at[bi][out_rows, :] = jnp.where(head1, o[1], o[0])

        def pipelined(n_groups, group, stages=(scores, softmax, values)):
            def step(t, valid):
                for k in reversed(range(len(stages))):
                    if valid(t - k):
                        for blk in group(t - k):
                            stages[k](blk)

            depth = len(stages) - 1
            head_steps = min(depth, n_groups)
            for t in range(head_steps):
                step(t, lambda g: 0 <= g < n_groups)
            if n_groups > depth:
                def body(t, carry):
                    step(t, lambda g: True)
                    return carry

                lax.fori_loop(depth, n_groups, body, 0)
            for t in range(max(n_groups, head_steps), n_groups + depth):
                step(t, lambda g: 0 <= g < n_groups)

        fu = min(d, ATT_FIRST_UNROLL)
        pipelined(d // fu, lambda it, fu=fu, sub=sub: [
            ((it * fu + u) * sub, it * fu + u, False, it * fu + u) for u in range(fu)])

        if nb > 1:
            lu = ATT_LATER_UNROLL
            per_res = (nb - 1) // lu
            assert per_res * lu == nb - 1

            def later_group(it, lu=lu, per_res=per_res, sub=sub, d=d):
                if per_res == 1:
                    r, n0 = it, 1
                elif d == 1:
                    r, n0 = 0, it * lu + 1
                elif isinstance(it, int):
                    r, n0 = it // per_res, (it % per_res) * lu + 1
                else:
                    r, n0 = lax.div(it, per_res), lax.rem(it, per_res) * lu + 1
                return [(r * sub + (n0 + u) * qb, r + d * qb * (n0 + u), True, d + it * lu + u)
                        for u in range(lu)]

            pipelined(d * per_res, later_group)

    def combine(tb, carry):
        rows = pl.ds(pl.multiple_of(tb * COMBINE_ROWS, COMBINE_ROWS), COMBINE_ROWS)
        lses = [lacc_ref[bi, rows, :] for bi in range(len(DILATED_BRANCHES))]
        top = functools.reduce(jnp.maximum, lses)
        ws = [jnp.exp(lse - top) for lse in lses]
        num = sum(w * oacc_ref[bi, rows, :] for bi, w in enumerate(ws))
        y_ref[rows, :] = (num / sum(ws)).astype(BF16)
        return carry

    lax.fori_loop(0, SEQ // COMBINE_ROWS, combine, 0)


def _attention(za):
    pairs = ATT_WIDTH // LANES
    col = lambda off: pl.BlockSpec((None, SEQ, LANES), lambda b, hp: (b, 0, hp + off))
    n_dil = len(DILATED_BRANCHES) - 1
    return pl.pallas_call(
        _attn_kernel,
        grid=(BATCH, pairs),
        in_specs=[col(0), col(pairs), col(2 * pairs)],
        out_specs=col(0),
        out_shape=jax.ShapeDtypeStruct((BATCH, SEQ, ATT_WIDTH), BF16),
        scratch_shapes=[pltpu.VMEM((2, SEQ, LANES), F32),
                        pltpu.VMEM((n_dil + 1, HEAD_PAIR, SEQ, LANES), BF16),
                        pltpu.VMEM((n_dil, SEQ, LANES), BF16),
                        pltpu.VMEM((n_dil, SEQ, LANES), BF16),
                        pltpu.VMEM((n_dil + 1, SEQ, LANES), F32),
                        pltpu.VMEM((n_dil + 1, SEQ, LANES), F32),
                        pltpu.VMEM((HEAD_PAIR * ATT_BLOCK, 2 * ATT_BLOCK), F32),
                        pltpu.VMEM((HEAD_PAIR * ATT_BLOCK, ATT_BLOCK), F32),
                        pltpu.VMEM((SEQ // ATT_BLOCK, HEAD_PAIR * ATT_BLOCK, 2 * ATT_BLOCK), BF16),
                        pltpu.VMEM((SEQ // ATT_BLOCK, HEAD_PAIR * ATT_BLOCK, 2 * ATT_BLOCK), F32)],
        compiler_params=_params("parallel", "parallel"),
        name="dilated_attention",
    )(za, za, za)


def _post_kernel(ys_ref, yr_ref, ya_ref, x_ref, wo_ref, an_ref, l1w_ref, l1b_ref,
                 w1_ref, w2_ref, l2w_ref, l2b_ref, o_ref, x1_ref, acc_ref):
    half = POST_TILE // 2
    n_chunks = D_FF // FF_CHUNK
    slab = half // n_chunks
    halves = (slice(0, half), slice(half, POST_TILE))

    def out_proj(rows):
        ya = ya_ref[rows, :].astype(F32)
        ms = jnp.mean(ya * ya, axis=-1, keepdims=True)
        ya = (ya * lax.rsqrt(ms + LN_EPS) * an_ref[...]).astype(BF16)
        h = jnp.dot(ys_ref[rows, :], wo_ref[0:SSM_WIDTH, :], preferred_element_type=F32)
        h = h + jnp.dot(yr_ref[rows, :], wo_ref[SSM_WIDTH:SSM_WIDTH + RET_WIDTH, :],
                        preferred_element_type=F32)
        h = h + jnp.dot(ya, wo_ref[SSM_WIDTH + RET_WIDTH:, :], preferred_element_type=F32)
        o_ref[rows, :] = DEEPNORM_ALPHA * x_ref[rows, :] + h

    def ln1(rows):
        x1_ref[rows, :] = _layer_norm(o_ref[rows, :], l1w_ref[...], l1b_ref[...])

    def ln2(rows):
        o_ref[rows, :] = _layer_norm(DEEPNORM_ALPHA * x1_ref[rows, :] + acc_ref[rows, :],
                                     l2w_ref[...], l2b_ref[...])

    def mlp(rows, between):
        xb = x1_ref[rows, :].astype(BF16)
        acc = jnp.zeros((half, D_MODEL), F32)
        for k in range(n_chunks):
            c = k * FF_CHUNK
            hid = jnp.dot(xb, w1_ref[:, c:c + FF_CHUNK], preferred_element_type=F32)
            hid = jnp.square(jnp.maximum(hid, 0.0)).astype(BF16)
            acc = acc + jnp.dot(hid, w2_ref[c:c + FF_CHUNK, :], preferred_element_type=F32)
            between(k)
        acc_ref[rows, :] = acc

    slab_of = lambda rows, k: slice(rows.start + k * slab, rows.start + (k + 1) * slab)
    out_proj(halves[0])
    out_proj(halves[1])
    ln1(halves[0])
    mlp(halves[0], lambda k: ln1(slab_of(halves[1], k)))
    mlp(halves[1], lambda k: ln2(slab_of(halves[0], k)))
    ln2(halves[1])


def _post(y_ssm, y_ret, y_att, x, w_out, attn_norm, ln1_w, ln1_b, w1, w2, ln2_w, ln2_b):
    tm = POST_TILE
    row = lambda width: pl.BlockSpec((None, tm, width), lambda b, i: (b, i, 0))
    full = lambda shape: pl.BlockSpec(shape, lambda b, i: (0, 0), pipeline_mode=pl.Buffered(1))
    return pl.pallas_call(
        _post_kernel,
        grid=(BATCH, SEQ // tm),
        in_specs=[row(SSM_WIDTH), row(RET_WIDTH), row(ATT_WIDTH), row(D_MODEL),
                  full((D_MODEL, D_MODEL)), full((1, ATT_WIDTH)), full((1, D_MODEL)), full((1, D_MODEL)),
                  full((D_MODEL, D_FF)), full((D_FF, D_MODEL)), full((1, D_MODEL)), full((1, D_MODEL))],
        out_specs=row(D_MODEL),
        out_shape=jax.ShapeDtypeStruct((BATCH, SEQ, D_MODEL), F32),
        scratch_shapes=[pltpu.VMEM((tm, D_MODEL), F32), pltpu.VMEM((tm, D_MODEL), F32)],
        compiler_params=_params("parallel", "parallel"),
        name="out_proj_mlp",
    )(y_ssm, y_ret, y_att, x, w_out, attn_norm, ln1_w, ln1_b, w1, w2, ln2_w, ln2_b)


def kernel(x, w_in, ssm_lambda_re, ssm_lambda_im, ssm_b_re, ssm_b_im, ssm_c_re, ssm_c_im, ssm_d, ssm_log_dt, ssm_w_glu, ssm_b_glu, ssm_out_norm, ret_out_norm, attn_out_norm, w_out, ln1_w, ln1_b, mlp_w1, mlp_w2, ln2_w, ln2_b):
    vec = lambda p: p.astype(F32).reshape(1, -1)
    x = x.astype(F32)
    for i in range(DEPTH):
        u, zr, za = _in_proj(x, w_in[i].astype(BF16))
        wb, a_re, a_im, wc = _ssm_weights(ssm_lambda_re[i], ssm_lambda_im[i], ssm_b_re[i], ssm_b_im[i],
                                          ssm_c_re[i], ssm_c_im[i], ssm_log_dt[i])
        y_ssm = _ssm(u, wb, a_re, a_im, wc, vec(ssm_d[i]), ssm_w_glu[i].astype(BF16),
                     vec(ssm_b_glu[i]), vec(ssm_out_norm[i]))
        y_ret = _retention(zr, vec(ret_out_norm[i]))
        y_att = _attention(za)
        x = _post(y_ssm, y_ret, y_att, x, w_out[i].astype(BF16), vec(attn_out_norm[i]),
                  vec(ln1_w[i]), vec(ln1_b[i]), mlp_w1[i].astype(BF16), mlp_w2[i].astype(BF16),
                  vec(ln2_w[i]), vec(ln2_b[i]))
    return x
```

```python
import functools
import math

import jax
import jax.numpy as jnp
from jax import lax
from jax.experimental import pallas as pl
from jax.experimental.pallas import tpu as pltpu

F32 = jnp.float32
BF16 = jnp.bfloat16

D_MODEL = 1024
BATCH = 16
SEQ = 2048
DEPTH = 2
SSM_WIDTH = 256
SSM_GROUP = 16
SSM_GROUPS = 16
SSM_STATE = 64
SSM_STATES = SSM_GROUPS * SSM_STATE
RET_HEAD_DIM = 64
RET_WIDTH = 256
RET_CHUNK = 128
ATT_HEAD_DIM = 64
ATT_WIDTH = 512
ATT_HEADS = 8
DILATED_BRANCHES = ((128, 1), (512, 4), (2048, 16))
ATT_BLOCK = 128
IN_WIDTH = SSM_WIDTH + 4 * RET_WIDTH + 3 * ATT_WIDTH
D_FF = 4 * D_MODEL
DEEPNORM_ALPHA = (2 * DEPTH) ** 0.25
LN_EPS = 1e-5

LANES = 128
HEAD_PAIR = LANES // ATT_HEAD_DIM
VMEM_LIMIT_BYTES = 56 * 1024 * 1024
MASKED_SCORE = -1e30

TOKEN_TILE = 512
POST_TILE = 1024
SSM_TIME_TILE = 64
SSM_BATCH_HALF = BATCH // 2
SSM_COL_CHUNK = 512
SSM_STAGE_PITCH = SSM_TIME_TILE + 8
FF_CHUNK = 1024
COMBINE_ROWS = 256
PREP_ROWS = 128
ATT_FIRST_UNROLL = 4
ATT_LATER_UNROLL = 3
RET_UNROLL = 16

assert RET_HEAD_DIM == ATT_HEAD_DIM and HEAD_PAIR == 2
assert all(w // d == ATT_BLOCK for w, d in DILATED_BRANCHES)
assert DILATED_BRANCHES[0][1] == 1 and all(
    b[1] % a[1] == 0 for a, b in zip(DILATED_BRANCHES, DILATED_BRANCHES[1:]))


def _params(*semantics):
    return pltpu.CompilerParams(dimension_semantics=semantics, vmem_limit_bytes=VMEM_LIMIT_BYTES)


def _sigmoid(x):
    return 1.0 / (1.0 + jnp.exp(-x))


def _layer_norm(r, w, b):
    mu = jnp.mean(r, axis=-1, keepdims=True)
    d = r - mu
    var = jnp.mean(d * d, axis=-1, keepdims=True)
    return d * lax.rsqrt(var + LN_EPS) * w + b


def _in_proj_kernel(x_ref, w_ref, u_ref, zr_ref, za_ref):
    xb = x_ref[...].astype(BF16)

    def proj(lo, hi):
        return jnp.dot(xb, w_ref[:, lo:hi], preferred_element_type=F32).astype(BF16)

    u_ref[...] = proj(0, SSM_WIDTH)
    zr_ref[...] = proj(SSM_WIDTH, SSM_WIDTH + 4 * RET_WIDTH)
    za_ref[...] = proj(SSM_WIDTH + 4 * RET_WIDTH, IN_WIDTH)


def _in_proj(x, w_in):
    tm = TOKEN_TILE
    row = lambda width: pl.BlockSpec((None, tm, width), lambda b, i: (b, i, 0))
    return pl.pallas_call(
        _in_proj_kernel,
        grid=(BATCH, SEQ // tm),
        in_specs=[row(D_MODEL), pl.BlockSpec((D_MODEL, IN_WIDTH), lambda b, i: (0, 0))],
        out_specs=[row(SSM_WIDTH), row(4 * RET_WIDTH), row(3 * ATT_WIDTH)],
        out_shape=[jax.ShapeDtypeStruct((BATCH, SEQ, SSM_WIDTH), BF16),
                   jax.ShapeDtypeStruct((BATCH, SEQ, 4 * RET_WIDTH), BF16),
                   jax.ShapeDtypeStruct((BATCH, SEQ, 3 * ATT_WIDTH), BF16)],
        compiler_params=_params("parallel", "parallel"),
        name="in_proj",
    )(x, w_in)


def _ssm_kernel(u_ref, wb_ref, are_ref, aim_ref, wc_ref, d_ref, wglu_ref, bglu_ref, nw_ref,
                y_ref, bu_ref, st_ref, ut_ref, stage_ref):
    n_st = SSM_STATES
    bh = SSM_BATCH_HALF
    lt = SSM_TIME_TILE
    pitch = SSM_STAGE_PITCH
    lane_slabs = SSM_WIDTH // LANES
    chunks = [(slice(c, c + SSM_COL_CHUNK), slice(n_st + c, n_st + c + SSM_COL_CHUNK))
              for c in range(0, n_st, SSM_COL_CHUNK)]

    @pl.when(pl.program_id(0) == 0)
    def _():
        st_ref[...] = jnp.zeros_like(st_ref)

    def stage_rows(h, t):
        return pl.ds(h * bh * pitch + t, bh, stride=pitch)

    def gather(h):
        for b in range(h * bh, (h + 1) * bh):
            for s in range(lane_slabs):
                stage_ref[s, b * pitch:b * pitch + lt, :] = (
                    u_ref[b, :, s * LANES:(s + 1) * LANES].astype(F32))
        for t in range(0, lt, 2):
            for s in range(lane_slabs):
                pair = jnp.concatenate([stage_ref.at[s][stage_rows(h, t + k), :] for k in range(2)], axis=0)
                ut_ref[h, t * bh:(t + 2) * bh, s * LANES:(s + 1) * LANES] = pair.astype(BF16)

    def scatter(h, y):
        for t in range(lt):
            for s in range(lane_slabs):
                stage_ref.at[s][stage_rows(h, t), :] = y[t * bh:(t + 1) * bh, s * LANES:(s + 1) * LANES]
        for b in range(h * bh, (h + 1) * bh):
            for s in range(lane_slabs):
                y_ref[b, :, s * LANES:(s + 1) * LANES] = (
                    stage_ref[s, b * pitch:b * pitch + lt, :].astype(BF16))

    def project(h, cols):
        for sl in cols:
            bu_ref[h, :, sl] = jnp.dot(ut_ref[h], wb_ref[:, sl], preferred_element_type=F32)

    def scan(h, cols):
        re, im = cols
        ar = jnp.broadcast_to(are_ref[:, re], (bh, SSM_COL_CHUNK))
        ai = jnp.broadcast_to(aim_ref[:, re], (bh, SSM_COL_CHUNK))
        xr, xi = st_ref[h, :, re], st_ref[h, :, im]
        for t in range(SSM_TIME_TILE):
            rows = slice(t * bh, (t + 1) * bh)
            xr, xi = (ar * xr - ai * xi + bu_ref[h, rows, re],
                      ar * xi + ai * xr + bu_ref[h, rows, im])
            bu_ref[h, rows, re] = xr
            bu_ref[h, rows, im] = xi
        st_ref[h, :, re] = xr
        st_ref[h, :, im] = xi

    def readout(h, cols, acc):
        for sl in cols:
            acc = acc + jnp.dot(bu_ref[h, :, sl].astype(BF16), wc_ref[sl, :], preferred_element_type=F32)
        return acc

    def finish(h, y):
        y = y + d_ref[...] * ut_ref[h].astype(F32)
        cdf = 0.5 * (1.0 + jnp.tanh(math.sqrt(2.0 / math.pi) * (y + 0.044715 * (y * y * y))))
        g = y * cdf
        gate = jnp.dot(g.astype(BF16), wglu_ref[...], preferred_element_type=F32) + bglu_ref[...]
        out = g * _sigmoid(gate)
        ms = jnp.mean(out * out, axis=-1, keepdims=True)
        scatter(h, out * lax.rsqrt(ms + LN_EPS) * nw_ref[...])

    zero = jnp.zeros((SSM_TIME_TILE * bh, SSM_WIDTH), F32)
    gather(0)
    for k, cols in enumerate(chunks):
        project(0, cols)
        if k == 0:
            gather(1)
    for cols in chunks:
        project(1, cols)
        scan(0, cols)
    y0 = zero
    for cols in chunks:
        y0 = readout(0, cols, y0)
        scan(1, cols)
    y1 = zero
    for k, cols in enumerate(chunks):
        y1 = readout(1, cols, y1)
        if k == 0:
            finish(0, y0)
    finish(1, y1)


def _ssm(u, wb, a_re, a_im, wc, d_skip, w_glu, b_glu, norm_w):
    rows = SSM_TIME_TILE * SSM_BATCH_HALF
    full = lambda shape: pl.BlockSpec(shape, lambda i: (0, 0))
    steps = pl.BlockSpec((BATCH, SSM_TIME_TILE, SSM_WIDTH), lambda i: (0, i, 0))
    return pl.pallas_call(
        _ssm_kernel,
        grid=(SEQ // SSM_TIME_TILE,),
        in_specs=[steps,
                  full((SSM_WIDTH, 2 * SSM_STATES)), full((1, SSM_STATES)), full((1, SSM_STATES)),
                  full((2 * SSM_STATES, SSM_WIDTH)), full((1, SSM_WIDTH)),
                  full((SSM_WIDTH, SSM_WIDTH)), full((1, SSM_WIDTH)), full((1, SSM_WIDTH))],
        out_specs=steps,
        out_shape=jax.ShapeDtypeStruct((BATCH, SEQ, SSM_WIDTH), BF16),
        scratch_shapes=[pltpu.VMEM((2, rows, 2 * SSM_STATES), F32),
                        pltpu.VMEM((2, SSM_BATCH_HALF, 2 * SSM_STATES), F32),
                        pltpu.VMEM((2, rows, SSM_WIDTH), BF16),
                        pltpu.VMEM((SSM_WIDTH // LANES, BATCH * SSM_STAGE_PITCH, LANES), F32)],
        compiler_params=_params("arbitrary"),
        name="s5_mixer",
    )(u, wb, a_re, a_im, wc, d_skip, w_glu, b_glu, norm_w)


def _ssm_weights(lam_re, lam_im, b_re, b_im, c_re, c_im, log_dt):
    g, p, h = SSM_GROUPS, SSM_STATE, SSM_GROUP
    lr, li = lam_re.astype(F32), lam_im.astype(F32)
    dt = jnp.exp(log_dt.astype(F32))[:, None]
    mag = jnp.exp(lr * dt)
    a_re, a_im = mag * jnp.cos(li * dt), mag * jnp.sin(li * dt)
    den = lr * lr + li * li
    nr, ni = a_re - 1.0, a_im
    f_re = ((nr * lr + ni * li) / den)[..., None]
    f_im = ((ni * lr - nr * li) / den)[..., None]
    br, bi = b_re.astype(F32), b_im.astype(F32)
    bb_re = f_re * br - f_im * bi
    bb_im = f_re * bi + f_im * br
    eye = jnp.eye(g, dtype=F32)
    embed_b = lambda bb: jnp.einsum('gph,gk->ghkp', bb, eye).reshape(g * h, g * p)
    embed_c = lambda cc: jnp.einsum('ghp,gk->gpkh', cc, eye).reshape(g * p, g * h)
    wb = jnp.concatenate([embed_b(bb_re), embed_b(bb_im)], axis=1).astype(BF16)
    wc = jnp.concatenate([embed_c(c_re.astype(F32)), -embed_c(c_im.astype(F32))], axis=0).astype(BF16)
    return wb, a_re.reshape(1, g * p), a_im.reshape(1, g * p), wc


def _ret_kernel(q_ref, k_ref, v_ref, g_ref, nw_ref, y_ref, dmat_ref):
    c = RET_CHUNK
    dh = RET_HEAD_DIM
    hp = pl.program_id(1)
    lane = lax.broadcasted_iota(jnp.int32, (c, LANES), 1)
    row = lax.broadcasted_iota(jnp.int32, (c, LANES), 0)
    head1 = lane >= dh
    head_of_lane = (HEAD_PAIR * hp).astype(F32) + head1.astype(F32)
    lg = jnp.log(1.0 - jnp.exp2(-5.0 - head_of_lane))
    rowf = row.astype(F32)
    xi = jnp.exp((rowf + 1.0) * lg)
    zeta = jnp.exp((c - 1.0 - rowf) * lg) * (dh ** -0.5)
    g_chunk = jnp.exp(c * lg)
    block_diag = (row >= dh) == head1
    diff = (row - lane).astype(F32)
    for j in range(HEAD_PAIR):
        lg_j = jnp.log(1.0 - jnp.exp2(jnp.zeros((c, c), F32) - 5.0 - (HEAD_PAIR * hp + j).astype(F32)))
        dmat_ref[j] = jnp.where(diff >= 0, jnp.exp(jnp.maximum(diff, 0.0) * lg_j), 0.0) * (dh ** -0.5)
    nw = nw_ref[...]

    def head_mean(t):
        s0 = jnp.sum(jnp.where(head1, 0.0, t), axis=-1, keepdims=True)
        s1 = jnp.sum(jnp.where(head1, t, 0.0), axis=-1, keepdims=True)
        return jnp.where(head1, s1, s0) * (1.0 / dh)

    def chunk(n, r_prev):
        rows = pl.ds(pl.multiple_of(n * c, c), c)
        q, k, v = q_ref[rows, :], k_ref[rows, :], v_ref[rows, :]
        o = jnp.zeros((c, LANES), F32)
        for j in range(HEAD_PAIR):
            mine = head1 if j else jnp.logical_not(head1)
            qj = jnp.where(mine, q, jnp.zeros_like(q))
            s = lax.dot_general(qj, k, (((1,), (1,)), ((), ())), preferred_element_type=F32)
            s = s * dmat_ref[j]
            vj = jnp.where(mine, v, jnp.zeros_like(v))
            o = o + jnp.dot(s.astype(BF16), vj, preferred_element_type=F32)
        qx = (q.astype(F32) * xi).astype(BF16)
        o = o + jnp.dot(qx, r_prev.astype(BF16), preferred_element_type=F32)
        kz = (k.astype(F32) * zeta).T.astype(BF16)
        kv = jnp.dot(kz, v, preferred_element_type=F32)
        dlt = o - head_mean(o)
        var = head_mean(dlt * dlt)
        gate = g_ref[rows, :].astype(F32)
        y = dlt * lax.rsqrt(var + LN_EPS) * nw * (gate * _sigmoid(gate))
        y_ref[rows, :] = y.astype(BF16)
        return jnp.where(block_diag, g_chunk * r_prev + kv, 0.0)

    def chunks(it, r):
        for u in range(RET_UNROLL):
            r = chunk(it * RET_UNROLL + u, r)
        return r

    lax.fori_loop(0, SEQ // (c * RET_UNROLL), chunks, jnp.zeros((LANES, LANES), F32))


def _retention(zr, norm_w):
    col = lambda off: pl.BlockSpec((None, SEQ, LANES), lambda b, hp: (b, 0, hp + off))
    pairs = RET_WIDTH // LANES
    return pl.pallas_call(
        _ret_kernel,
        grid=(BATCH, pairs),
        in_specs=[col(0), col(pairs), col(2 * pairs), col(3 * pairs),
                  pl.BlockSpec((1, LANES), lambda b, hp: (0, hp))],
        out_specs=col(0),
        out_shape=jax.ShapeDtypeStruct((BATCH, SEQ, RET_WIDTH), BF16),
        scratch_shapes=[pltpu.VMEM((HEAD_PAIR, RET_CHUNK, RET_CHUNK), F32)],
        compiler_params=_params("parallel", "parallel"),
        name="retention",
    )(zr, zr, zr, zr, norm_w)


def _attn_kernel(q_ref, k_ref, v_ref, y_ref, src_ref, dq_ref, dk_ref, dv_ref,
                 oacc_ref, lacc_ref, bias_ref, bias0_ref, p_ref, s_ref):
    qb = ATT_BLOCK
    dh = ATT_HEAD_DIM
    hp = pl.program_id(0)
    head1 = lax.broadcasted_iota(jnp.int32, (qb, LANES), 1) >= dh

    def emit_q(bi, rows, blk):
        other = lax.broadcasted_iota(jnp.int32, blk.shape, 1) >= dh
        blk = blk * (dh ** -0.5)
        dq_ref[bi, 0, rows, :] = jnp.where(other, 0.0, blk).astype(BF16)
        dq_ref[bi, 1, rows, :] = jnp.where(other, blk, 0.0).astype(BF16)

    def emit_k(bi, rows, blk):
        if bi > 0:
            dk_ref[bi - 1, rows, :] = blk.astype(BF16)

    def emit_v(bi, rows, blk):
        if bi > 0:
            dv_ref[bi - 1, rows, :] = blk.astype(BF16)

    piece = PREP_ROWS
    for src, emit in ((q_ref, emit_q), (k_ref, emit_k), (v_ref, emit_v)):
        for c0 in range(0, SEQ, piece):
            rows = slice(c0, c0 + piece)
            natural = src[rows, :].astype(F32)
            src_ref[0, rows, :] = natural
            emit(0, rows, natural)
        d_prev = 1
        for bi in range(1, len(DILATED_BRANCHES)):
            d = DILATED_BRANCHES[bi][1]
            step = d // d_prev
            sub_prev, sub = SEQ // d_prev, SEQ // d
            cur, nxt = (bi - 1) % 2, bi % 2
            for r_prev in range(d_prev):
                for t in range(step):
                    r = r_prev + d_prev * t
                    for c0 in range(0, sub, piece):
                        n = min(piece, sub - c0)
                        rows = slice(r * sub + c0, r * sub + c0 + n)
                        blk = src_ref.at[cur][pl.ds(r_prev * sub_prev + t + step * c0, n, stride=step), :]
                        emit(bi, rows, blk)
                        if bi + 1 < len(DILATED_BRANCHES):
                            src_ref[nxt, rows, :] = blk
            d_prev = d

    row2 = lax.broadcasted_iota(jnp.int32, (HEAD_PAIR * qb, 2 * qb), 0)
    key2 = lax.broadcasted_iota(jnp.int32, (HEAD_PAIR * qb, 2 * qb), 1)
    row1 = lax.broadcasted_iota(jnp.int32, (HEAD_PAIR * qb, qb), 0)
    key1 = lax.broadcasted_iota(jnp.int32, (HEAD_PAIR * qb, qb), 1)

    def head_slope(row):
        head = (HEAD_PAIR * hp).astype(F32) + (row >= qb).astype(F32)
        return jnp.exp2(-(8.0 / ATT_HEADS) * (head + 1.0))

    @pl.when(pl.program_id(1) == 0)
    def _():
        dist2 = qb + jnp.where(row2 >= qb, row2 - qb, row2) - key2
        dist1 = jnp.where(row1 >= qb, row1 - qb, row1) - key1
        slope2, slope1 = head_slope(row2), head_slope(row1)
        for bi, (window, d) in enumerate(DILATED_BRANCHES):
            bias_ref[bi] = jnp.where((dist2 >= 0) & (dist2 <= window // d),
                                     -slope2 * (d * dist2).astype(F32), MASKED_SCORE)
            bias0_ref[bi] = jnp.where(dist1 >= 0, -slope1 * (d * dist1).astype(F32), MASKED_SCORE)

    def aligned(x):
        return x if isinstance(x, int) else pl.multiple_of(x, qb)

    for bi, (window, d) in enumerate(DILATED_BRANCHES):
        sub = SEQ // d
        nb = sub // qb
        ksrc = k_ref if bi == 0 else dk_ref.at[bi - 1]
        vsrc = v_ref if bi == 0 else dv_ref.at[bi - 1]

        def rows_of(blk, d=d):
            base, first_token, has_prev, _ = blk
            rows = pl.ds(aligned(base), qb)
            krows = pl.ds(aligned(base - qb), 2 * qb) if has_prev else rows
            out_rows = rows if d == 1 else pl.ds(first_token, qb, stride=d)
            return rows, krows, out_rows, (2 * qb if has_prev else qb)

        def scores(blk, bi=bi, ksrc=ksrc, rows_of=rows_of):
            rows, krows, out_rows, nk = rows_of(blk)
            q2 = jnp.concatenate([dq_ref[bi, j, rows, :] for j in range(HEAD_PAIR)], axis=0)
            s_ref[blk[3], :, 0:nk] = lax.dot_general(
                q2, ksrc[krows, :], (((1,), (1,)), ((), ())), preferred_element_type=F32)

        def softmax(blk, bi=bi, rows_of=rows_of):
            rows, krows, out_rows, nk = rows_of(blk)
            s = s_ref[blk[3], :, 0:nk] + (bias_ref[bi] if blk[2] else bias0_ref[bi])
            m = jnp.max(s, axis=-1, keepdims=True)
            e = jnp.exp(s - m)
            l = jnp.sum(e, axis=-1, keepdims=True)
            p_ref[blk[3], :, 0:nk] = (e * (1.0 / l)).astype(BF16)
            lse = m + jnp.log(l)
            lacc_ref.at[bi][out_rows, :] = jnp.where(
                head1, jnp.broadcast_to(lse[qb:], (qb, LANES)), jnp.broadcast_to(lse[:qb], (qb, LANES)))

        def values(blk, bi=bi, vsrc=vsrc, rows_of=rows_of):
            rows, krows, out_rows, nk = rows_of(blk)
            v = vsrc[krows, :]
            o = [jnp.dot(p_ref[blk[3], j * qb:(j + 1) * qb, 0:nk], v, preferred_element_type=F32)
                 for j in range(HEAD_PAIR)]
            oacc_ref.at[bi][out_rows, :] = jnp.where(head1, o[1], o[0])

        def pipelined(n_groups, group, stages=(scores, softmax, values)):
            def step(t, valid):
                for k in reversed(range(len(stages))):
                    if valid(t - k):
                        for blk in group(t - k):
                            stages[k](blk)

            depth = len(stages) - 1
            head_steps = min(depth, n_groups)
            for t in range(head_steps):
                step(t, lambda g: 0 <= g < n_groups)
            if n_groups > depth:
                def body(t, carry):
                    step(t, lambda g: True)
                    return carry

                lax.fori_loop(depth, n_groups, body, 0)
            for t in range(max(n_groups, head_steps), n_groups + depth):
                step(t, lambda g: 0 <= g < n_groups)

        fu = min(d, ATT_FIRST_UNROLL)
        pipelined(d // fu, lambda it, fu=fu, sub=sub: [
            ((it * fu + u) * sub, it * fu + u, False, it * fu + u) for u in range(fu)])

        if nb > 1:
            lu = ATT_LATER_UNROLL
            per_res = (nb - 1) // lu
            assert per_res * lu == nb - 1

            def later_group(it, lu=lu, per_res=per_res, sub=sub, d=d):
                if per_res == 1:
                    r, n0 = it, 1
                elif d == 1:
                    r, n0 = 0, it * lu + 1
                elif isinstance(it, int):
                    r, n0 = it // per_res, (it % per_res) * lu + 1
                else:
                    r, n0 = lax.div(it, per_res), lax.rem(it, per_res) * lu + 1
                return [(r * sub + (n0 + u) * qb, r + d * qb * (n0 + u), True, d + it * lu + u)
                        for u in range(lu)]

            pipelined(d * per_res, later_group)

    def combine(tb, carry):
        rows = pl.ds(pl.multiple_of(tb * COMBINE_ROWS, COMBINE_ROWS), COMBINE_ROWS)
        lses = [lacc_ref[bi, rows, :] for bi in range(len(DILATED_BRANCHES))]
        top = functools.reduce(jnp.maximum, lses)
        ws = [jnp.exp(lse - top) for lse in lses]
        num = sum(w * oacc_ref[bi, rows, :] for bi, w in enumerate(ws))
        y_ref[rows, :] = (num / sum(ws)).astype(BF16)
        return carry

    lax.fori_loop(0, SEQ // COMBINE_ROWS, combine, 0)


def _attention(za):
    pairs = ATT_WIDTH // LANES
    col = lambda off: pl.BlockSpec((None, SEQ, LANES), lambda hp, b: (b, 0, hp + off))
    n_dil = len(DILATED_BRANCHES) - 1
    return pl.pallas_call(
        _attn_kernel,
        grid=(pairs, BATCH),
        in_specs=[col(0), col(pairs), col(2 * pairs)],
        out_specs=col(0),
        out_shape=jax.ShapeDtypeStruct((BATCH, SEQ, ATT_WIDTH), BF16),
        scratch_shapes=[pltpu.VMEM((2, SEQ, LANES), F32),
                        pltpu.VMEM((n_dil + 1, HEAD_PAIR, SEQ, LANES), BF16),
                        pltpu.VMEM((n_dil, SEQ, LANES), BF16),
                        pltpu.VMEM((n_dil, SEQ, LANES), BF16),
                        pltpu.VMEM((n_dil + 1, SEQ, LANES), F32),
                        pltpu.VMEM((n_dil + 1, SEQ, LANES), F32),
                        pltpu.VMEM((n_dil + 1, HEAD_PAIR * ATT_BLOCK, 2 * ATT_BLOCK), F32),
                        pltpu.VMEM((n_dil + 1, HEAD_PAIR * ATT_BLOCK, ATT_BLOCK), F32),
                        pltpu.VMEM((SEQ // ATT_BLOCK, HEAD_PAIR * ATT_BLOCK, 2 * ATT_BLOCK), BF16),
                        pltpu.VMEM((SEQ // ATT_BLOCK, HEAD_PAIR * ATT_BLOCK, 2 * ATT_BLOCK), F32)],
        compiler_params=_params("arbitrary", "arbitrary"),
        name="dilated_attention",
    )(za, za, za)


def _post_kernel(ys_ref, yr_ref, ya_ref, x_ref, wo_ref, an_ref, l1w_ref, l1b_ref,
                 w1_ref, w2_ref, l2w_ref, l2b_ref, o_ref, x1_ref, acc_ref):
    half = POST_TILE // 2
    n_chunks = D_FF // FF_CHUNK
    slab = half // n_chunks
    halves = (slice(0, half), slice(half, POST_TILE))

    def out_proj(rows):
        ya = ya_ref[rows, :].astype(F32)
        ms = jnp.mean(ya * ya, axis=-1, keepdims=True)
        ya = (ya * lax.rsqrt(ms + LN_EPS) * an_ref[...]).astype(BF16)
        h = jnp.dot(ys_ref[rows, :], wo_ref[0:SSM_WIDTH, :], preferred_element_type=F32)
        h = h + jnp.dot(yr_ref[rows, :], wo_ref[SSM_WIDTH:SSM_WIDTH + RET_WIDTH, :],
                        preferred_element_type=F32)
        h = h + jnp.dot(ya, wo_ref[SSM_WIDTH + RET_WIDTH:, :], preferred_element_type=F32)
        o_ref[rows, :] = DEEPNORM_ALPHA * x_ref[rows, :] + h

    def ln1(rows):
        x1_ref[rows, :] = _layer_norm(o_ref[rows, :], l1w_ref[...], l1b_ref[...])

    def ln2(rows):
        o_ref[rows, :] = _layer_norm(DEEPNORM_ALPHA * x1_ref[rows, :] + acc_ref[rows, :],
                                     l2w_ref[...], l2b_ref[...])

    def mlp(rows, between):
        xb = x1_ref[rows, :].astype(BF16)
        acc = jnp.zeros((half, D_MODEL), F32)
        for k in range(n_chunks):
            c = k * FF_CHUNK
            hid = jnp.dot(xb, w1_ref[:, c:c + FF_CHUNK], preferred_element_type=F32)
            hid = jnp.square(jnp.maximum(hid, 0.0)).astype(BF16)
            acc = acc + jnp.dot(hid, w2_ref[c:c + FF_CHUNK, :], preferred_element_type=F32)
            between(k)
        acc_ref[rows, :] = acc

    slab_of = lambda rows, k: slice(rows.start + k * slab, rows.start + (k + 1) * slab)
    out_proj(halves[0])
    out_proj(halves[1])
    ln1(halves[0])
    mlp(halves[0], lambda k: ln1(slab_of(halves[1], k)))
    mlp(halves[1], lambda k: ln2(slab_of(halves[0], k)))
    ln2(halves[1])


def _post(y_ssm, y_ret, y_att, x, w_out, attn_norm, ln1_w, ln1_b, w1, w2, ln2_w, ln2_b):
    tm = POST_TILE
    row = lambda width: pl.BlockSpec((None, tm, width), lambda b, i: (b, i, 0))
    full = lambda shape: pl.BlockSpec(shape, lambda b, i: (0, 0), pipeline_mode=pl.Buffered(1))
    return pl.pallas_call(
        _post_kernel,
        grid=(BATCH, SEQ // tm),
        in_specs=[row(SSM_WIDTH), row(RET_WIDTH), row(ATT_WIDTH), row(D_MODEL),
                  full((D_MODEL, D_MODEL)), full((1, ATT_WIDTH)), full((1, D_MODEL)), full((1, D_MODEL)),
                  full((D_MODEL, D_FF)), full((D_FF, D_MODEL)), full((1, D_MODEL)), full((1, D_MODEL))],
        out_specs=row(D_MODEL),
        out_shape=jax.ShapeDtypeStruct((BATCH, SEQ, D_MODEL), F32),
        scratch_shapes=[pltpu.VMEM((tm, D_MODEL), F32), pltpu.VMEM((tm, D_MODEL), F32)],
        compiler_params=_params("parallel", "parallel"),
        name="out_proj_mlp",
    )(y_ssm, y_ret, y_att, x, w_out, attn_norm, ln1_w, ln1_b, w1, w2, ln2_w, ln2_b)


def kernel(x, w_in, ssm_lambda_re, ssm_lambda_im, ssm_b_re, ssm_b_im, ssm_c_re, ssm_c_im, ssm_d, ssm_log_dt, ssm_w_glu, ssm_b_glu, ssm_out_norm, ret_out_norm, attn_out_norm, w_out, ln1_w, ln1_b, mlp_w1, mlp_w2, ln2_w, ln2_b):
    vec = lambda p: p.astype(F32).reshape(1, -1)
    x = x.astype(F32)
    for i in range(DEPTH):
        u, zr, za = _in_proj(x, w_in[i].astype(BF16))
        wb, a_re, a_im, wc = _ssm_weights(ssm_lambda_re[i], ssm_lambda_im[i], ssm_b_re[i], ssm_b_im[i],
                                          ssm_c_re[i], ssm_c_im[i], ssm_log_dt[i])
        y_ssm = _ssm(u, wb, a_re, a_im, wc, vec(ssm_d[i]), ssm_w_glu[i].astype(BF16),
                     vec(ssm_b_glu[i]), vec(ssm_out_norm[i]))
        y_ret = _retention(zr, vec(ret_out_norm[i]))
        y_att = _attention(za)
        x = _post(y_ssm, y_ret, y_att, x, w_out[i].astype(BF16), vec(attn_out_norm[i]),
                  vec(ln1_w[i]), vec(ln1_b[i]), mlp_w1[i].astype(BF16), mlp_w2[i].astype(BF16),
                  vec(ln2_w[i]), vec(ln2_b[i]))
    return x
```

```python
import functools
import math

import jax
import jax.numpy as jnp
from jax import lax
from jax.experimental import pallas as pl
from jax.experimental.pallas import tpu as pltpu

F32 = jnp.float32
BF16 = jnp.bfloat16

D_MODEL = 1024
BATCH = 16
SEQ = 2048
DEPTH = 2
SSM_WIDTH = 256
SSM_GROUP = 16
SSM_GROUPS = 16
SSM_STATE = 64
SSM_STATES = SSM_GROUPS * SSM_STATE
RET_HEAD_DIM = 64
RET_WIDTH = 256
RET_CHUNK = 128
ATT_HEAD_DIM = 64
ATT_WIDTH = 512
ATT_HEADS = 8
DILATED_BRANCHES = ((128, 1), (512, 4), (2048, 16))
ATT_BLOCK = 128
IN_WIDTH = SSM_WIDTH + 4 * RET_WIDTH + 3 * ATT_WIDTH
D_FF = 4 * D_MODEL
DEEPNORM_ALPHA = (2 * DEPTH) ** 0.25
LN_EPS = 1e-5

LANES = 128
HEAD_PAIR = LANES // ATT_HEAD_DIM
VMEM_LIMIT_BYTES = 56 * 1024 * 1024
MASKED_SCORE = -1e30

TOKEN_TILE = 512
POST_TILE = 1024
SSM_TIME_TILE = 64
SSM_BATCH_HALF = BATCH // 2
SSM_COL_CHUNK = 512
SSM_STAGE_PITCH = SSM_TIME_TILE + 8
FF_CHUNK = 1024
COMBINE_ROWS = 256
PREP_ROWS = 128
ATT_FIRST_UNROLL = 4
ATT_LATER_UNROLL = 3
RET_UNROLL = 16

assert RET_HEAD_DIM == ATT_HEAD_DIM and HEAD_PAIR == 2
assert all(w // d == ATT_BLOCK for w, d in DILATED_BRANCHES)
assert DILATED_BRANCHES[0][1] == 1 and all(
    b[1] % a[1] == 0 for a, b in zip(DILATED_BRANCHES, DILATED_BRANCHES[1:]))


def _params(*semantics):
    return pltpu.CompilerParams(dimension_semantics=semantics, vmem_limit_bytes=VMEM_LIMIT_BYTES)


def _sigmoid(x):
    return 1.0 / (1.0 + jnp.exp(-x))


def _layer_norm(r, w, b):
    mu = jnp.mean(r, axis=-1, keepdims=True)
    d = r - mu
    var = jnp.mean(d * d, axis=-1, keepdims=True)
    return d * lax.rsqrt(var + LN_EPS) * w + b


def _in_proj_kernel(x_ref, w_ref, u_ref, zr_ref, za_ref):
    xb = x_ref[...].astype(BF16)

    def proj(lo, hi):
        return jnp.dot(xb, w_ref[:, lo:hi], preferred_element_type=F32).astype(BF16)

    u_ref[...] = proj(0, SSM_WIDTH)
    zr_ref[...] = proj(SSM_WIDTH, SSM_WIDTH + 4 * RET_WIDTH)
    za_ref[...] = proj(SSM_WIDTH + 4 * RET_WIDTH, IN_WIDTH)


def _in_proj(x, w_in):
    tm = TOKEN_TILE
    row = lambda width: pl.BlockSpec((None, tm, width), lambda b, i: (b, i, 0))
    return pl.pallas_call(
        _in_proj_kernel,
        grid=(BATCH, SEQ // tm),
        in_specs=[row(D_MODEL), pl.BlockSpec((D_MODEL, IN_WIDTH), lambda b, i: (0, 0))],
        out_specs=[row(SSM_WIDTH), row(4 * RET_WIDTH), row(3 * ATT_WIDTH)],
        out_shape=[jax.ShapeDtypeStruct((BATCH, SEQ, SSM_WIDTH), BF16),
                   jax.ShapeDtypeStruct((BATCH, SEQ, 4 * RET_WIDTH), BF16),
                   jax.ShapeDtypeStruct((BATCH, SEQ, 3 * ATT_WIDTH), BF16)],
        compiler_params=_params("parallel", "parallel"),
        name="in_proj",
    )(x, w_in)


def _ssm_kernel(u_ref, wb_ref, are_ref, aim_ref, wc_ref, d_ref, wglu_ref, bglu_ref, nw_ref,
                y_ref, bu_ref, st_ref, ut_ref, stage_ref):
    n_st = SSM_STATES
    bh = SSM_BATCH_HALF
    lt = SSM_TIME_TILE
    pitch = SSM_STAGE_PITCH
    lane_slabs = SSM_WIDTH // LANES
    chunks = [(slice(c, c + SSM_COL_CHUNK), slice(n_st + c, n_st + c + SSM_COL_CHUNK))
              for c in range(0, n_st, SSM_COL_CHUNK)]

    @pl.when(pl.program_id(0) == 0)
    def _():
        st_ref[...] = jnp.zeros_like(st_ref)

    def stage_rows(h, t):
        return pl.ds(h * bh * pitch + t, bh, stride=pitch)

    def gather(h):
        for b in range(h * bh, (h + 1) * bh):
            for s in range(lane_slabs):
                stage_ref[s, b * pitch:b * pitch + lt, :] = (
                    u_ref[b, :, s * LANES:(s + 1) * LANES].astype(F32))
        for t in range(0, lt, 2):
            for s in range(lane_slabs):
                pair = jnp.concatenate([stage_ref.at[s][stage_rows(h, t + k), :] for k in range(2)], axis=0)
                ut_ref[h, t * bh:(t + 2) * bh, s * LANES:(s + 1) * LANES] = pair.astype(BF16)

    def scatter(h, y):
        for t in range(lt):
            for s in range(lane_slabs):
                stage_ref.at[s][stage_rows(h, t), :] = y[t * bh:(t + 1) * bh, s * LANES:(s + 1) * LANES]
        for b in range(h * bh, (h + 1) * bh):
            for s in range(lane_slabs):
                y_ref[b, :, s * LANES:(s + 1) * LANES] = (
                    stage_ref[s, b * pitch:b * pitch + lt, :].astype(BF16))

    def project(h, cols):
        for sl in cols:
            bu_ref[h, :, sl] = jnp.dot(ut_ref[h], wb_ref[:, sl], preferred_element_type=F32)

    def scan(h, cols):
        re, im = cols
        ar = jnp.broadcast_to(are_ref[:, re], (bh, SSM_COL_CHUNK))
        ai = jnp.broadcast_to(aim_ref[:, re], (bh, SSM_COL_CHUNK))
        xr, xi = st_ref[h, :, re], st_ref[h, :, im]
        for t in range(SSM_TIME_TILE):
            rows = slice(t * bh, (t + 1) * bh)
            xr, xi = (ar * xr - ai * xi + bu_ref[h, rows, re],
                      ar * xi + ai * xr + bu_ref[h, rows, im])
            bu_ref[h, rows, re] = xr
            bu_ref[h, rows, im] = xi
        st_ref[h, :, re] = xr
        st_ref[h, :, im] = xi

    def readout(h, cols, acc):
        for sl in cols:
            acc = acc + jnp.dot(bu_ref[h, :, sl].astype(BF16), wc_ref[sl, :], preferred_element_type=F32)
        return acc

    def finish(h, y):
        y = y + d_ref[...] * ut_ref[h].astype(F32)
        cdf = 0.5 * (1.0 + jnp.tanh(math.sqrt(2.0 / math.pi) * (y + 0.044715 * (y * y * y))))
        g = y * cdf
        gate = jnp.dot(g.astype(BF16), wglu_ref[...], preferred_element_type=F32) + bglu_ref[...]
        out = g * _sigmoid(gate)
        ms = jnp.mean(out * out, axis=-1, keepdims=True)
        scatter(h, out * lax.rsqrt(ms + LN_EPS) * nw_ref[...])

    zero = jnp.zeros((SSM_TIME_TILE * bh, SSM_WIDTH), F32)
    gather(0)
    for k, cols in enumerate(chunks):
        project(0, cols)
        if k == 0:
            gather(1)
    for cols in chunks:
        project(1, cols)
        scan(0, cols)
    y0 = zero
    for cols in chunks:
        y0 = readout(0, cols, y0)
        scan(1, cols)
    y1 = zero
    for k, cols in enumerate(chunks):
        y1 = readout(1, cols, y1)
        if k == 0:
            finish(0, y0)
    finish(1, y1)


def _ssm(u, wb, a_re, a_im, wc, d_skip, w_glu, b_glu, norm_w):
    rows = SSM_TIME_TILE * SSM_BATCH_HALF
    full = lambda shape: pl.BlockSpec(shape, lambda i: (0, 0))
    steps = pl.BlockSpec((BATCH, SSM_TIME_TILE, SSM_WIDTH), lambda i: (0, i, 0))
    return pl.pallas_call(
        _ssm_kernel,
        grid=(SEQ // SSM_TIME_TILE,),
        in_specs=[steps,
                  full((SSM_WIDTH, 2 * SSM_STATES)), full((1, SSM_STATES)), full((1, SSM_STATES)),
                  full((2 * SSM_STATES, SSM_WIDTH)), full((1, SSM_WIDTH)),
                  full((SSM_WIDTH, SSM_WIDTH)), full((1, SSM_WIDTH)), full((1, SSM_WIDTH))],
        out_specs=steps,
        out_shape=jax.ShapeDtypeStruct((BATCH, SEQ, SSM_WIDTH), BF16),
        scratch_shapes=[pltpu.VMEM((2, rows, 2 * SSM_STATES), F32),
                        pltpu.VMEM((2, SSM_BATCH_HALF, 2 * SSM_STATES), F32),
                        pltpu.VMEM((2, rows, SSM_WIDTH), BF16),
                        pltpu.VMEM((SSM_WIDTH // LANES, BATCH * SSM_STAGE_PITCH, LANES), F32)],
        compiler_params=_params("arbitrary"),
        name="s5_mixer",
    )(u, wb, a_re, a_im, wc, d_skip, w_glu, b_glu, norm_w)


def _ssm_weights(lam_re, lam_im, b_re, b_im, c_re, c_im, log_dt):
    g, p, h = SSM_GROUPS, SSM_STATE, SSM_GROUP
    lr, li = lam_re.astype(F32), lam_im.astype(F32)
    dt = jnp.exp(log_dt.astype(F32))[:, None]
    mag = jnp.exp(lr * dt)
    a_re, a_im = mag * jnp.cos(li * dt), mag * jnp.sin(li * dt)
    den = lr * lr + li * li
    nr, ni = a_re - 1.0, a_im
    f_re = ((nr * lr + ni * li) / den)[..., None]
    f_im = ((ni * lr - nr * li) / den)[..., None]
    br, bi = b_re.astype(F32), b_im.astype(F32)
    bb_re = f_re * br - f_im * bi
    bb_im = f_re * bi + f_im * br
    eye = jnp.eye(g, dtype=F32)
    embed_b = lambda bb: jnp.einsum('gph,gk->ghkp', bb, eye).reshape(g * h, g * p)
    embed_c = lambda cc: jnp.einsum('ghp,gk->gpkh', cc, eye).reshape(g * p, g * h)
    wb = jnp.concatenate([embed_b(bb_re), embed_b(bb_im)], axis=1).astype(BF16)
    wc = jnp.concatenate([embed_c(c_re.astype(F32)), -embed_c(c_im.astype(F32))], axis=0).astype(BF16)
    return wb, a_re.reshape(1, g * p), a_im.reshape(1, g * p), wc


def _ret_kernel(q_ref, k_ref, v_ref, g_ref, nw_ref, y_ref, dmat_ref):
    c = RET_CHUNK
    dh = RET_HEAD_DIM
    hp = pl.program_id(1)
    lane = lax.broadcasted_iota(jnp.int32, (c, LANES), 1)
    row = lax.broadcasted_iota(jnp.int32, (c, LANES), 0)
    head1 = lane >= dh
    head_of_lane = (HEAD_PAIR * hp).astype(F32) + head1.astype(F32)
    lg = jnp.log(1.0 - jnp.exp2(-5.0 - head_of_lane))
    rowf = row.astype(F32)
    xi = jnp.exp((rowf + 1.0) * lg)
    zeta = jnp.exp((c - 1.0 - rowf) * lg) * (dh ** -0.5)
    g_chunk = jnp.exp(c * lg)
    block_diag = (row >= dh) == head1
    diff = (row - lane).astype(F32)
    for j in range(HEAD_PAIR):
        lg_j = jnp.log(1.0 - jnp.exp2(jnp.zeros((c, c), F32) - 5.0 - (HEAD_PAIR * hp + j).astype(F32)))
        dmat_ref[j] = jnp.where(diff >= 0, jnp.exp(jnp.maximum(diff, 0.0) * lg_j), 0.0) * (dh ** -0.5)
    nw = nw_ref[...]

    def head_mean(t):
        s0 = jnp.sum(jnp.where(head1, 0.0, t), axis=-1, keepdims=True)
        s1 = jnp.sum(jnp.where(head1, t, 0.0), axis=-1, keepdims=True)
        return jnp.where(head1, s1, s0) * (1.0 / dh)

    def chunk(n, r_prev):
        rows = pl.ds(pl.multiple_of(n * c, c), c)
        q, k, v = q_ref[rows, :], k_ref[rows, :], v_ref[rows, :]
        o = jnp.zeros((c, LANES), F32)
        for j in range(HEAD_PAIR):
            mine = head1 if j else jnp.logical_not(head1)
            qj = jnp.where(mine, q, jnp.zeros_like(q))
            s = lax.dot_general(qj, k, (((1,), (1,)), ((), ())), preferred_element_type=F32)
            s = s * dmat_ref[j]
            vj = jnp.where(mine, v, jnp.zeros_like(v))
            o = o + jnp.dot(s.astype(BF16), vj, preferred_element_type=F32)
        qx = (q.astype(F32) * xi).astype(BF16)
        o = o + jnp.dot(qx, r_prev.astype(BF16), preferred_element_type=F32)
        kz = (k.astype(F32) * zeta).T.astype(BF16)
        kv = jnp.dot(kz, v, preferred_element_type=F32)
        dlt = o - head_mean(o)
        var = head_mean(dlt * dlt)
        gate = g_ref[rows, :].astype(F32)
        y = dlt * lax.rsqrt(var + LN_EPS) * nw * (gate * _sigmoid(gate))
        y_ref[rows, :] = y.astype(BF16)
        return jnp.where(block_diag, g_chunk * r_prev + kv, 0.0)

    def chunks(it, r):
        for u in range(RET_UNROLL):
            r = chunk(it * RET_UNROLL + u, r)
        return r

    lax.fori_loop(0, SEQ // (c * RET_UNROLL), chunks, jnp.zeros((LANES, LANES), F32))


def _retention(zr, norm_w):
    col = lambda off: pl.BlockSpec((None, SEQ, LANES), lambda b, hp: (b, 0, hp + off))
    pairs = RET_WIDTH // LANES
    return pl.pallas_call(
        _ret_kernel,
        grid=(BATCH, pairs),
        in_specs=[col(0), col(pairs), col(2 * pairs), col(3 * pairs),
                  pl.BlockSpec((1, LANES), lambda b, hp: (0, hp))],
        out_specs=col(0),
        out_shape=jax.ShapeDtypeStruct((BATCH, SEQ, RET_WIDTH), BF16),
        scratch_shapes=[pltpu.VMEM((HEAD_PAIR, RET_CHUNK, RET_CHUNK), F32)],
        compiler_params=_params("parallel", "parallel"),
        name="retention",
    )(zr, zr, zr, zr, norm_w)


def _attn_kernel(q_ref, k_ref, v_ref, y_ref, src_ref, dq_ref, dk_ref, dv_ref,
                 oacc_ref, macc_ref, lacc_ref, bias_ref, bias0_ref, p_ref, s_ref):
    qb = ATT_BLOCK
    dh = ATT_HEAD_DIM
    hp = pl.program_id(1)
    head1 = lax.broadcasted_iota(jnp.int32, (qb, LANES), 1) >= dh

    def emit_q(bi, rows, blk):
        other = lax.broadcasted_iota(jnp.int32, blk.shape, 1) >= dh
        blk = blk * (dh ** -0.5)
        dq_ref[bi, 0, rows, :] = jnp.where(other, 0.0, blk).astype(BF16)
        dq_ref[bi, 1, rows, :] = jnp.where(other, blk, 0.0).astype(BF16)

    def emit_k(bi, rows, blk):
        if bi > 0:
            dk_ref[bi - 1, rows, :] = blk.astype(BF16)

    def emit_v(bi, rows, blk):
        if bi > 0:
            dv_ref[bi - 1, rows, :] = blk.astype(BF16)

    piece = PREP_ROWS
    for src, emit in ((q_ref, emit_q), (k_ref, emit_k), (v_ref, emit_v)):
        for c0 in range(0, SEQ, piece):
            rows = slice(c0, c0 + piece)
            natural = src[rows, :].astype(F32)
            src_ref[0, rows, :] = natural
            emit(0, rows, natural)
        d_prev = 1
        for bi in range(1, len(DILATED_BRANCHES)):
            d = DILATED_BRANCHES[bi][1]
            step = d // d_prev
            sub_prev, sub = SEQ // d_prev, SEQ // d
            cur, nxt = (bi - 1) % 2, bi % 2
            for r_prev in range(d_prev):
                for t in range(step):
                    r = r_prev + d_prev * t
                    for c0 in range(0, sub, piece):
                        n = min(piece, sub - c0)
                        rows = slice(r * sub + c0, r * sub + c0 + n)
                        blk = src_ref.at[cur][pl.ds(r_prev * sub_prev + t + step * c0, n, stride=step), :]
                        emit(bi, rows, blk)
                        if bi + 1 < len(DILATED_BRANCHES):
                            src_ref[nxt, rows, :] = blk
            d_prev = d

    row2 = lax.broadcasted_iota(jnp.int32, (HEAD_PAIR * qb, 2 * qb), 0)
    key2 = lax.broadcasted_iota(jnp.int32, (HEAD_PAIR * qb, 2 * qb), 1)
    row1 = lax.broadcasted_iota(jnp.int32, (HEAD_PAIR * qb, qb), 0)
    key1 = lax.broadcasted_iota(jnp.int32, (HEAD_PAIR * qb, qb), 1)

    def head_slope(row):
        head = (HEAD_PAIR * hp).astype(F32) + (row >= qb).astype(F32)
        return jnp.exp2(-(8.0 / ATT_HEADS) * (head + 1.0))

    dist2 = qb + jnp.where(row2 >= qb, row2 - qb, row2) - key2
    dist1 = jnp.where(row1 >= qb, row1 - qb, row1) - key1
    slope2, slope1 = head_slope(row2), head_slope(row1)

    def aligned(x):
        return x if isinstance(x, int) else pl.multiple_of(x, qb)

    for bi, (window, d) in enumerate(DILATED_BRANCHES):
        sub = SEQ // d
        nb = sub // qb
        ksrc = k_ref if bi == 0 else dk_ref.at[bi - 1]
        vsrc = v_ref if bi == 0 else dv_ref.at[bi - 1]
        bias_ref[...] = jnp.where((dist2 >= 0) & (dist2 <= window // d),
                                  -slope2 * (d * dist2).astype(F32), MASKED_SCORE)
        bias0_ref[...] = jnp.where(dist1 >= 0, -slope1 * (d * dist1).astype(F32), MASKED_SCORE)

        def rows_of(blk, d=d):
            base, first_token, has_prev, _ = blk
            rows = pl.ds(aligned(base), qb)
            krows = pl.ds(aligned(base - qb), 2 * qb) if has_prev else rows
            out_rows = rows if d == 1 else pl.ds(first_token, qb, stride=d)
            return rows, krows, out_rows, (2 * qb if has_prev else qb)

        def scores(blk, bi=bi, ksrc=ksrc, rows_of=rows_of):
            rows, krows, out_rows, nk = rows_of(blk)
            q2 = jnp.concatenate([dq_ref[bi, j, rows, :] for j in range(HEAD_PAIR)], axis=0)
            s_ref[blk[3], :, 0:nk] = lax.dot_general(
                q2, ksrc[krows, :], (((1,), (1,)), ((), ())), preferred_element_type=F32)

        def softmax(blk, bi=bi, rows_of=rows_of):
            rows, krows, out_rows, nk = rows_of(blk)
            s = s_ref[blk[3], :, 0:nk] + (bias_ref[...] if blk[2] else bias0_ref[...])
            m = jnp.max(s, axis=-1, keepdims=True)
            e = jnp.exp(s - m)
            l = jnp.sum(e, axis=-1, keepdims=True)
            p_ref[blk[3], :, 0:nk] = e.astype(BF16)
            packed = lambda c: jnp.where(head1, jnp.broadcast_to(c[qb:], (qb, LANES)),
                                         jnp.broadcast_to(c[:qb], (qb, LANES)))
            macc_ref.at[bi][out_rows, :] = packed(m)
            lacc_ref.at[bi][out_rows, :] = packed(l)

        def values(blk, bi=bi, vsrc=vsrc, rows_of=rows_of):
            rows, krows, out_rows, nk = rows_of(blk)
            v = vsrc[krows, :]
            o = [jnp.dot(p_ref[blk[3], j * qb:(j + 1) * qb, 0:nk], v, preferred_element_type=F32)
                 for j in range(HEAD_PAIR)]
            oacc_ref.at[bi][out_rows, :] = jnp.where(head1, o[1], o[0])

        def pipelined(n_groups, group, stages=(scores, softmax, values)):
            def step(t, valid):
                for k in reversed(range(len(stages))):
                    if valid(t - k):
                        for blk in group(t - k):
                            stages[k](blk)

            depth = len(stages) - 1
            head_steps = min(depth, n_groups)
            for t in range(head_steps):
                step(t, lambda g: 0 <= g < n_groups)
            if n_groups > depth:
                def body(t, carry):
                    step(t, lambda g: True)
                    return carry

                lax.fori_loop(depth, n_groups, body, 0)
            for t in range(max(n_groups, head_steps), n_groups + depth):
                step(t, lambda g: 0 <= g < n_groups)

        fu = min(d, ATT_FIRST_UNROLL)
        pipelined(d // fu, lambda it, fu=fu, sub=sub: [
            ((it * fu + u) * sub, it * fu + u, False, it * fu + u) for u in range(fu)])

        if nb > 1:
            lu = ATT_LATER_UNROLL
            per_res = (nb - 1) // lu
            assert per_res * lu == nb - 1

            def later_group(it, lu=lu, per_res=per_res, sub=sub, d=d):
                if per_res == 1:
                    r, n0 = it, 1
                elif d == 1:
                    r, n0 = 0, it * lu + 1
                elif isinstance(it, int):
                    r, n0 = it // per_res, (it % per_res) * lu + 1
                else:
                    r, n0 = lax.div(it, per_res), lax.rem(it, per_res) * lu + 1
                return [(r * sub + (n0 + u) * qb, r + d * qb * (n0 + u), True, d + it * lu + u)
                        for u in range(lu)]

            pipelined(d * per_res, later_group)

    def combine(tb, carry):
        rows = pl.ds(pl.multiple_of(tb * COMBINE_ROWS, COMBINE_ROWS), COMBINE_ROWS)
        ms = [macc_ref[bi, rows, :] for bi in range(len(DILATED_BRANCHES))]
        top = functools.reduce(jnp.maximum, ms)
        ws = [jnp.exp(m - top) for m in ms]
        num = sum(w * oacc_ref[bi, rows, :] for bi, w in enumerate(ws))
        den = sum(w * lacc_ref[bi, rows, :] for bi, w in enumerate(ws))
        y_ref[rows, :] = (num / den).astype(BF16)
        return carry

    lax.fori_loop(0, SEQ // COMBINE_ROWS, combine, 0)


def _attention(za):
    pairs = ATT_WIDTH // LANES
    col = lambda off: pl.BlockSpec((None, SEQ, LANES), lambda b, hp: (b, 0, hp + off))
    n_dil = len(DILATED_BRANCHES) - 1
    return pl.pallas_call(
        _attn_kernel,
        grid=(BATCH, pairs),
        in_specs=[col(0), col(pairs), col(2 * pairs)],
        out_specs=col(0),
        out_shape=jax.ShapeDtypeStruct((BATCH, SEQ, ATT_WIDTH), BF16),
        scratch_shapes=[pltpu.VMEM((2, SEQ, LANES), F32),
                        pltpu.VMEM((n_dil + 1, HEAD_PAIR, SEQ, LANES), BF16),
                        pltpu.VMEM((n_dil, SEQ, LANES), BF16),
                        pltpu.VMEM((n_dil, SEQ, LANES), BF16),
                        pltpu.VMEM((n_dil + 1, SEQ, LANES), F32),
                        pltpu.VMEM((n_dil + 1, SEQ, LANES), F32),
                        pltpu.VMEM((n_dil + 1, SEQ, LANES), F32),
                        pltpu.VMEM((HEAD_PAIR * ATT_BLOCK, 2 * ATT_BLOCK), F32),
                        pltpu.VMEM((HEAD_PAIR * ATT_BLOCK, ATT_BLOCK), F32),
                        pltpu.VMEM((SEQ // ATT_BLOCK, HEAD_PAIR * ATT_BLOCK, 2 * ATT_BLOCK), BF16),
                        pltpu.VMEM((SEQ // ATT_BLOCK, HEAD_PAIR * ATT_BLOCK, 2 * ATT_BLOCK), F32)],
        compiler_params=_params("parallel", "parallel"),
        name="dilated_attention",
    )(za, za, za)


def _post_kernel(ys_ref, yr_ref, ya_ref, x_ref, wo_ref, an_ref, l1w_ref, l1b_ref,
                 w1_ref, w2_ref, l2w_ref, l2b_ref, o_ref, x1_ref, acc_ref):
    half = POST_TILE // 2
    n_chunks = D_FF // FF_CHUNK
    slab = half // n_chunks
    halves = (slice(0, half), slice(half, POST_TILE))

    def out_proj(rows):
        ya = ya_ref[rows, :].astype(F32)
        ms = jnp.mean(ya * ya, axis=-1, keepdims=True)
        ya = (ya * lax.rsqrt(ms + LN_EPS) * an_ref[...]).astype(BF16)
        h = jnp.dot(ys_ref[rows, :], wo_ref[0:SSM_WIDTH, :], preferred_element_type=F32)
        h = h + jnp.dot(yr_ref[rows, :], wo_ref[SSM_WIDTH:SSM_WIDTH + RET_WIDTH, :],
                        preferred_element_type=F32)
        h = h + jnp.dot(ya, wo_ref[SSM_WIDTH + RET_WIDTH:, :], preferred_element_type=F32)
        o_ref[rows, :] = DEEPNORM_ALPHA * x_ref[rows, :] + h

    def ln1(rows):
        x1_ref[rows, :] = _layer_norm(o_ref[rows, :], l1w_ref[...], l1b_ref[...])

    def ln2(rows):
        o_ref[rows, :] = _layer_norm(DEEPNORM_ALPHA * x1_ref[rows, :] + acc_ref[rows, :],
                                     l2w_ref[...], l2b_ref[...])

    def mlp(rows, between):
        xb = x1_ref[rows, :].astype(BF16)
        acc = jnp.zeros((half, D_MODEL), F32)
        for k in range(n_chunks):
            c = k * FF_CHUNK
            hid = jnp.dot(xb, w1_ref[:, c:c + FF_CHUNK], preferred_element_type=F32)
            hid = jnp.square(jnp.maximum(hid, 0.0)).astype(BF16)
            acc = acc + jnp.dot(hid, w2_ref[c:c + FF_CHUNK, :], preferred_element_type=F32)
            between(k)
        acc_ref[rows, :] = acc

    slab_of = lambda rows, k: slice(rows.start + k * slab, rows.start + (k + 1) * slab)
    out_proj(halves[0])
    out_proj(halves[1])
    ln1(halves[0])
    mlp(halves[0], lambda k: ln1(slab_of(halves[1], k)))
    mlp(halves[1], lambda k: ln2(slab_of(halves[0], k)))
    ln2(halves[1])


def _post(y_ssm, y_ret, y_att, x, w_out, attn_norm, ln1_w, ln1_b, w1, w2, ln2_w, ln2_b):
    tm = POST_TILE
    row = lambda width: pl.BlockSpec((None, tm, width), lambda b, i: (b, i, 0))
    full = lambda shape: pl.BlockSpec(shape, lambda b, i: (0, 0), pipeline_mode=pl.Buffered(1))
    return pl.pallas_call(
        _post_kernel,
        grid=(BATCH, SEQ // tm),
        in_specs=[row(SSM_WIDTH), row(RET_WIDTH), row(ATT_WIDTH), row(D_MODEL),
                  full((D_MODEL, D_MODEL)), full((1, ATT_WIDTH)), full((1, D_MODEL)), full((1, D_MODEL)),
                  full((D_MODEL, D_FF)), full((D_FF, D_MODEL)), full((1, D_MODEL)), full((1, D_MODEL))],
        out_specs=row(D_MODEL),
        out_shape=jax.ShapeDtypeStruct((BATCH, SEQ, D_MODEL), F32),
        scratch_shapes=[pltpu.VMEM((tm, D_MODEL), F32), pltpu.VMEM((tm, D_MODEL), F32)],
        compiler_params=_params("parallel", "parallel"),
        name="out_proj_mlp",
    )(y_ssm, y_ret, y_att, x, w_out, attn_norm, ln1_w, ln1_b, w1, w2, ln2_w, ln2_b)


def kernel(x, w_in, ssm_lambda_re, ssm_lambda_im, ssm_b_re, ssm_b_im, ssm_c_re, ssm_c_im, ssm_d, ssm_log_dt, ssm_w_glu, ssm_b_glu, ssm_out_norm, ret_out_norm, attn_out_norm, w_out, ln1_w, ln1_b, mlp_w1, mlp_w2, ln2_w, ln2_b):
    vec = lambda p: p.astype(F32).reshape(1, -1)
    x = x.astype(F32)
    for i in range(DEPTH):
        u, zr, za = _in_proj(x, w_in[i].astype(BF16))
        wb, a_re, a_im, wc = _ssm_weights(ssm_lambda_re[i], ssm_lambda_im[i], ssm_b_re[i], ssm_b_im[i],
                                          ssm_c_re[i], ssm_c_im[i], ssm_log_dt[i])
        y_ssm = _ssm(u, wb, a_re, a_im, wc, vec(ssm_d[i]), ssm_w_glu[i].astype(BF16),
                     vec(ssm_b_glu[i]), vec(ssm_out_norm[i]))
        y_ret = _retention(zr, vec(ret_out_norm[i]))
        y_att = _attention(za)
        x = _post(y_ssm, y_ret, y_att, x, w_out[i].astype(BF16), vec(attn_out_norm[i]),
                  vec(ln1_w[i]), vec(ln1_b[i]), mlp_w1[i].astype(BF16), mlp_w2[i].astype(BF16),
                  vec(ln2_w[i]), vec(ln2_b[i]))
    return x
```

```python
import functools
import math

import jax
import jax.numpy as jnp
from jax import lax
from jax.experimental import pallas as pl
from jax.experimental.pallas import tpu as pltpu

F32 = jnp.float32
BF16 = jnp.bfloat16

D_MODEL = 1024
BATCH = 16
SEQ = 2048
DEPTH = 2
SSM_WIDTH = 256
SSM_GROUP = 16
SSM_GROUPS = 16
SSM_STATE = 64
SSM_STATES = SSM_GROUPS * SSM_STATE
RET_HEAD_DIM = 64
RET_WIDTH = 256
RET_CHUNK = 128
ATT_HEAD_DIM = 64
ATT_WIDTH = 512
ATT_HEADS = 8
DILATED_BRANCHES = ((128, 1), (512, 4), (2048, 16))
ATT_BLOCK = 128
IN_WIDTH = SSM_WIDTH + 4 * RET_WIDTH + 3 * ATT_WIDTH
D_FF = 4 * D_MODEL
DEEPNORM_ALPHA = (2 * DEPTH) ** 0.25
LN_EPS = 1e-5

LANES = 128
HEAD_PAIR = LANES // ATT_HEAD_DIM
VMEM_LIMIT_BYTES = 56 * 1024 * 1024
MASKED_SCORE = -1e30

TOKEN_TILE = 512
POST_TILE = 1024
SSM_TIME_TILE = 64
SSM_BATCH_HALF = BATCH // 2
SSM_COL_CHUNK = 512
SSM_STAGE_PITCH = SSM_TIME_TILE + 8
FF_CHUNK = 1024
COMBINE_ROWS = 256
ATT_HOME_BRANCH = 1
PREP_ROWS = 128
ATT_FIRST_UNROLL = 4
ATT_LATER_UNROLL = 3
RET_UNROLL = 16

assert RET_HEAD_DIM == ATT_HEAD_DIM and HEAD_PAIR == 2
assert all(w // d == ATT_BLOCK for w, d in DILATED_BRANCHES)
assert DILATED_BRANCHES[0][1] == 1 and all(
    b[1] % a[1] == 0 for a, b in zip(DILATED_BRANCHES, DILATED_BRANCHES[1:]))


def _params(*semantics):
    return pltpu.CompilerParams(dimension_semantics=semantics, vmem_limit_bytes=VMEM_LIMIT_BYTES)


def _sigmoid(x):
    return 1.0 / (1.0 + jnp.exp(-x))


def _layer_norm(r, w, b):
    mu = jnp.mean(r, axis=-1, keepdims=True)
    d = r - mu
    var = jnp.mean(d * d, axis=-1, keepdims=True)
    return d * lax.rsqrt(var + LN_EPS) * w + b


def _in_proj_kernel(x_ref, w_ref, u_ref, zr_ref, za_ref):
    xb = x_ref[...].astype(BF16)

    def proj(lo, hi):
        return jnp.dot(xb, w_ref[:, lo:hi], preferred_element_type=F32).astype(BF16)

    u_ref[...] = proj(0, SSM_WIDTH)
    zr_ref[...] = proj(SSM_WIDTH, SSM_WIDTH + 4 * RET_WIDTH)
    za_ref[...] = proj(SSM_WIDTH + 4 * RET_WIDTH, IN_WIDTH)


def _in_proj(x, w_in):
    tm = TOKEN_TILE
    row = lambda width: pl.BlockSpec((None, tm, width), lambda b, i: (b, i, 0))
    return pl.pallas_call(
        _in_proj_kernel,
        grid=(BATCH, SEQ // tm),
        in_specs=[row(D_MODEL), pl.BlockSpec((D_MODEL, IN_WIDTH), lambda b, i: (0, 0))],
        out_specs=[row(SSM_WIDTH), row(4 * RET_WIDTH), row(3 * ATT_WIDTH)],
        out_shape=[jax.ShapeDtypeStruct((BATCH, SEQ, SSM_WIDTH), BF16),
                   jax.ShapeDtypeStruct((BATCH, SEQ, 4 * RET_WIDTH), BF16),
                   jax.ShapeDtypeStruct((BATCH, SEQ, 3 * ATT_WIDTH), BF16)],
        compiler_params=_params("parallel", "parallel"),
        name="in_proj",
    )(x, w_in)


def _ssm_kernel(u_ref, wb_ref, are_ref, aim_ref, wc_ref, d_ref, wglu_ref, bglu_ref, nw_ref,
                y_ref, bu_ref, st_ref, ut_ref, stage_ref):
    n_st = SSM_STATES
    bh = SSM_BATCH_HALF
    lt = SSM_TIME_TILE
    pitch = SSM_STAGE_PITCH
    lane_slabs = SSM_WIDTH // LANES
    chunks = [(slice(c, c + SSM_COL_CHUNK), slice(n_st + c, n_st + c + SSM_COL_CHUNK))
              for c in range(0, n_st, SSM_COL_CHUNK)]

    @pl.when(pl.program_id(0) == 0)
    def _():
        st_ref[...] = jnp.zeros_like(st_ref)

    def stage_rows(h, t):
        return pl.ds(h * bh * pitch + t, bh, stride=pitch)

    def gather(h):
        for b in range(h * bh, (h + 1) * bh):
            for s in range(lane_slabs):
                stage_ref[s, b * pitch:b * pitch + lt, :] = (
                    u_ref[b, :, s * LANES:(s + 1) * LANES].astype(F32))
        for t in range(0, lt, 2):
            for s in range(lane_slabs):
                pair = jnp.concatenate([stage_ref.at[s][stage_rows(h, t + k), :] for k in range(2)], axis=0)
                ut_ref[h, t * bh:(t + 2) * bh, s * LANES:(s + 1) * LANES] = pair.astype(BF16)

    def scatter(h, y):
        for t in range(lt):
            for s in range(lane_slabs):
                stage_ref.at[s][stage_rows(h, t), :] = y[t * bh:(t + 1) * bh, s * LANES:(s + 1) * LANES]
        for b in range(h * bh, (h + 1) * bh):
            for s in range(lane_slabs):
                y_ref[b, :, s * LANES:(s + 1) * LANES] = (
                    stage_ref[s, b * pitch:b * pitch + lt, :].astype(BF16))

    def project(h, cols):
        for sl in cols:
            bu_ref[h, :, sl] = jnp.dot(ut_ref[h], wb_ref[:, sl], preferred_element_type=F32)

    def scan(h, cols):
        re, im = cols
        ar = jnp.broadcast_to(are_ref[:, re], (bh, SSM_COL_CHUNK))
        ai = jnp.broadcast_to(aim_ref[:, re], (bh, SSM_COL_CHUNK))
        xr, xi = st_ref[h, :, re], st_ref[h, :, im]
        for t in range(SSM_TIME_TILE):
            rows = slice(t * bh, (t + 1) * bh)
            xr, xi = (ar * xr - ai * xi + bu_ref[h, rows, re],
                      ar * xi + ai * xr + bu_ref[h, rows, im])
            bu_ref[h, rows, re] = xr
            bu_ref[h, rows, im] = xi
        st_ref[h, :, re] = xr
        st_ref[h, :, im] = xi

    def readout(h, cols, acc):
        for sl in cols:
            acc = acc + jnp.dot(bu_ref[h, :, sl].astype(BF16), wc_ref[sl, :], preferred_element_type=F32)
        return acc

    def finish(h, y):
        y = y + d_ref[...] * ut_ref[h].astype(F32)
        cdf = 0.5 * (1.0 + jnp.tanh(math.sqrt(2.0 / math.pi) * (y + 0.044715 * (y * y * y))))
        g = y * cdf
        gate = jnp.dot(g.astype(BF16), wglu_ref[...], preferred_element_type=F32) + bglu_ref[...]
        out = g * _sigmoid(gate)
        ms = jnp.mean(out * out, axis=-1, keepdims=True)
        scatter(h, out * lax.rsqrt(ms + LN_EPS) * nw_ref[...])

    zero = jnp.zeros((SSM_TIME_TILE * bh, SSM_WIDTH), F32)
    gather(0)
    for k, cols in enumerate(chunks):
        project(0, cols)
        if k == 0:
            gather(1)
    for cols in chunks:
        project(1, cols)
        scan(0, cols)
    y0 = zero
    for cols in chunks:
        y0 = readout(0, cols, y0)
        scan(1, cols)
    y1 = zero
    for k, cols in enumerate(chunks):
        y1 = readout(1, cols, y1)
        if k == 0:
            finish(0, y0)
    finish(1, y1)


def _ssm(u, wb, a_re, a_im, wc, d_skip, w_glu, b_glu, norm_w):
    rows = SSM_TIME_TILE * SSM_BATCH_HALF
    full = lambda shape: pl.BlockSpec(shape, lambda i: (0, 0))
    steps = pl.BlockSpec((BATCH, SSM_TIME_TILE, SSM_WIDTH), lambda i: (0, i, 0))
    return pl.pallas_call(
        _ssm_kernel,
        grid=(SEQ // SSM_TIME_TILE,),
        in_specs=[steps,
                  full((SSM_WIDTH, 2 * SSM_STATES)), full((1, SSM_STATES)), full((1, SSM_STATES)),
                  full((2 * SSM_STATES, SSM_WIDTH)), full((1, SSM_WIDTH)),
                  full((SSM_WIDTH, SSM_WIDTH)), full((1, SSM_WIDTH)), full((1, SSM_WIDTH))],
        out_specs=steps,
        out_shape=jax.ShapeDtypeStruct((BATCH, SEQ, SSM_WIDTH), BF16),
        scratch_shapes=[pltpu.VMEM((2, rows, 2 * SSM_STATES), F32),
                        pltpu.VMEM((2, SSM_BATCH_HALF, 2 * SSM_STATES), F32),
                        pltpu.VMEM((2, rows, SSM_WIDTH), BF16),
                        pltpu.VMEM((SSM_WIDTH // LANES, BATCH * SSM_STAGE_PITCH, LANES), F32)],
        compiler_params=_params("arbitrary"),
        name="s5_mixer",
    )(u, wb, a_re, a_im, wc, d_skip, w_glu, b_glu, norm_w)


def _ssm_weights(lam_re, lam_im, b_re, b_im, c_re, c_im, log_dt):
    g, p, h = SSM_GROUPS, SSM_STATE, SSM_GROUP
    lr, li = lam_re.astype(F32), lam_im.astype(F32)
    dt = jnp.exp(log_dt.astype(F32))[:, None]
    mag = jnp.exp(lr * dt)
    a_re, a_im = mag * jnp.cos(li * dt), mag * jnp.sin(li * dt)
    den = lr * lr + li * li
    nr, ni = a_re - 1.0, a_im
    f_re = ((nr * lr + ni * li) / den)[..., None]
    f_im = ((ni * lr - nr * li) / den)[..., None]
    br, bi = b_re.astype(F32), b_im.astype(F32)
    bb_re = f_re * br - f_im * bi
    bb_im = f_re * bi + f_im * br
    eye = jnp.eye(g, dtype=F32)
    embed_b = lambda bb: jnp.einsum('gph,gk->ghkp', bb, eye).reshape(g * h, g * p)
    embed_c = lambda cc: jnp.einsum('ghp,gk->gpkh', cc, eye).reshape(g * p, g * h)
    wb = jnp.concatenate([embed_b(bb_re), embed_b(bb_im)], axis=1).astype(BF16)
    wc = jnp.concatenate([embed_c(c_re.astype(F32)), -embed_c(c_im.astype(F32))], axis=0).astype(BF16)
    return wb, a_re.reshape(1, g * p), a_im.reshape(1, g * p), wc


def _ret_kernel(q_ref, k_ref, v_ref, g_ref, nw_ref, y_ref, dmat_ref):
    c = RET_CHUNK
    dh = RET_HEAD_DIM
    hp = pl.program_id(1)
    lane = lax.broadcasted_iota(jnp.int32, (c, LANES), 1)
    row = lax.broadcasted_iota(jnp.int32, (c, LANES), 0)
    head1 = lane >= dh
    head_of_lane = (HEAD_PAIR * hp).astype(F32) + head1.astype(F32)
    lg = jnp.log(1.0 - jnp.exp2(-5.0 - head_of_lane))
    rowf = row.astype(F32)
    xi = jnp.exp((rowf + 1.0) * lg)
    zeta = jnp.exp((c - 1.0 - rowf) * lg) * (dh ** -0.5)
    g_chunk = jnp.exp(c * lg)
    block_diag = (row >= dh) == head1
    diff = (row - lane).astype(F32)
    for j in range(HEAD_PAIR):
        lg_j = jnp.log(1.0 - jnp.exp2(jnp.zeros((c, c), F32) - 5.0 - (HEAD_PAIR * hp + j).astype(F32)))
        dmat_ref[j] = jnp.where(diff >= 0, jnp.exp(jnp.maximum(diff, 0.0) * lg_j), 0.0) * (dh ** -0.5)
    nw = nw_ref[...]

    def head_mean(t):
        s0 = jnp.sum(jnp.where(head1, 0.0, t), axis=-1, keepdims=True)
        s1 = jnp.sum(jnp.where(head1, t, 0.0), axis=-1, keepdims=True)
        return jnp.where(head1, s1, s0) * (1.0 / dh)

    def chunk(n, r_prev):
        rows = pl.ds(pl.multiple_of(n * c, c), c)
        q, k, v = q_ref[rows, :], k_ref[rows, :], v_ref[rows, :]
        o = jnp.zeros((c, LANES), F32)
        for j in range(HEAD_PAIR):
            mine = head1 if j else jnp.logical_not(head1)
            qj = jnp.where(mine, q, jnp.zeros_like(q))
            s = lax.dot_general(qj, k, (((1,), (1,)), ((), ())), preferred_element_type=F32)
            s = s * dmat_ref[j]
            vj = jnp.where(mine, v, jnp.zeros_like(v))
            o = o + jnp.dot(s.astype(BF16), vj, preferred_element_type=F32)
        qx = (q.astype(F32) * xi).astype(BF16)
        o = o + jnp.dot(qx, r_prev.astype(BF16), preferred_element_type=F32)
        kz = (k.astype(F32) * zeta).T.astype(BF16)
        kv = jnp.dot(kz, v, preferred_element_type=F32)
        dlt = o - head_mean(o)
        var = head_mean(dlt * dlt)
        gate = g_ref[rows, :].astype(F32)
        y = dlt * lax.rsqrt(var + LN_EPS) * nw * (gate * _sigmoid(gate))
        y_ref[rows, :] = y.astype(BF16)
        return jnp.where(block_diag, g_chunk * r_prev + kv, 0.0)

    def chunks(it, r):
        for u in range(RET_UNROLL):
            r = chunk(it * RET_UNROLL + u, r)
        return r

    lax.fori_loop(0, SEQ // (c * RET_UNROLL), chunks, jnp.zeros((LANES, LANES), F32))


def _retention(zr, norm_w):
    col = lambda off: pl.BlockSpec((None, SEQ, LANES), lambda b, hp: (b, 0, hp + off))
    pairs = RET_WIDTH // LANES
    return pl.pallas_call(
        _ret_kernel,
        grid=(BATCH, pairs),
        in_specs=[col(0), col(pairs), col(2 * pairs), col(3 * pairs),
                  pl.BlockSpec((1, LANES), lambda b, hp: (0, hp))],
        out_specs=col(0),
        out_shape=jax.ShapeDtypeStruct((BATCH, SEQ, RET_WIDTH), BF16),
        scratch_shapes=[pltpu.VMEM((HEAD_PAIR, RET_CHUNK, RET_CHUNK), F32)],
        compiler_params=_params("parallel", "parallel"),
        name="retention",
    )(zr, zr, zr, zr, norm_w)


def _attn_kernel(q_ref, k_ref, v_ref, y_ref, src_ref, dq_ref, dk_ref, dv_ref,
                 oacc_ref, macc_ref, lacc_ref, bias_ref, bias0_ref, p_ref, s_ref):
    qb = ATT_BLOCK
    dh = ATT_HEAD_DIM
    d_home = DILATED_BRANCHES[ATT_HOME_BRANCH][1]
    hp = pl.program_id(1)
    head1 = lax.broadcasted_iota(jnp.int32, (qb, LANES), 1) >= dh

    def emit_q(bi, rows, blk):
        other = lax.broadcasted_iota(jnp.int32, blk.shape, 1) >= dh
        blk = blk * (dh ** -0.5)
        dq_ref[bi, 0, rows, :] = jnp.where(other, 0.0, blk).astype(BF16)
        dq_ref[bi, 1, rows, :] = jnp.where(other, blk, 0.0).astype(BF16)

    def emit_k(bi, rows, blk):
        if bi > 0:
            dk_ref[bi - 1, rows, :] = blk.astype(BF16)

    def emit_v(bi, rows, blk):
        if bi > 0:
            dv_ref[bi - 1, rows, :] = blk.astype(BF16)

    piece = PREP_ROWS
    for src, emit in ((q_ref, emit_q), (k_ref, emit_k), (v_ref, emit_v)):
        for c0 in range(0, SEQ, piece):
            rows = slice(c0, c0 + piece)
            natural = src[rows, :].astype(F32)
            src_ref[0, rows, :] = natural
            emit(0, rows, natural)
        d_prev = 1
        for bi in range(1, len(DILATED_BRANCHES)):
            d = DILATED_BRANCHES[bi][1]
            step = d // d_prev
            sub_prev, sub = SEQ // d_prev, SEQ // d
            cur, nxt = (bi - 1) % 2, bi % 2
            for r_prev in range(d_prev):
                for t in range(step):
                    r = r_prev + d_prev * t
                    for c0 in range(0, sub, piece):
                        n = min(piece, sub - c0)
                        rows = slice(r * sub + c0, r * sub + c0 + n)
                        blk = src_ref.at[cur][pl.ds(r_prev * sub_prev + t + step * c0, n, stride=step), :]
                        emit(bi, rows, blk)
                        if bi + 1 < len(DILATED_BRANCHES):
                            src_ref[nxt, rows, :] = blk
            d_prev = d

    row2 = lax.broadcasted_iota(jnp.int32, (HEAD_PAIR * qb, 2 * qb), 0)
    key2 = lax.broadcasted_iota(jnp.int32, (HEAD_PAIR * qb, 2 * qb), 1)
    row1 = lax.broadcasted_iota(jnp.int32, (HEAD_PAIR * qb, qb), 0)
    key1 = lax.broadcasted_iota(jnp.int32, (HEAD_PAIR * qb, qb), 1)

    def head_slope(row):
        head = (HEAD_PAIR * hp).astype(F32) + (row >= qb).astype(F32)
        return jnp.exp2(-(8.0 / ATT_HEADS) * (head + 1.0))

    dist2 = qb + jnp.where(row2 >= qb, row2 - qb, row2) - key2
    dist1 = jnp.where(row1 >= qb, row1 - qb, row1) - key1
    slope2, slope1 = head_slope(row2), head_slope(row1)

    def aligned(x):
        return x if isinstance(x, int) else pl.multiple_of(x, qb)

    for bi, (window, d) in enumerate(DILATED_BRANCHES):
        sub = SEQ // d
        nb = sub // qb
        ksrc = k_ref if bi == 0 else dk_ref.at[bi - 1]
        vsrc = v_ref if bi == 0 else dv_ref.at[bi - 1]
        bias_ref[...] = jnp.where((dist2 >= 0) & (dist2 <= window // d),
                                  -slope2 * (d * dist2).astype(F32), MASKED_SCORE)
        bias0_ref[...] = jnp.where(dist1 >= 0, -slope1 * (d * dist1).astype(F32), MASKED_SCORE)

        def rows_of(blk, d=d):
            base, first_token, has_prev, _ = blk
            rows = pl.ds(aligned(base), qb)
            krows = pl.ds(aligned(base - qb), 2 * qb) if has_prev else rows
            if d == 1 or d == d_home:
                out_rows = rows
            else:
                ft = first_token
                home = (ft % d_home) * (SEQ // d_home) + ft // d_home if isinstance(ft, int) else (
                    lax.rem(ft, d_home) * (SEQ // d_home) + lax.div(ft, d_home))
                out_rows = pl.ds(home, qb, stride=d // d_home)
            return rows, krows, out_rows, (2 * qb if has_prev else qb)

        def scores(blk, bi=bi, ksrc=ksrc, rows_of=rows_of):
            rows, krows, out_rows, nk = rows_of(blk)
            q2 = jnp.concatenate([dq_ref[bi, j, rows, :] for j in range(HEAD_PAIR)], axis=0)
            s_ref[blk[3], :, 0:nk] = lax.dot_general(
                q2, ksrc[krows, :], (((1,), (1,)), ((), ())), preferred_element_type=F32)

        def softmax(blk, bi=bi, rows_of=rows_of):
            rows, krows, out_rows, nk = rows_of(blk)
            s = s_ref[blk[3], :, 0:nk] + (bias_ref[...] if blk[2] else bias0_ref[...])
            m = jnp.max(s, axis=-1, keepdims=True)
            e = jnp.exp(s - m)
            l = jnp.sum(e, axis=-1, keepdims=True)
            p_ref[blk[3], :, 0:nk] = e.astype(BF16)
            packed = lambda c: jnp.where(head1, jnp.broadcast_to(c[qb:], (qb, LANES)),
                                         jnp.broadcast_to(c[:qb], (qb, LANES)))
            macc_ref.at[bi][out_rows, :] = packed(m)
            lacc_ref.at[bi][out_rows, :] = packed(l)

        def values(blk, bi=bi, vsrc=vsrc, rows_of=rows_of):
            rows, krows, out_rows, nk = rows_of(blk)
            v = vsrc[krows, :]
            o = [jnp.dot(p_ref[blk[3], j * qb:(j + 1) * qb, 0:nk], v, preferred_element_type=F32)
                 for j in range(HEAD_PAIR)]
            oacc_ref.at[bi][out_rows, :] = jnp.where(head1, o[1], o[0])

        def pipelined(n_groups, group, stages=(scores, softmax, values)):
            def step(t, valid):
                for k in reversed(range(len(stages))):
                    if valid(t - k):
                        for blk in group(t - k):
                            stages[k](blk)

            depth = len(stages) - 1
            head_steps = min(depth, n_groups)
            for t in range(head_steps):
                step(t, lambda g: 0 <= g < n_groups)
            if n_groups > depth:
                def body(t, carry):
                    step(t, lambda g: True)
                    return carry

                lax.fori_loop(depth, n_groups, body, 0)
            for t in range(max(n_groups, head_steps), n_groups + depth):
                step(t, lambda g: 0 <= g < n_groups)

        fu = min(d, ATT_FIRST_UNROLL)
        pipelined(d // fu, lambda it, fu=fu, sub=sub: [
            ((it * fu + u) * sub, it * fu + u, False, it * fu + u) for u in range(fu)])

        if nb > 1:
            lu = ATT_LATER_UNROLL
            per_res = (nb - 1) // lu
            assert per_res * lu == nb - 1

            def later_group(it, lu=lu, per_res=per_res, sub=sub, d=d):
                if per_res == 1:
                    r, n0 = it, 1
                elif d == 1:
                    r, n0 = 0, it * lu + 1
                elif isinstance(it, int):
                    r, n0 = it // per_res, (it % per_res) * lu + 1
                else:
                    r, n0 = lax.div(it, per_res), lax.rem(it, per_res) * lu + 1
                return [(r * sub + (n0 + u) * qb, r + d * qb * (n0 + u), True, d + it * lu + u)
                        for u in range(lu)]

            pipelined(d * per_res, later_group)

    tiles_per_residue = SEQ // d_home // COMBINE_ROWS

    def combine(tb, carry):
        rows = pl.ds(pl.multiple_of(tb * COMBINE_ROWS, COMBINE_ROWS), COMBINE_ROWS)
        tokens = pl.ds(lax.div(tb, tiles_per_residue) + d_home * COMBINE_ROWS * lax.rem(tb, tiles_per_residue),
                       COMBINE_ROWS, stride=d_home)
        at = lambda ref, bi: ref.at[bi][tokens if DILATED_BRANCHES[bi][1] == 1 else rows, :]
        ms = [at(macc_ref, bi) for bi in range(len(DILATED_BRANCHES))]
        top = functools.reduce(jnp.maximum, ms)
        ws = [jnp.exp(m - top) for m in ms]
        num = sum(w * at(oacc_ref, bi) for bi, w in enumerate(ws))
        den = sum(w * at(lacc_ref, bi) for bi, w in enumerate(ws))
        src_ref.at[0][tokens, :] = num / den
        return carry

    lax.fori_loop(0, SEQ // COMBINE_ROWS, combine, 0)
    for c0 in range(0, SEQ, COMBINE_ROWS):
        y_ref[c0:c0 + COMBINE_ROWS, :] = src_ref[0, c0:c0 + COMBINE_ROWS, :].astype(BF16)


def _attention(za):
    pairs = ATT_WIDTH // LANES
    col = lambda off: pl.BlockSpec((None, SEQ, LANES), lambda b, hp: (b, 0, hp + off))
    n_dil = len(DILATED_BRANCHES) - 1
    return pl.pallas_call(
        _attn_kernel,
        grid=(BATCH, pairs),
        in_specs=[col(0), col(pairs), col(2 * pairs)],
        out_specs=col(0),
        out_shape=jax.ShapeDtypeStruct((BATCH, SEQ, ATT_WIDTH), BF16),
        scratch_shapes=[pltpu.VMEM((2, SEQ, LANES), F32),
                        pltpu.VMEM((n_dil + 1, HEAD_PAIR, SEQ, LANES), BF16),
                        pltpu.VMEM((n_dil, SEQ, LANES), BF16),
                        pltpu.VMEM((n_dil, SEQ, LANES), BF16),
                        pltpu.VMEM((n_dil + 1, SEQ, LANES), F32),
                        pltpu.VMEM((n_dil + 1, SEQ, LANES), F32),
                        pltpu.VMEM((n_dil + 1, SEQ, LANES), F32),
                        pltpu.VMEM((HEAD_PAIR * ATT_BLOCK, 2 * ATT_BLOCK), F32),
                        pltpu.VMEM((HEAD_PAIR * ATT_BLOCK, ATT_BLOCK), F32),
                        pltpu.VMEM((SEQ // ATT_BLOCK, HEAD_PAIR * ATT_BLOCK, 2 * ATT_BLOCK), BF16),
                        pltpu.VMEM((SEQ // ATT_BLOCK, HEAD_PAIR * ATT_BLOCK, 2 * ATT_BLOCK), F32)],
        compiler_params=_params("parallel", "parallel"),
        name="dilated_attention",
    )(za, za, za)


def _post_kernel(ys_ref, yr_ref, ya_ref, x_ref, wo_ref, an_ref, l1w_ref, l1b_ref,
                 w1_ref, w2_ref, l2w_ref, l2b_ref, o_ref, x1_ref, acc_ref):
    half = POST_TILE // 2
    n_chunks = D_FF // FF_CHUNK
    slab = half // n_chunks
    halves = (slice(0, half), slice(half, POST_TILE))

    def out_proj(rows):
        ya = ya_ref[rows, :].astype(F32)
        ms = jnp.mean(ya * ya, axis=-1, keepdims=True)
        ya = (ya * lax.rsqrt(ms + LN_EPS) * an_ref[...]).astype(BF16)
        h = jnp.dot(ys_ref[rows, :], wo_ref[0:SSM_WIDTH, :], preferred_element_type=F32)
        h = h + jnp.dot(yr_ref[rows, :], wo_ref[SSM_WIDTH:SSM_WIDTH + RET_WIDTH, :],
                        preferred_element_type=F32)
        h = h + jnp.dot(ya, wo_ref[SSM_WIDTH + RET_WIDTH:, :], preferred_element_type=F32)
        o_ref[rows, :] = DEEPNORM_ALPHA * x_ref[rows, :] + h

    def ln1(rows):
        x1_ref[rows, :] = _layer_norm(o_ref[rows, :], l1w_ref[...], l1b_ref[...])

    def ln2(rows):
        o_ref[rows, :] = _layer_norm(DEEPNORM_ALPHA * x1_ref[rows, :] + acc_ref[rows, :],
                                     l2w_ref[...], l2b_ref[...])

    def mlp(rows, between):
        xb = x1_ref[rows, :].astype(BF16)
        acc = jnp.zeros((half, D_MODEL), F32)
        for k in range(n_chunks):
            c = k * FF_CHUNK
            hid = jnp.dot(xb, w1_ref[:, c:c + FF_CHUNK], preferred_element_type=F32)
            hid = jnp.square(jnp.maximum(hid, 0.0)).astype(BF16)
            acc = acc + jnp.dot(hid, w2_ref[c:c + FF_CHUNK, :], preferred_element_type=F32)
            between(k)
        acc_ref[rows, :] = acc

    slab_of = lambda rows, k: slice(rows.start + k * slab, rows.start + (k + 1) * slab)
    out_proj(halves[0])
    out_proj(halves[1])
    ln1(halves[0])
    mlp(halves[0], lambda k: ln1(slab_of(halves[1], k)))
    mlp(halves[1], lambda k: ln2(slab_of(halves[0], k)))
    ln2(halves[1])


def _post(y_ssm, y_ret, y_att, x, w_out, attn_norm, ln1_w, ln1_b, w1, w2, ln2_w, ln2_b):
    tm = POST_TILE
    row = lambda width: pl.BlockSpec((None, tm, width), lambda b, i: (b, i, 0))
    full = lambda shape: pl.BlockSpec(shape, lambda b, i: (0, 0), pipeline_mode=pl.Buffered(1))
    return pl.pallas_call(
        _post_kernel,
        grid=(BATCH, SEQ // tm),
        in_specs=[row(SSM_WIDTH), row(RET_WIDTH), row(ATT_WIDTH), row(D_MODEL),
                  full((D_MODEL, D_MODEL)), full((1, ATT_WIDTH)), full((1, D_MODEL)), full((1, D_MODEL)),
                  full((D_MODEL, D_FF)), full((D_FF, D_MODEL)), full((1, D_MODEL)), full((1, D_MODEL))],
        out_specs=row(D_MODEL),
        out_shape=jax.ShapeDtypeStruct((BATCH, SEQ, D_MODEL), F32),
        scratch_shapes=[pltpu.VMEM((tm, D_MODEL), F32), pltpu.VMEM((tm, D_MODEL), F32)],
        compiler_params=_params("parallel", "parallel"),
        name="out_proj_mlp",
    )(y_ssm, y_ret, y_att, x, w_out, attn_norm, ln1_w, ln1_b, w1, w2, ln2_w, ln2_b)


def kernel(x, w_in, ssm_lambda_re, ssm_lambda_im, ssm_b_re, ssm_b_im, ssm_c_re, ssm_c_im, ssm_d, ssm_log_dt, ssm_w_glu, ssm_b_glu, ssm_out_norm, ret_out_norm, attn_out_norm, w_out, ln1_w, ln1_b, mlp_w1, mlp_w2, ln2_w, ln2_b):
    vec = lambda p: p.astype(F32).reshape(1, -1)
    x = x.astype(F32)
    for i in range(DEPTH):
        u, zr, za = _in_proj(x, w_in[i].astype(BF16))
        wb, a_re, a_im, wc = _ssm_weights(ssm_lambda_re[i], ssm_lambda_im[i], ssm_b_re[i], ssm_b_im[i],
                                          ssm_c_re[i], ssm_c_im[i], ssm_log_dt[i])
        y_ssm = _ssm(u, wb, a_re, a_im, wc, vec(ssm_d[i]), ssm_w_glu[i].astype(BF16),
                     vec(ssm_b_glu[i]), vec(ssm_out_norm[i]))
        y_ret = _retention(zr, vec(ret_out_norm[i]))
        y_att = _attention(za)
        x = _post(y_ssm, y_ret, y_att, x, w_out[i].astype(BF16), vec(attn_out_norm[i]),
                  vec(ln1_w[i]), vec(ln1_b[i]), mlp_w1[i].astype(BF16), mlp_w2[i].astype(BF16),
                  vec(ln2_w[i]), vec(ln2_b[i]))
    return x
```

```python
import functools
import math

import jax
import jax.numpy as jnp
from jax import lax
from jax.experimental import pallas as pl
from jax.experimental.pallas import tpu as pltpu

F32 = jnp.float32
BF16 = jnp.bfloat16

D_MODEL = 1024
BATCH = 16
SEQ = 2048
DEPTH = 2
SSM_WIDTH = 256
SSM_GROUP = 16
SSM_GROUPS = 16
SSM_STATE = 64
SSM_STATES = SSM_GROUPS * SSM_STATE
RET_HEAD_DIM = 64
RET_WIDTH = 256
RET_CHUNK = 128
ATT_HEAD_DIM = 64
ATT_WIDTH = 512
ATT_HEADS = 8
DILATED_BRANCHES = ((128, 1), (512, 4), (2048, 16))
ATT_BLOCK = 128
IN_WIDTH = SSM_WIDTH + 4 * RET_WIDTH + 3 * ATT_WIDTH
D_FF = 4 * D_MODEL
DEEPNORM_ALPHA = (2 * DEPTH) ** 0.25
LN_EPS = 1e-5

LANES = 128
HEAD_PAIR = LANES // ATT_HEAD_DIM
VMEM_LIMIT_BYTES = 56 * 1024 * 1024
MASKED_SCORE = -1e30

TOKEN_TILE = 512
IN_PROJ_COLS = 256
POST_TILE = 1024
POST_PARTS = 4
SSM_TIME_TILE = 64
SSM_BATCH_HALF = BATCH // 2
SSM_COL_CHUNK = 512
SSM_STAGE_PITCH = SSM_TIME_TILE + 8
FF_CHUNK = 1024
COMBINE_ROWS = 256
ATT_HOME_BRANCH = 1
PREP_ROWS = 128
ATT_FIRST_UNROLL = 4
ATT_LATER_UNROLL = 3

assert RET_HEAD_DIM == ATT_HEAD_DIM and HEAD_PAIR == 2
assert all(w // d == ATT_BLOCK for w, d in DILATED_BRANCHES)
assert DILATED_BRANCHES[0][1] == 1 and all(
    b[1] % a[1] == 0 for a, b in zip(DILATED_BRANCHES, DILATED_BRANCHES[1:]))


def _params(*semantics):
    return pltpu.CompilerParams(dimension_semantics=semantics, vmem_limit_bytes=VMEM_LIMIT_BYTES)


def _sigmoid(x):
    return 1.0 / (1.0 + jnp.exp(-x))


def _layer_norm(r, w, b):
    mu = jnp.mean(r, axis=-1, keepdims=True)
    d = r - mu
    var = jnp.mean(d * d, axis=-1, keepdims=True)
    return d * lax.rsqrt(var + LN_EPS) * w + b


def _in_proj_ret_kernel(x_ref, w_ref, nw_ref, u_ref, za_ref, y_ref, zr_ref, znew_ref, r_ref, const_ref):
    c = RET_CHUNK
    dh = RET_HEAD_DIM
    pairs = RET_WIDTH // LANES
    tm = TOKEN_TILE
    step = pl.program_id(0)
    tiles_per_batch = SEQ // tm
    lane = lax.broadcasted_iota(jnp.int32, (c, LANES), 1)
    row = lax.broadcasted_iota(jnp.int32, (c, LANES), 0)
    head1 = lane >= dh
    block_diag = (row >= dh) == head1
    XI, ZETA, G_CHUNK, DMAT = 0, 1, 2, 3

    @pl.when(step == 0)
    def _():
        zr_ref[...] = jnp.zeros_like(zr_ref)
        rowf = row.astype(F32)
        diff = (row - lane).astype(F32)
        for p in range(pairs):
            lg = jnp.log(1.0 - jnp.exp2(-5.0 - (HEAD_PAIR * p + head1.astype(F32))))
            const_ref[p, XI] = jnp.exp((rowf + 1.0) * lg)
            const_ref[p, ZETA] = jnp.exp((c - 1.0 - rowf) * lg) * (dh ** -0.5)
            const_ref[p, G_CHUNK] = jnp.exp(c * lg)
            for j in range(HEAD_PAIR):
                lg_j = math.log(1.0 - 2.0 ** (-5.0 - (HEAD_PAIR * p + j)))
                const_ref[p, DMAT + j] = jnp.where(
                    diff >= 0, jnp.exp(jnp.maximum(diff, 0.0) * lg_j), 0.0) * (dh ** -0.5)

    @pl.when(lax.rem(jnp.maximum(step - 1, 0), tiles_per_batch) == 0)
    def _():
        r_ref[...] = jnp.zeros_like(r_ref)

    def head_mean(t):
        s0 = jnp.sum(jnp.where(head1, 0.0, t), axis=-1, keepdims=True)
        s1 = jnp.sum(jnp.where(head1, t, 0.0), axis=-1, keepdims=True)
        return jnp.where(head1, s1, s0) * (1.0 / dh)

    def retention_chunk(p, n):
        rows = slice(n * c, (n + 1) * c)
        col = lambda part: slice(part * RET_WIDTH + p * LANES, part * RET_WIDTH + (p + 1) * LANES)
        q, k, v = (zr_ref[rows, col(part)] for part in range(3))
        r_prev = r_ref[p]
        o = jnp.zeros((c, LANES), F32)
        for j in range(HEAD_PAIR):
            mine = head1 if j else jnp.logical_not(head1)
            qj = jnp.where(mine, q, jnp.zeros_like(q))
            s = lax.dot_general(qj, k, (((1,), (1,)), ((), ())), preferred_element_type=F32)
            s = s * const_ref[p, DMAT + j]
            vj = jnp.where(mine, v, jnp.zeros_like(v))
            o = o + jnp.dot(s.astype(BF16), vj, preferred_element_type=F32)
        qx = (q.astype(F32) * const_ref[p, XI]).astype(BF16)
        o = o + jnp.dot(qx, r_prev.astype(BF16), preferred_element_type=F32)
        kz = (k.astype(F32) * const_ref[p, ZETA]).T.astype(BF16)
        kv = jnp.dot(kz, v, preferred_element_type=F32)
        r_ref[p] = jnp.where(block_diag, const_ref[p, G_CHUNK] * r_prev + kv, 0.0)
        dlt = o - head_mean(o)
        var = head_mean(dlt * dlt)
        gate = zr_ref[rows, col(3)].astype(F32)
        y = dlt * lax.rsqrt(var + LN_EPS) * nw_ref[:, p * LANES:(p + 1) * LANES] * (gate * _sigmoid(gate))
        y_ref[n * c:(n + 1) * c, p * LANES:(p + 1) * LANES] = y.astype(BF16)

    xb = x_ref[...].astype(BF16)

    def project(lo):
        z = jnp.dot(xb, w_ref[:, lo:lo + IN_PROJ_COLS], preferred_element_type=F32).astype(BF16)
        hi = lo + IN_PROJ_COLS
        if hi <= SSM_WIDTH:
            u_ref[:, lo:hi] = z
        elif hi <= SSM_WIDTH + 4 * RET_WIDTH:
            znew_ref[:, lo - SSM_WIDTH:hi - SSM_WIDTH] = z
        else:
            za_ref[:, lo - SSM_WIDTH - 4 * RET_WIDTH:hi - SSM_WIDTH - 4 * RET_WIDTH] = z

    col_blocks = list(range(0, IN_WIDTH, IN_PROJ_COLS))
    units = [(p, n) for n in range(tm // c) for p in range(pairs)]
    for k in range(max(len(col_blocks), len(units))):
        if k < len(col_blocks):
            project(col_blocks[k])
        if k < len(units):
            retention_chunk(*units[k])
    zr_ref[...] = znew_ref[...]


def _in_proj_ret(x, w_in, ret_norm):
    tm = TOKEN_TILE
    per_batch = SEQ // tm
    n_tiles = BATCH * per_batch

    def tile(width, index):
        return pl.BlockSpec((None, tm, width),
                            lambda s: (lax.div(index(s), per_batch), lax.rem(index(s), per_batch), 0))

    cur = lambda width: tile(width, lambda s: jnp.minimum(s, n_tiles - 1))
    lag = lambda width: tile(width, lambda s: jnp.maximum(s - 1, 0))
    return pl.pallas_call(
        _in_proj_ret_kernel,
        grid=(n_tiles + 1,),
        in_specs=[cur(D_MODEL),
                  pl.BlockSpec((D_MODEL, IN_WIDTH), lambda s: (0, 0), pipeline_mode=pl.Buffered(1)),
                  pl.BlockSpec((1, RET_WIDTH), lambda s: (0, 0))],
        out_specs=[cur(SSM_WIDTH), cur(3 * ATT_WIDTH), lag(RET_WIDTH)],
        out_shape=[jax.ShapeDtypeStruct((BATCH, SEQ, SSM_WIDTH), BF16),
                   jax.ShapeDtypeStruct((BATCH, SEQ, 3 * ATT_WIDTH), BF16),
                   jax.ShapeDtypeStruct((BATCH, SEQ, RET_WIDTH), BF16)],
        scratch_shapes=[pltpu.VMEM((tm, 4 * RET_WIDTH), BF16),
                        pltpu.VMEM((tm, 4 * RET_WIDTH), BF16),
                        pltpu.VMEM((RET_WIDTH // LANES, LANES, LANES), F32),
                        pltpu.VMEM((RET_WIDTH // LANES, 3 + HEAD_PAIR, RET_CHUNK, LANES), F32)],
        compiler_params=_params("arbitrary"),
        name="in_proj_retention",
    )(x, w_in, ret_norm)


def _ssm_kernel(u_ref, wb_ref, are_ref, aim_ref, wc_ref, d_ref, wglu_ref, bglu_ref, nw_ref,
                y_ref, bu_ref, st_ref, ut_ref, stage_ref):
    n_st = SSM_STATES
    bh = SSM_BATCH_HALF
    lt = SSM_TIME_TILE
    pitch = SSM_STAGE_PITCH
    lane_slabs = SSM_WIDTH // LANES
    chunks = [(slice(c, c + SSM_COL_CHUNK), slice(n_st + c, n_st + c + SSM_COL_CHUNK))
              for c in range(0, n_st, SSM_COL_CHUNK)]

    @pl.when(pl.program_id(0) == 0)
    def _():
        st_ref[...] = jnp.zeros_like(st_ref)

    def stage_rows(h, t):
        return pl.ds(h * bh * pitch + t, bh, stride=pitch)

    def gather(h):
        for b in range(h * bh, (h + 1) * bh):
            for s in range(lane_slabs):
                stage_ref[s, b * pitch:b * pitch + lt, :] = (
                    u_ref[b, :, s * LANES:(s + 1) * LANES].astype(F32))
        for t in range(0, lt, 2):
            for s in range(lane_slabs):
                pair = jnp.concatenate([stage_ref.at[s][stage_rows(h, t + k), :] for k in range(2)], axis=0)
                ut_ref[h, t * bh:(t + 2) * bh, s * LANES:(s + 1) * LANES] = pair.astype(BF16)

    def scatter(h, y):
        for t in range(lt):
            for s in range(lane_slabs):
                stage_ref.at[s][stage_rows(h, t), :] = y[t * bh:(t + 1) * bh, s * LANES:(s + 1) * LANES]
        for b in range(h * bh, (h + 1) * bh):
            for s in range(lane_slabs):
                y_ref[b, :, s * LANES:(s + 1) * LANES] = (
                    stage_ref[s, b * pitch:b * pitch + lt, :].astype(BF16))

    def project(h, cols):
        for sl in cols:
            bu_ref[h, :, sl] = jnp.dot(ut_ref[h], wb_ref[:, sl], preferred_element_type=F32)

    def scan(h, cols):
        re, im = cols
        ar = jnp.broadcast_to(are_ref[:, re], (bh, SSM_COL_CHUNK))
        ai = jnp.broadcast_to(aim_ref[:, re], (bh, SSM_COL_CHUNK))
        xr, xi = st_ref[h, :, re], st_ref[h, :, im]
        for t in range(SSM_TIME_TILE):
            rows = slice(t * bh, (t + 1) * bh)
            xr, xi = (ar * xr - ai * xi + bu_ref[h, rows, re],
                      ar * xi + ai * xr + bu_ref[h, rows, im])
            bu_ref[h, rows, re] = xr
            bu_ref[h, rows, im] = xi
        st_ref[h, :, re] = xr
        st_ref[h, :, im] = xi

    def readout(h, cols, acc):
        for sl in cols:
            acc = acc + jnp.dot(bu_ref[h, :, sl].astype(BF16), wc_ref[sl, :], preferred_element_type=F32)
        return acc

    def finish(h, y):
        y = y + d_ref[...] * ut_ref[h].astype(F32)
        cdf = 0.5 * (1.0 + jnp.tanh(math.sqrt(2.0 / math.pi) * (y + 0.044715 * (y * y * y))))
        g = y * cdf
        gate = jnp.dot(g.astype(BF16), wglu_ref[...], preferred_element_type=F32) + bglu_ref[...]
        out = g * _sigmoid(gate)
        ms = jnp.mean(out * out, axis=-1, keepdims=True)
        scatter(h, out * lax.rsqrt(ms + LN_EPS) * nw_ref[...])

    zero = jnp.zeros((SSM_TIME_TILE * bh, SSM_WIDTH), F32)
    gather(0)
    for k, cols in enumerate(chunks):
        project(0, cols)
        if k == 0:
            gather(1)
    for cols in chunks:
        project(1, cols)
        scan(0, cols)
    y0 = zero
    for cols in chunks:
        y0 = readout(0, cols, y0)
        scan(1, cols)
    y1 = zero
    for k, cols in enumerate(chunks):
        y1 = readout(1, cols, y1)
        if k == 0:
            finish(0, y0)
    finish(1, y1)


def _ssm(u, wb, a_re, a_im, wc, d_skip, w_glu, b_glu, norm_w):
    rows = SSM_TIME_TILE * SSM_BATCH_HALF
    full = lambda shape: pl.BlockSpec(shape, lambda i: (0, 0))
    steps = pl.BlockSpec((BATCH, SSM_TIME_TILE, SSM_WIDTH), lambda i: (0, i, 0))
    return pl.pallas_call(
        _ssm_kernel,
        grid=(SEQ // SSM_TIME_TILE,),
        in_specs=[steps,
                  full((SSM_WIDTH, 2 * SSM_STATES)), full((1, SSM_STATES)), full((1, SSM_STATES)),
                  full((2 * SSM_STATES, SSM_WIDTH)), full((1, SSM_WIDTH)),
                  full((SSM_WIDTH, SSM_WIDTH)), full((1, SSM_WIDTH)), full((1, SSM_WIDTH))],
        out_specs=steps,
        out_shape=jax.ShapeDtypeStruct((BATCH, SEQ, SSM_WIDTH), BF16),
        scratch_shapes=[pltpu.VMEM((2, rows, 2 * SSM_STATES), F32),
                        pltpu.VMEM((2, SSM_BATCH_HALF, 2 * SSM_STATES), F32),
                        pltpu.VMEM((2, rows, SSM_WIDTH), BF16),
                        pltpu.VMEM((SSM_WIDTH // LANES, BATCH * SSM_STAGE_PITCH, LANES), F32)],
        compiler_params=_params("arbitrary"),
        name="s5_mixer",
    )(u, wb, a_re, a_im, wc, d_skip, w_glu, b_glu, norm_w)


def _ssm_weights(lam_re, lam_im, b_re, b_im, c_re, c_im, log_dt):
    g, p, h = SSM_GROUPS, SSM_STATE, SSM_GROUP
    lr, li = lam_re.astype(F32), lam_im.astype(F32)
    dt = jnp.exp(log_dt.astype(F32))[:, None]
    mag = jnp.exp(lr * dt)
    a_re, a_im = mag * jnp.cos(li * dt), mag * jnp.sin(li * dt)
    den = lr * lr + li * li
    nr, ni = a_re - 1.0, a_im
    f_re = ((nr * lr + ni * li) / den)[..., None]
    f_im = ((ni * lr - nr * li) / den)[..., None]
    br, bi = b_re.astype(F32), b_im.astype(F32)
    bb_re = f_re * br - f_im * bi
    bb_im = f_re * bi + f_im * br
    eye = jnp.eye(g, dtype=F32)
    embed_b = lambda bb: jnp.einsum('gph,gk->ghkp', bb, eye).reshape(g * h, g * p)
    embed_c = lambda cc: jnp.einsum('ghp,gk->gpkh', cc, eye).reshape(g * p, g * h)
    wb = jnp.concatenate([embed_b(bb_re), embed_b(bb_im)], axis=1).astype(BF16)
    wc = jnp.concatenate([embed_c(c_re.astype(F32)), -embed_c(c_im.astype(F32))], axis=0).astype(BF16)
    return wb, a_re.reshape(1, g * p), a_im.reshape(1, g * p), wc


def _attn_kernel(q_ref, k_ref, v_ref, y_ref, src_ref, dq_ref, dk_ref, dv_ref,
                 oacc_ref, macc_ref, lacc_ref, bias_ref, bias0_ref, p_ref, s_ref):
    qb = ATT_BLOCK
    dh = ATT_HEAD_DIM
    d_home = DILATED_BRANCHES[ATT_HOME_BRANCH][1]
    hp = pl.program_id(1)
    head1 = lax.broadcasted_iota(jnp.int32, (qb, LANES), 1) >= dh

    def emit_q(bi, rows, blk):
        other = lax.broadcasted_iota(jnp.int32, blk.shape, 1) >= dh
        blk = blk * (dh ** -0.5)
        dq_ref[bi, 0, rows, :] = jnp.where(other, 0.0, blk).astype(BF16)
        dq_ref[bi, 1, rows, :] = jnp.where(other, blk, 0.0).astype(BF16)

    def emit_k(bi, rows, blk):
        if bi > 0:
            dk_ref[bi - 1, rows, :] = blk.astype(BF16)

    def emit_v(bi, rows, blk):
        if bi > 0:
            dv_ref[bi - 1, rows, :] = blk.astype(BF16)

    piece = PREP_ROWS
    for src, emit in ((q_ref, emit_q), (k_ref, emit_k), (v_ref, emit_v)):
        for c0 in range(0, SEQ, piece):
            rows = slice(c0, c0 + piece)
            natural = src[rows, :].astype(F32)
            src_ref[0, rows, :] = natural
            emit(0, rows, natural)
        d_prev = 1
        for bi in range(1, len(DILATED_BRANCHES)):
            d = DILATED_BRANCHES[bi][1]
            step = d // d_prev
            sub_prev, sub = SEQ // d_prev, SEQ // d
            cur, nxt = (bi - 1) % 2, bi % 2
            for r_prev in range(d_prev):
                for t in range(step):
                    r = r_prev + d_prev * t
                    for c0 in range(0, sub, piece):
                        n = min(piece, sub - c0)
                        rows = slice(r * sub + c0, r * sub + c0 + n)
                        blk = src_ref.at[cur][pl.ds(r_prev * sub_prev + t + step * c0, n, stride=step), :]
                        emit(bi, rows, blk)
                        if bi + 1 < len(DILATED_BRANCHES):
                            src_ref[nxt, rows, :] = blk
            d_prev = d

    row2 = lax.broadcasted_iota(jnp.int32, (HEAD_PAIR * qb, 2 * qb), 0)
    key2 = lax.broadcasted_iota(jnp.int32, (HEAD_PAIR * qb, 2 * qb), 1)
    row1 = lax.broadcasted_iota(jnp.int32, (HEAD_PAIR * qb, qb), 0)
    key1 = lax.broadcasted_iota(jnp.int32, (HEAD_PAIR * qb, qb), 1)

    def head_slope(row):
        head = (HEAD_PAIR * hp).astype(F32) + (row >= qb).astype(F32)
        return jnp.exp2(-(8.0 / ATT_HEADS) * (head + 1.0))

    dist2 = qb + jnp.where(row2 >= qb, row2 - qb, row2) - key2
    dist1 = jnp.where(row1 >= qb, row1 - qb, row1) - key1
    slope2, slope1 = head_slope(row2), head_slope(row1)

    def aligned(x):
        return x if isinstance(x, int) else pl.multiple_of(x, qb)

    for bi, (window, d) in enumerate(DILATED_BRANCHES):
        sub = SEQ // d
        nb = sub // qb
        ksrc = k_ref if bi == 0 else dk_ref.at[bi - 1]
        vsrc = v_ref if bi == 0 else dv_ref.at[bi - 1]
        bias_ref[...] = jnp.where((dist2 >= 0) & (dist2 <= window // d),
                                  -slope2 * (d * dist2).astype(F32), MASKED_SCORE)
        bias0_ref[...] = jnp.where(dist1 >= 0, -slope1 * (d * dist1).astype(F32), MASKED_SCORE)

        def rows_of(blk, d=d):
            base, first_token, has_prev, _ = blk
            rows = pl.ds(aligned(base), qb)
            krows = pl.ds(aligned(base - qb), 2 * qb) if has_prev else rows
            if d == 1 or d == d_home:
                out_rows = rows
            else:
                ft = first_token
                home = (ft % d_home) * (SEQ // d_home) + ft // d_home if isinstance(ft, int) else (
                    lax.rem(ft, d_home) * (SEQ // d_home) + lax.div(ft, d_home))
                out_rows = pl.ds(home, qb, stride=d // d_home)
            return rows, krows, out_rows, (2 * qb if has_prev else qb)

        def scores(blk, bi=bi, ksrc=ksrc, rows_of=rows_of):
            rows, krows, out_rows, nk = rows_of(blk)
            q2 = jnp.concatenate([dq_ref[bi, j, rows, :] for j in range(HEAD_PAIR)], axis=0)
            s_ref[blk[3], :, 0:nk] = lax.dot_general(
                q2, ksrc[krows, :], (((1,), (1,)), ((), ())), preferred_element_type=F32)

        def softmax(blk, bi=bi, rows_of=rows_of):
            rows, krows, out_rows, nk = rows_of(blk)
            s = s_ref[blk[3], :, 0:nk] + (bias_ref[...] if blk[2] else bias0_ref[...])
            m = jnp.max(s, axis=-1, keepdims=True)
            e = jnp.exp(s - m)
            l = jnp.sum(e, axis=-1, keepdims=True)
            p_ref[blk[3], :, 0:nk] = e.astype(BF16)
            packed = lambda c: jnp.where(head1, jnp.broadcast_to(c[qb:], (qb, LANES)),
                                         jnp.broadcast_to(c[:qb], (qb, LANES)))
            macc_ref.at[bi][out_rows, :] = packed(m)
            lacc_ref.at[bi][out_rows, :] = packed(l)

        def values(blk, bi=bi, vsrc=vsrc, rows_of=rows_of):
            rows, krows, out_rows, nk = rows_of(blk)
            v = vsrc[krows, :]
            o = [jnp.dot(p_ref[blk[3], j * qb:(j + 1) * qb, 0:nk], v, preferred_element_type=F32)
                 for j in range(HEAD_PAIR)]
            oacc_ref.at[bi][out_rows, :] = jnp.where(head1, o[1], o[0])

        def pipelined(n_groups, group, stages=(scores, softmax, values)):
            def step(t, valid):
                for k in reversed(range(len(stages))):
                    if valid(t - k):
                        for blk in group(t - k):
                            stages[k](blk)

            depth = len(stages) - 1
            head_steps = min(depth, n_groups)
            for t in range(head_steps):
                step(t, lambda g: 0 <= g < n_groups)
            if n_groups > depth:
                def body(t, carry):
                    step(t, lambda g: True)
                    return carry

                lax.fori_loop(depth, n_groups, body, 0)
            for t in range(max(n_groups, head_steps), n_groups + depth):
                step(t, lambda g: 0 <= g < n_groups)

        fu = min(d, ATT_FIRST_UNROLL)
        pipelined(d // fu, lambda it, fu=fu, sub=sub: [
            ((it * fu + u) * sub, it * fu + u, False, it * fu + u) for u in range(fu)])

        if nb > 1:
            lu = ATT_LATER_UNROLL
            per_res = (nb - 1) // lu
            assert per_res * lu == nb - 1

            def later_group(it, lu=lu, per_res=per_res, sub=sub, d=d):
                if per_res == 1:
                    r, n0 = it, 1
                elif d == 1:
                    r, n0 = 0, it * lu + 1
                elif isinstance(it, int):
                    r, n0 = it // per_res, (it % per_res) * lu + 1
                else:
                    r, n0 = lax.div(it, per_res), lax.rem(it, per_res) * lu + 1
                return [(r * sub + (n0 + u) * qb, r + d * qb * (n0 + u), True, d + it * lu + u)
                        for u in range(lu)]

            pipelined(d * per_res, later_group)

    tiles_per_residue = SEQ // d_home // COMBINE_ROWS

    def combine(tb, carry):
        rows = pl.ds(pl.multiple_of(tb * COMBINE_ROWS, COMBINE_ROWS), COMBINE_ROWS)
        tokens = pl.ds(lax.div(tb, tiles_per_residue) + d_home * COMBINE_ROWS * lax.rem(tb, tiles_per_residue),
                       COMBINE_ROWS, stride=d_home)
        at = lambda ref, bi: ref.at[bi][tokens if DILATED_BRANCHES[bi][1] == 1 else rows, :]
        ms = [at(macc_ref, bi) for bi in range(len(DILATED_BRANCHES))]
        top = functools.reduce(jnp.maximum, ms)
        ws = [jnp.exp(m - top) for m in ms]
        num = sum(w * at(oacc_ref, bi) for bi, w in enumerate(ws))
        den = sum(w * at(lacc_ref, bi) for bi, w in enumerate(ws))
        src_ref.at[0][tokens, :] = num / den
        return carry

    lax.fori_loop(0, SEQ // COMBINE_ROWS, combine, 0)
    for c0 in range(0, SEQ, COMBINE_ROWS):
        y_ref[c0:c0 + COMBINE_ROWS, :] = src_ref[0, c0:c0 + COMBINE_ROWS, :].astype(BF16)


def _attention(za):
    pairs = ATT_WIDTH // LANES
    col = lambda off: pl.BlockSpec((None, SEQ, LANES), lambda b, hp: (b, 0, hp + off))
    n_dil = len(DILATED_BRANCHES) - 1
    return pl.pallas_call(
        _attn_kernel,
        grid=(BATCH, pairs),
        in_specs=[col(0), col(pairs), col(2 * pairs)],
        out_specs=col(0),
        out_shape=jax.ShapeDtypeStruct((BATCH, SEQ, ATT_WIDTH), BF16),
        scratch_shapes=[pltpu.VMEM((2, SEQ, LANES), F32),
                        pltpu.VMEM((n_dil + 1, HEAD_PAIR, SEQ, LANES), BF16),
                        pltpu.VMEM((n_dil, SEQ, LANES), BF16),
                        pltpu.VMEM((n_dil, SEQ, LANES), BF16),
                        pltpu.VMEM((n_dil + 1, SEQ, LANES), F32),
                        pltpu.VMEM((n_dil + 1, SEQ, LANES), F32),
                        pltpu.VMEM((n_dil + 1, SEQ, LANES), F32),
                        pltpu.VMEM((HEAD_PAIR * ATT_BLOCK, 2 * ATT_BLOCK), F32),
                        pltpu.VMEM((HEAD_PAIR * ATT_BLOCK, ATT_BLOCK), F32),
                        pltpu.VMEM((SEQ // ATT_BLOCK, HEAD_PAIR * ATT_BLOCK, 2 * ATT_BLOCK), BF16),
                        pltpu.VMEM((SEQ // ATT_BLOCK, HEAD_PAIR * ATT_BLOCK, 2 * ATT_BLOCK), F32)],
        compiler_params=_params("parallel", "parallel"),
        name="dilated_attention",
    )(za, za, za)


def _post_kernel(ys_ref, yr_ref, ya_ref, x_ref, wo_ref, an_ref, l1w_ref, l1b_ref,
                 w1_ref, w2_ref, l2w_ref, l2b_ref, o_ref, x1_ref, acc_ref):
    part = POST_TILE // POST_PARTS
    n_chunks = D_FF // FF_CHUNK
    slab = part // n_chunks
    parts = [slice(r0, r0 + part) for r0 in range(0, POST_TILE, part)]

    def out_proj(rows):
        ya = ya_ref[rows, :].astype(F32)
        ms = jnp.mean(ya * ya, axis=-1, keepdims=True)
        ya = (ya * lax.rsqrt(ms + LN_EPS) * an_ref[...]).astype(BF16)
        h = jnp.dot(ys_ref[rows, :], wo_ref[0:SSM_WIDTH, :], preferred_element_type=F32)
        h = h + jnp.dot(yr_ref[rows, :], wo_ref[SSM_WIDTH:SSM_WIDTH + RET_WIDTH, :],
                        preferred_element_type=F32)
        h = h + jnp.dot(ya, wo_ref[SSM_WIDTH + RET_WIDTH:, :], preferred_element_type=F32)
        o_ref[rows, :] = DEEPNORM_ALPHA * x_ref[rows, :] + h

    def ln1(rows):
        x1_ref[rows, :] = _layer_norm(o_ref[rows, :], l1w_ref[...], l1b_ref[...])

    def ln2(rows):
        o_ref[rows, :] = _layer_norm(DEEPNORM_ALPHA * x1_ref[rows, :] + acc_ref[rows, :],
                                     l2w_ref[...], l2b_ref[...])

    def mlp(rows, between):
        xb = x1_ref[rows, :].astype(BF16)
        acc = jnp.zeros((part, D_MODEL), F32)
        for k in range(n_chunks):
            c = k * FF_CHUNK
            hid = jnp.dot(xb, w1_ref[:, c:c + FF_CHUNK], preferred_element_type=F32)
            hid = jnp.square(jnp.maximum(hid, 0.0)).astype(BF16)
            acc = acc + jnp.dot(hid, w2_ref[c:c + FF_CHUNK, :], preferred_element_type=F32)
            between(k)
        acc_ref[rows, :] = acc

    slab_of = lambda rows, k: slice(rows.start + k * slab, rows.start + (k + 1) * slab)
    for rows in parts:
        out_proj(rows)
    ln1(parts[0])
    for i, rows in enumerate(parts):
        def between(k, i=i):
            if i + 1 < len(parts):
                ln1(slab_of(parts[i + 1], k))
            if i > 0:
                ln2(slab_of(parts[i - 1], k))
        mlp(rows, between)
    ln2(parts[-1])


def _post(y_ssm, y_ret, y_att, x, w_out, attn_norm, ln1_w, ln1_b, w1, w2, ln2_w, ln2_b):
    tm = POST_TILE
    row = lambda width: pl.BlockSpec((None, tm, width), lambda b, i: (b, i, 0))
    full = lambda shape: pl.BlockSpec(shape, lambda b, i: (0, 0), pipeline_mode=pl.Buffered(1))
    return pl.pallas_call(
        _post_kernel,
        grid=(BATCH, SEQ // tm),
        in_specs=[row(SSM_WIDTH), row(RET_WIDTH), row(ATT_WIDTH), row(D_MODEL),
                  full((D_MODEL, D_MODEL)), full((1, ATT_WIDTH)), full((1, D_MODEL)), full((1, D_MODEL)),
                  full((D_MODEL, D_FF)), full((D_FF, D_MODEL)), full((1, D_MODEL)), full((1, D_MODEL))],
        out_specs=row(D_MODEL),
        out_shape=jax.ShapeDtypeStruct((BATCH, SEQ, D_MODEL), F32),
        scratch_shapes=[pltpu.VMEM((tm, D_MODEL), F32), pltpu.VMEM((tm, D_MODEL), F32)],
        compiler_params=_params("parallel", "parallel"),
        name="out_proj_mlp",
    )(y_ssm, y_ret, y_att, x, w_out, attn_norm, ln1_w, ln1_b, w1, w2, ln2_w, ln2_b)


def kernel(x, w_in, ssm_lambda_re, ssm_lambda_im, ssm_b_re, ssm_b_im, ssm_c_re, ssm_c_im, ssm_d, ssm_log_dt, ssm_w_glu, ssm_b_glu, ssm_out_norm, ret_out_norm, attn_out_norm, w_out, ln1_w, ln1_b, mlp_w1, mlp_w2, ln2_w, ln2_b):
    vec = lambda p: p.astype(F32).reshape(1, -1)
    x = x.astype(F32)
    for i in range(DEPTH):
        u, za, y_ret = _in_proj_ret(x, w_in[i].astype(BF16), vec(ret_out_norm[i]))
        wb, a_re, a_im, wc = _ssm_weights(ssm_lambda_re[i], ssm_lambda_im[i], ssm_b_re[i], ssm_b_im[i],
                                          ssm_c_re[i], ssm_c_im[i], ssm_log_dt[i])
        y_ssm = _ssm(u, wb, a_re, a_im, wc, vec(ssm_d[i]), ssm_w_glu[i].astype(BF16),
                     vec(ssm_b_glu[i]), vec(ssm_out_norm[i]))
        y_att = _attention(za)
        x = _post(y_ssm, y_ret, y_att, x, w_out[i].astype(BF16), vec(attn_out_norm[i]),
                  vec(ln1_w[i]), vec(ln1_b[i]), mlp_w1[i].astype(BF16), mlp_w2[i].astype(BF16),
                  vec(ln2_w[i]), vec(ln2_b[i]))
    return x
```

```python
import functools
import math

import jax
import jax.numpy as jnp
from jax import lax
from jax.experimental import pallas as pl
from jax.experimental.pallas import tpu as pltpu

F32 = jnp.float32
BF16 = jnp.bfloat16

D_MODEL = 1024
BATCH = 16
SEQ = 2048
DEPTH = 2
SSM_WIDTH = 256
SSM_GROUP = 16
SSM_GROUPS = 16
SSM_STATE = 64
SSM_STATES = SSM_GROUPS * SSM_STATE
RET_HEAD_DIM = 64
RET_WIDTH = 256
RET_CHUNK = 128
ATT_HEAD_DIM = 64
ATT_WIDTH = 512
ATT_HEADS = 8
DILATED_BRANCHES = ((128, 1), (512, 4), (2048, 16))
ATT_BLOCK = 128
IN_WIDTH = SSM_WIDTH + 4 * RET_WIDTH + 3 * ATT_WIDTH
D_FF = 4 * D_MODEL
DEEPNORM_ALPHA = (2 * DEPTH) ** 0.25
LN_EPS = 1e-5

LANES = 128
HEAD_PAIR = LANES // ATT_HEAD_DIM
VMEM_LIMIT_BYTES = 56 * 1024 * 1024
MASKED_SCORE = -1e30

TOKEN_TILE = 512
IN_PROJ_COLS = 256
POST_TILE = 1024
POST_PARTS = 4
SSM_TIME_TILE = 64
SSM_BATCH_HALF = BATCH // 2
SSM_COL_CHUNK = 512
SSM_STAGE_PITCH = SSM_TIME_TILE + 8
FF_CHUNK = 1024
COMBINE_ROWS = 256
ATT_HOME_BRANCH = 1
PREP_ROWS = 128
ATT_FIRST_UNROLL = 4
ATT_LATER_UNROLL = 3

assert RET_HEAD_DIM == ATT_HEAD_DIM and HEAD_PAIR == 2
assert all(w // d == ATT_BLOCK for w, d in DILATED_BRANCHES)
assert DILATED_BRANCHES[0][1] == 1 and all(
    b[1] % a[1] == 0 for a, b in zip(DILATED_BRANCHES, DILATED_BRANCHES[1:]))


def _params(*semantics):
    return pltpu.CompilerParams(dimension_semantics=semantics, vmem_limit_bytes=VMEM_LIMIT_BYTES)


def _layer_block(shape, layer, single_buffer=False):
    mode = dict(pipeline_mode=pl.Buffered(1)) if single_buffer else {}
    return pl.BlockSpec((None,) + tuple(shape), lambda *_: (layer,) + (0,) * len(shape), **mode)


def _sigmoid(x):
    return 1.0 / (1.0 + jnp.exp(-x))


def _layer_norm(r, w, b):
    mu = jnp.mean(r, axis=-1, keepdims=True)
    d = r - mu
    var = jnp.mean(d * d, axis=-1, keepdims=True)
    return d * lax.rsqrt(var + LN_EPS) * w + b


def _in_proj_ret_kernel(x_ref, w_ref, nw_ref, u_ref, za_ref, y_ref, zr_ref, znew_ref, r_ref, const_ref):
    c = RET_CHUNK
    dh = RET_HEAD_DIM
    pairs = RET_WIDTH // LANES
    tm = TOKEN_TILE
    step = pl.program_id(0)
    tiles_per_batch = SEQ // tm
    lane = lax.broadcasted_iota(jnp.int32, (c, LANES), 1)
    row = lax.broadcasted_iota(jnp.int32, (c, LANES), 0)
    head1 = lane >= dh
    block_diag = (row >= dh) == head1
    XI, ZETA, G_CHUNK, DMAT = 0, 1, 2, 3

    @pl.when(step == 0)
    def _():
        zr_ref[...] = jnp.zeros_like(zr_ref)
        rowf = row.astype(F32)
        diff = (row - lane).astype(F32)
        for p in range(pairs):
            lg = jnp.log(1.0 - jnp.exp2(-5.0 - (HEAD_PAIR * p + head1.astype(F32))))
            const_ref[p, XI] = jnp.exp((rowf + 1.0) * lg)
            const_ref[p, ZETA] = jnp.exp((c - 1.0 - rowf) * lg) * (dh ** -0.5)
            const_ref[p, G_CHUNK] = jnp.exp(c * lg)
            for j in range(HEAD_PAIR):
                lg_j = math.log(1.0 - 2.0 ** (-5.0 - (HEAD_PAIR * p + j)))
                const_ref[p, DMAT + j] = jnp.where(
                    diff >= 0, jnp.exp(jnp.maximum(diff, 0.0) * lg_j), 0.0) * (dh ** -0.5)

    @pl.when(lax.rem(jnp.maximum(step - 1, 0), tiles_per_batch) == 0)
    def _():
        r_ref[...] = jnp.zeros_like(r_ref)

    def head_mean(t):
        s0 = jnp.sum(jnp.where(head1, 0.0, t), axis=-1, keepdims=True)
        s1 = jnp.sum(jnp.where(head1, t, 0.0), axis=-1, keepdims=True)
        return jnp.where(head1, s1, s0) * (1.0 / dh)

    def retention_chunk(p, n):
        rows = slice(n * c, (n + 1) * c)
        col = lambda part: slice(part * RET_WIDTH + p * LANES, part * RET_WIDTH + (p + 1) * LANES)
        q, k, v = (zr_ref[rows, col(part)] for part in range(3))
        r_prev = r_ref[p]
        o = jnp.zeros((c, LANES), F32)
        for j in range(HEAD_PAIR):
            mine = head1 if j else jnp.logical_not(head1)
            qj = jnp.where(mine, q, jnp.zeros_like(q))
            s = lax.dot_general(qj, k, (((1,), (1,)), ((), ())), preferred_element_type=F32)
            s = s * const_ref[p, DMAT + j]
            vj = jnp.where(mine, v, jnp.zeros_like(v))
            o = o + jnp.dot(s.astype(BF16), vj, preferred_element_type=F32)
        qx = (q.astype(F32) * const_ref[p, XI]).astype(BF16)
        o = o + jnp.dot(qx, r_prev.astype(BF16), preferred_element_type=F32)
        kz = (k.astype(F32) * const_ref[p, ZETA]).T.astype(BF16)
        kv = jnp.dot(kz, v, preferred_element_type=F32)
        r_ref[p] = jnp.where(block_diag, const_ref[p, G_CHUNK] * r_prev + kv, 0.0)
        dlt = o - head_mean(o)
        var = head_mean(dlt * dlt)
        gate = zr_ref[rows, col(3)].astype(F32)
        y = dlt * lax.rsqrt(var + LN_EPS) * nw_ref[:, p * LANES:(p + 1) * LANES] * (gate * _sigmoid(gate))
        y_ref[n * c:(n + 1) * c, p * LANES:(p + 1) * LANES] = y.astype(BF16)

    xb = x_ref[...].astype(BF16)

    def project(lo):
        z = jnp.dot(xb, w_ref[:, lo:lo + IN_PROJ_COLS], preferred_element_type=F32).astype(BF16)
        hi = lo + IN_PROJ_COLS
        if hi <= SSM_WIDTH:
            u_ref[:, lo:hi] = z
        elif hi <= SSM_WIDTH + 4 * RET_WIDTH:
            znew_ref[:, lo - SSM_WIDTH:hi - SSM_WIDTH] = z
        else:
            za_ref[:, lo - SSM_WIDTH - 4 * RET_WIDTH:hi - SSM_WIDTH - 4 * RET_WIDTH] = z

    col_blocks = list(range(0, IN_WIDTH, IN_PROJ_COLS))
    units = [(p, n) for n in range(tm // c) for p in range(pairs)]
    for k in range(max(len(col_blocks), len(units))):
        if k < len(col_blocks):
            project(col_blocks[k])
        if k < len(units):
            retention_chunk(*units[k])
    zr_ref[...] = znew_ref[...]


def _in_proj_ret(x, w_in, ret_norm, layer):
    tm = TOKEN_TILE
    per_batch = SEQ // tm
    n_tiles = BATCH * per_batch

    def tile(width, index):
        return pl.BlockSpec((None, tm, width),
                            lambda s: (lax.div(index(s), per_batch), lax.rem(index(s), per_batch), 0))

    cur = lambda width: tile(width, lambda s: jnp.minimum(s, n_tiles - 1))
    lag = lambda width: tile(width, lambda s: jnp.maximum(s - 1, 0))
    return pl.pallas_call(
        _in_proj_ret_kernel,
        grid=(n_tiles + 1,),
        in_specs=[cur(D_MODEL),
                  _layer_block((D_MODEL, IN_WIDTH), layer, single_buffer=True),
                  _layer_block((1, RET_WIDTH), layer)],
        out_specs=[cur(SSM_WIDTH), cur(3 * ATT_WIDTH), lag(RET_WIDTH)],
        out_shape=[jax.ShapeDtypeStruct((BATCH, SEQ, SSM_WIDTH), BF16),
                   jax.ShapeDtypeStruct((BATCH, SEQ, 3 * ATT_WIDTH), BF16),
                   jax.ShapeDtypeStruct((BATCH, SEQ, RET_WIDTH), BF16)],
        scratch_shapes=[pltpu.VMEM((tm, 4 * RET_WIDTH), BF16),
                        pltpu.VMEM((tm, 4 * RET_WIDTH), BF16),
                        pltpu.VMEM((RET_WIDTH // LANES, LANES, LANES), F32),
                        pltpu.VMEM((RET_WIDTH // LANES, 3 + HEAD_PAIR, RET_CHUNK, LANES), F32)],
        compiler_params=_params("arbitrary"),
        name="in_proj_retention",
    )(x, w_in, ret_norm)


def _ssm_kernel(u_ref, wb_ref, are_ref, aim_ref, wc_ref, d_ref, wglu_ref, bglu_ref, nw_ref,
                y_ref, bu_ref, st_ref, ut_ref, stage_ref):
    n_st = SSM_STATES
    bh = SSM_BATCH_HALF
    lt = SSM_TIME_TILE
    pitch = SSM_STAGE_PITCH
    lane_slabs = SSM_WIDTH // LANES
    chunks = [(slice(c, c + SSM_COL_CHUNK), slice(n_st + c, n_st + c + SSM_COL_CHUNK))
              for c in range(0, n_st, SSM_COL_CHUNK)]

    @pl.when(pl.program_id(0) == 0)
    def _():
        st_ref[...] = jnp.zeros_like(st_ref)

    def stage_rows(h, t):
        return pl.ds(h * bh * pitch + t, bh, stride=pitch)

    def gather(h):
        for b in range(h * bh, (h + 1) * bh):
            for s in range(lane_slabs):
                stage_ref[s, b * pitch:b * pitch + lt, :] = (
                    u_ref[b, :, s * LANES:(s + 1) * LANES].astype(F32))
        for t in range(0, lt, 2):
            for s in range(lane_slabs):
                pair = jnp.concatenate([stage_ref.at[s][stage_rows(h, t + k), :] for k in range(2)], axis=0)
                ut_ref[h, t * bh:(t + 2) * bh, s * LANES:(s + 1) * LANES] = pair.astype(BF16)

    def scatter(h, y):
        for t in range(lt):
            for s in range(lane_slabs):
                stage_ref.at[s][stage_rows(h, t), :] = y[t * bh:(t + 1) * bh, s * LANES:(s + 1) * LANES]
        for b in range(h * bh, (h + 1) * bh):
            for s in range(lane_slabs):
                y_ref[b, :, s * LANES:(s + 1) * LANES] = (
                    stage_ref[s, b * pitch:b * pitch + lt, :].astype(BF16))

    def project(h, cols):
        for sl in cols:
            bu_ref[h, :, sl] = jnp.dot(ut_ref[h], wb_ref[:, sl], preferred_element_type=F32)

    def scan(h, cols):
        re, im = cols
        ar = jnp.broadcast_to(are_ref[:, re], (bh, SSM_COL_CHUNK))
        ai = jnp.broadcast_to(aim_ref[:, re], (bh, SSM_COL_CHUNK))
        xr, xi = st_ref[h, :, re], st_ref[h, :, im]
        for t in range(SSM_TIME_TILE):
            rows = slice(t * bh, (t + 1) * bh)
            xr, xi = (ar * xr - ai * xi + bu_ref[h, rows, re],
                      ar * xi + ai * xr + bu_ref[h, rows, im])
            bu_ref[h, rows, re] = xr
            bu_ref[h, rows, im] = xi
        st_ref[h, :, re] = xr
        st_ref[h, :, im] = xi

    def readout(h, cols, acc):
        for sl in cols:
            acc = acc + jnp.dot(bu_ref[h, :, sl].astype(BF16), wc_ref[sl, :], preferred_element_type=F32)
        return acc

    def finish(h, y):
        y = y + d_ref[...] * ut_ref[h].astype(F32)
        cdf = 0.5 * (1.0 + jnp.tanh(math.sqrt(2.0 / math.pi) * (y + 0.044715 * (y * y * y))))
        g = y * cdf
        gate = jnp.dot(g.astype(BF16), wglu_ref[...], preferred_element_type=F32) + bglu_ref[...]
        out = g * _sigmoid(gate)
        ms = jnp.mean(out * out, axis=-1, keepdims=True)
        scatter(h, out * lax.rsqrt(ms + LN_EPS) * nw_ref[...])

    zero = jnp.zeros((SSM_TIME_TILE * bh, SSM_WIDTH), F32)
    gather(0)
    for k, cols in enumerate(chunks):
        project(0, cols)
        if k == 0:
            gather(1)
    for cols in chunks:
        project(1, cols)
        scan(0, cols)
    y0 = zero
    for cols in chunks:
        y0 = readout(0, cols, y0)
        scan(1, cols)
    y1 = zero
    for k, cols in enumerate(chunks):
        y1 = readout(1, cols, y1)
        if k == 0:
            finish(0, y0)
    finish(1, y1)


def _ssm(u, wb, a_re, a_im, wc, d_skip, w_glu, b_glu, norm_w, layer):
    rows = SSM_TIME_TILE * SSM_BATCH_HALF
    full = lambda shape: _layer_block(shape, layer)
    steps = pl.BlockSpec((BATCH, SSM_TIME_TILE, SSM_WIDTH), lambda i: (0, i, 0))
    return pl.pallas_call(
        _ssm_kernel,
        grid=(SEQ // SSM_TIME_TILE,),
        in_specs=[steps,
                  full((SSM_WIDTH, 2 * SSM_STATES)), full((1, SSM_STATES)), full((1, SSM_STATES)),
                  full((2 * SSM_STATES, SSM_WIDTH)), full((1, SSM_WIDTH)),
                  full((SSM_WIDTH, SSM_WIDTH)), full((1, SSM_WIDTH)), full((1, SSM_WIDTH))],
        out_specs=steps,
        out_shape=jax.ShapeDtypeStruct((BATCH, SEQ, SSM_WIDTH), BF16),
        scratch_shapes=[pltpu.VMEM((2, rows, 2 * SSM_STATES), F32),
                        pltpu.VMEM((2, SSM_BATCH_HALF, 2 * SSM_STATES), F32),
                        pltpu.VMEM((2, rows, SSM_WIDTH), BF16),
                        pltpu.VMEM((SSM_WIDTH // LANES, BATCH * SSM_STAGE_PITCH, LANES), F32)],
        compiler_params=_params("arbitrary"),
        name="s5_mixer",
    )(u, wb, a_re, a_im, wc, d_skip, w_glu, b_glu, norm_w)


def _ssm_weights(lam_re, lam_im, b_re, b_im, c_re, c_im, log_dt):
    g, p, h = SSM_GROUPS, SSM_STATE, SSM_GROUP
    lr, li = lam_re.astype(F32), lam_im.astype(F32)
    dt = jnp.exp(log_dt.astype(F32))[:, None]
    mag = jnp.exp(lr * dt)
    a_re, a_im = mag * jnp.cos(li * dt), mag * jnp.sin(li * dt)
    den = lr * lr + li * li
    nr, ni = a_re - 1.0, a_im
    f_re = ((nr * lr + ni * li) / den)[..., None]
    f_im = ((ni * lr - nr * li) / den)[..., None]
    br, bi = b_re.astype(F32), b_im.astype(F32)
    bb_re = f_re * br - f_im * bi
    bb_im = f_re * bi + f_im * br
    eye = jnp.eye(g, dtype=F32)
    embed_b = lambda bb: jnp.einsum('gph,gk->ghkp', bb, eye).reshape(g * h, g * p)
    embed_c = lambda cc: jnp.einsum('ghp,gk->gpkh', cc, eye).reshape(g * p, g * h)
    wb = jnp.concatenate([embed_b(bb_re), embed_b(bb_im)], axis=1).astype(BF16)
    wc = jnp.concatenate([embed_c(c_re.astype(F32)), -embed_c(c_im.astype(F32))], axis=0).astype(BF16)
    return wb, a_re.reshape(1, g * p), a_im.reshape(1, g * p), wc


def _attn_kernel(q_ref, k_ref, v_ref, y_ref, src_ref, dq_ref, dk_ref, dv_ref,
                 oacc_ref, macc_ref, lacc_ref, bias_ref, bias0_ref, p_ref, s_ref):
    qb = ATT_BLOCK
    dh = ATT_HEAD_DIM
    d_home = DILATED_BRANCHES[ATT_HOME_BRANCH][1]
    hp = pl.program_id(1)
    head1 = lax.broadcasted_iota(jnp.int32, (qb, LANES), 1) >= dh

    def emit_q(bi, rows, blk):
        other = lax.broadcasted_iota(jnp.int32, blk.shape, 1) >= dh
        blk = blk * (dh ** -0.5)
        dq_ref[bi, 0, rows, :] = jnp.where(other, 0.0, blk).astype(BF16)
        dq_ref[bi, 1, rows, :] = jnp.where(other, blk, 0.0).astype(BF16)

    def emit_k(bi, rows, blk):
        if bi > 0:
            dk_ref[bi - 1, rows, :] = blk.astype(BF16)

    def emit_v(bi, rows, blk):
        if bi > 0:
            dv_ref[bi - 1, rows, :] = blk.astype(BF16)

    piece = PREP_ROWS
    for src, emit in ((q_ref, emit_q), (k_ref, emit_k), (v_ref, emit_v)):
        for c0 in range(0, SEQ, piece):
            rows = slice(c0, c0 + piece)
            natural = src[rows, :].astype(F32)
            src_ref[0, rows, :] = natural
            emit(0, rows, natural)
        d_prev = 1
        for bi in range(1, len(DILATED_BRANCHES)):
            d = DILATED_BRANCHES[bi][1]
            step = d // d_prev
            sub_prev, sub = SEQ // d_prev, SEQ // d
            cur, nxt = (bi - 1) % 2, bi % 2
            for r_prev in range(d_prev):
                for t in range(step):
                    r = r_prev + d_prev * t
                    for c0 in range(0, sub, piece):
                        n = min(piece, sub - c0)
                        rows = slice(r * sub + c0, r * sub + c0 + n)
                        blk = src_ref.at[cur][pl.ds(r_prev * sub_prev + t + step * c0, n, stride=step), :]
                        emit(bi, rows, blk)
                        if bi + 1 < len(DILATED_BRANCHES):
                            src_ref[nxt, rows, :] = blk
            d_prev = d

    row2 = lax.broadcasted_iota(jnp.int32, (HEAD_PAIR * qb, 2 * qb), 0)
    key2 = lax.broadcasted_iota(jnp.int32, (HEAD_PAIR * qb, 2 * qb), 1)
    row1 = lax.broadcasted_iota(jnp.int32, (HEAD_PAIR * qb, qb), 0)
    key1 = lax.broadcasted_iota(jnp.int32, (HEAD_PAIR * qb, qb), 1)

    def head_slope(row):
        head = (HEAD_PAIR * hp).astype(F32) + (row >= qb).astype(F32)
        return jnp.exp2(-(8.0 / ATT_HEADS) * (head + 1.0))

    dist2 = qb + jnp.where(row2 >= qb, row2 - qb, row2) - key2
    dist1 = jnp.where(row1 >= qb, row1 - qb, row1) - key1
    slope2, slope1 = head_slope(row2), head_slope(row1)

    def aligned(x):
        return x if isinstance(x, int) else pl.multiple_of(x, qb)

    for bi, (window, d) in enumerate(DILATED_BRANCHES):
        sub = SEQ // d
        nb = sub // qb
        ksrc = k_ref if bi == 0 else dk_ref.at[bi - 1]
        vsrc = v_ref if bi == 0 else dv_ref.at[bi - 1]
        bias_ref[...] = jnp.where((dist2 >= 0) & (dist2 <= window // d),
                                  -slope2 * (d * dist2).astype(F32), MASKED_SCORE)
        bias0_ref[...] = jnp.where(dist1 >= 0, -slope1 * (d * dist1).astype(F32), MASKED_SCORE)

        def rows_of(blk, d=d):
            base, first_token, has_prev, _ = blk
            rows = pl.ds(aligned(base), qb)
            krows = pl.ds(aligned(base - qb), 2 * qb) if has_prev else rows
            if d == 1 or d == d_home:
                out_rows = rows
            else:
                ft = first_token
                home = (ft % d_home) * (SEQ // d_home) + ft // d_home if isinstance(ft, int) else (
                    lax.rem(ft, d_home) * (SEQ // d_home) + lax.div(ft, d_home))
                out_rows = pl.ds(home, qb, stride=d // d_home)
            return rows, krows, out_rows, (2 * qb if has_prev else qb)

        def scores(blk, bi=bi, ksrc=ksrc, rows_of=rows_of):
            rows, krows, out_rows, nk = rows_of(blk)
            q2 = jnp.concatenate([dq_ref[bi, j, rows, :] for j in range(HEAD_PAIR)], axis=0)
            s_ref[blk[3], :, 0:nk] = lax.dot_general(
                q2, ksrc[krows, :], (((1,), (1,)), ((), ())), preferred_element_type=F32)

        def softmax(blk, bi=bi, rows_of=rows_of):
            rows, krows, out_rows, nk = rows_of(blk)
            s = s_ref[blk[3], :, 0:nk] + (bias_ref[...] if blk[2] else bias0_ref[...])
            m = jnp.max(s, axis=-1, keepdims=True)
            e = jnp.exp(s - m)
            l = jnp.sum(e, axis=-1, keepdims=True)
            p_ref[blk[3], :, 0:nk] = e.astype(BF16)
            packed = lambda c: jnp.where(head1, jnp.broadcast_to(c[qb:], (qb, LANES)),
                                         jnp.broadcast_to(c[:qb], (qb, LANES)))
            macc_ref.at[bi][out_rows, :] = packed(m)
            lacc_ref.at[bi][out_rows, :] = packed(l)

        def values(blk, bi=bi, vsrc=vsrc, rows_of=rows_of):
            rows, krows, out_rows, nk = rows_of(blk)
            v = vsrc[krows, :]
            o = [jnp.dot(p_ref[blk[3], j * qb:(j + 1) * qb, 0:nk], v, preferred_element_type=F32)
                 for j in range(HEAD_PAIR)]
            oacc_ref.at[bi][out_rows, :] = jnp.where(head1, o[1], o[0])

        def pipelined(n_groups, group, stages=(scores, softmax, values)):
            def step(t, valid):
                for k in reversed(range(len(stages))):
                    if valid(t - k):
                        for blk in group(t - k):
                            stages[k](blk)

            depth = len(stages) - 1
            head_steps = min(depth, n_groups)
            for t in range(head_steps):
                step(t, lambda g: 0 <= g < n_groups)
            if n_groups > depth:
                def body(t, carry):
                    step(t, lambda g: True)
                    return carry

                lax.fori_loop(depth, n_groups, body, 0)
            for t in range(max(n_groups, head_steps), n_groups + depth):
                step(t, lambda g: 0 <= g < n_groups)

        fu = min(d, ATT_FIRST_UNROLL)
        pipelined(d // fu, lambda it, fu=fu, sub=sub: [
            ((it * fu + u) * sub, it * fu + u, False, it * fu + u) for u in range(fu)])

        if nb > 1:
            lu = ATT_LATER_UNROLL
            per_res = (nb - 1) // lu
            assert per_res * lu == nb - 1

            def later_group(it, lu=lu, per_res=per_res, sub=sub, d=d):
                if per_res == 1:
                    r, n0 = it, 1
                elif d == 1:
                    r, n0 = 0, it * lu + 1
                elif isinstance(it, int):
                    r, n0 = it // per_res, (it % per_res) * lu + 1
                else:
                    r, n0 = lax.div(it, per_res), lax.rem(it, per_res) * lu + 1
                return [(r * sub + (n0 + u) * qb, r + d * qb * (n0 + u), True, d + it * lu + u)
                        for u in range(lu)]

            pipelined(d * per_res, later_group)

    tiles_per_residue = SEQ // d_home // COMBINE_ROWS

    def combine(tb, carry):
        rows = pl.ds(pl.multiple_of(tb * COMBINE_ROWS, COMBINE_ROWS), COMBINE_ROWS)
        tokens = pl.ds(lax.div(tb, tiles_per_residue) + d_home * COMBINE_ROWS * lax.rem(tb, tiles_per_residue),
                       COMBINE_ROWS, stride=d_home)
        at = lambda ref, bi: ref.at[bi][tokens if DILATED_BRANCHES[bi][1] == 1 else rows, :]
        ms = [at(macc_ref, bi) for bi in range(len(DILATED_BRANCHES))]
        top = functools.reduce(jnp.maximum, ms)
        ws = [jnp.exp(m - top) for m in ms]
        num = sum(w * at(oacc_ref, bi) for bi, w in enumerate(ws))
        den = sum(w * at(lacc_ref, bi) for bi, w in enumerate(ws))
        src_ref.at[0][tokens, :] = num / den
        return carry

    lax.fori_loop(0, SEQ // COMBINE_ROWS, combine, 0)
    for c0 in range(0, SEQ, COMBINE_ROWS):
        y_ref[c0:c0 + COMBINE_ROWS, :] = src_ref[0, c0:c0 + COMBINE_ROWS, :].astype(BF16)


def _attention(za):
    pairs = ATT_WIDTH // LANES
    col = lambda off: pl.BlockSpec((None, SEQ, LANES), lambda b, hp: (b, 0, hp + off))
    n_dil = len(DILATED_BRANCHES) - 1
    return pl.pallas_call(
        _attn_kernel,
        grid=(BATCH, pairs),
        in_specs=[col(0), col(pairs), col(2 * pairs)],
        out_specs=col(0),
        out_shape=jax.ShapeDtypeStruct((BATCH, SEQ, ATT_WIDTH), BF16),
        scratch_shapes=[pltpu.VMEM((2, SEQ, LANES), F32),
                        pltpu.VMEM((n_dil + 1, HEAD_PAIR, SEQ, LANES), BF16),
                        pltpu.VMEM((n_dil, SEQ, LANES), BF16),
                        pltpu.VMEM((n_dil, SEQ, LANES), BF16),
                        pltpu.VMEM((n_dil + 1, SEQ, LANES), F32),
                        pltpu.VMEM((n_dil + 1, SEQ, LANES), F32),
                        pltpu.VMEM((n_dil + 1, SEQ, LANES), F32),
                        pltpu.VMEM((HEAD_PAIR * ATT_BLOCK, 2 * ATT_BLOCK), F32),
                        pltpu.VMEM((HEAD_PAIR * ATT_BLOCK, ATT_BLOCK), F32),
                        pltpu.VMEM((SEQ // ATT_BLOCK, HEAD_PAIR * ATT_BLOCK, 2 * ATT_BLOCK), BF16),
                        pltpu.VMEM((SEQ // ATT_BLOCK, HEAD_PAIR * ATT_BLOCK, 2 * ATT_BLOCK), F32)],
        compiler_params=_params("parallel", "parallel"),
        name="dilated_attention",
    )(za, za, za)


def _post_kernel(ys_ref, yr_ref, ya_ref, x_ref, wo_ref, an_ref, l1w_ref, l1b_ref,
                 w1_ref, w2_ref, l2w_ref, l2b_ref, o_ref, x1_ref, acc_ref):
    part = POST_TILE // POST_PARTS
    n_chunks = D_FF // FF_CHUNK
    slab = part // n_chunks
    parts = [slice(r0, r0 + part) for r0 in range(0, POST_TILE, part)]

    def out_proj(rows):
        ya = ya_ref[rows, :].astype(F32)
        ms = jnp.mean(ya * ya, axis=-1, keepdims=True)
        ya = (ya * lax.rsqrt(ms + LN_EPS) * an_ref[...]).astype(BF16)
        h = jnp.dot(ys_ref[rows, :], wo_ref[0:SSM_WIDTH, :], preferred_element_type=F32)
        h = h + jnp.dot(yr_ref[rows, :], wo_ref[SSM_WIDTH:SSM_WIDTH + RET_WIDTH, :],
                        preferred_element_type=F32)
        h = h + jnp.dot(ya, wo_ref[SSM_WIDTH + RET_WIDTH:, :], preferred_element_type=F32)
        o_ref[rows, :] = DEEPNORM_ALPHA * x_ref[rows, :] + h

    def ln1(rows):
        x1_ref[rows, :] = _layer_norm(o_ref[rows, :], l1w_ref[...], l1b_ref[...])

    def ln2(rows):
        o_ref[rows, :] = _layer_norm(DEEPNORM_ALPHA * x1_ref[rows, :] + acc_ref[rows, :],
                                     l2w_ref[...], l2b_ref[...])

    def mlp(rows, between):
        xb = x1_ref[rows, :].astype(BF16)
        acc = jnp.zeros((part, D_MODEL), F32)
        for k in range(n_chunks):
            c = k * FF_CHUNK
            hid = jnp.dot(xb, w1_ref[:, c:c + FF_CHUNK], preferred_element_type=F32)
            hid = jnp.square(jnp.maximum(hid, 0.0)).astype(BF16)
            acc = acc + jnp.dot(hid, w2_ref[c:c + FF_CHUNK, :], preferred_element_type=F32)
            between(k)
        acc_ref[rows, :] = acc

    slab_of = lambda rows, k: slice(rows.start + k * slab, rows.start + (k + 1) * slab)
    for rows in parts:
        out_proj(rows)
    ln1(parts[0])
    for i, rows in enumerate(parts):
        def between(k, i=i):
            if i + 1 < len(parts):
                ln1(slab_of(parts[i + 1], k))
            if i > 0:
                ln2(slab_of(parts[i - 1], k))
        mlp(rows, between)
    ln2(parts[-1])


def _post(y_ssm, y_ret, y_att, x, w_out, attn_norm, ln1_w, ln1_b, w1, w2, ln2_w, ln2_b, layer):
    tm = POST_TILE
    row = lambda width: pl.BlockSpec((None, tm, width), lambda b, i: (b, i, 0))
    full = lambda shape: _layer_block(shape, layer, single_buffer=True)
    return pl.pallas_call(
        _post_kernel,
        grid=(BATCH, SEQ // tm),
        in_specs=[row(SSM_WIDTH), row(RET_WIDTH), row(ATT_WIDTH), row(D_MODEL),
                  full((D_MODEL, D_MODEL)), full((1, ATT_WIDTH)), full((1, D_MODEL)), full((1, D_MODEL)),
                  full((D_MODEL, D_FF)), full((D_FF, D_MODEL)), full((1, D_MODEL)), full((1, D_MODEL))],
        out_specs=row(D_MODEL),
        out_shape=jax.ShapeDtypeStruct((BATCH, SEQ, D_MODEL), F32),
        scratch_shapes=[pltpu.VMEM((tm, D_MODEL), F32), pltpu.VMEM((tm, D_MODEL), F32)],
        compiler_params=_params("parallel", "parallel"),
        name="out_proj_mlp",
    )(y_ssm, y_ret, y_att, x, w_out, attn_norm, ln1_w, ln1_b, w1, w2, ln2_w, ln2_b)


def kernel(x, w_in, ssm_lambda_re, ssm_lambda_im, ssm_b_re, ssm_b_im, ssm_c_re, ssm_c_im, ssm_d, ssm_log_dt, ssm_w_glu, ssm_b_glu, ssm_out_norm, ret_out_norm, attn_out_norm, w_out, ln1_w, ln1_b, mlp_w1, mlp_w2, ln2_w, ln2_b):
    vec = lambda p: p.astype(F32)[:, None, :]
    bf16 = lambda p: p.astype(BF16)
    wb, a_re, a_im, wc = jax.vmap(_ssm_weights)(ssm_lambda_re, ssm_lambda_im, ssm_b_re, ssm_b_im,
                                                ssm_c_re, ssm_c_im, ssm_log_dt)
    w_in, ssm_w_glu, w_out, mlp_w1, mlp_w2 = map(bf16, (w_in, ssm_w_glu, w_out, mlp_w1, mlp_w2))
    (ssm_d, ssm_b_glu, ssm_out_norm, ret_out_norm, attn_out_norm, ln1_w, ln1_b, ln2_w, ln2_b) = map(
        vec, (ssm_d, ssm_b_glu, ssm_out_norm, ret_out_norm, attn_out_norm, ln1_w, ln1_b, ln2_w, ln2_b))
    x = x.astype(F32)
    for i in range(DEPTH):
        u, za, y_ret = _in_proj_ret(x, w_in, ret_out_norm, i)
        y_ssm = _ssm(u, wb, a_re, a_im, wc, ssm_d, ssm_w_glu, ssm_b_glu, ssm_out_norm, i)
        y_att = _attention(za)
        x = _post(y_ssm, y_ret, y_att, x, w_out, attn_out_norm, ln1_w, ln1_b, mlp_w1, mlp_w2, ln2_w, ln2_b, i)
    return x
```

```python
import functools
import math

import jax
import jax.numpy as jnp
from jax import lax
from jax.experimental import pallas as pl
from jax.experimental.pallas import tpu as pltpu

F32 = jnp.float32
BF16 = jnp.bfloat16

D_MODEL = 1024
BATCH = 16
SEQ = 2048
DEPTH = 2
SSM_WIDTH = 256
SSM_GROUP = 16
SSM_GROUPS = 16
SSM_STATE = 64
SSM_STATES = SSM_GROUPS * SSM_STATE
RET_HEAD_DIM = 64
RET_WIDTH = 256
RET_CHUNK = 128
ATT_HEAD_DIM = 64
ATT_WIDTH = 512
ATT_HEADS = 8
DILATED_BRANCHES = ((128, 1), (512, 4), (2048, 16))
ATT_BLOCK = 128
IN_WIDTH = SSM_WIDTH + 4 * RET_WIDTH + 3 * ATT_WIDTH
D_FF = 4 * D_MODEL
DEEPNORM_ALPHA = (2 * DEPTH) ** 0.25
LN_EPS = 1e-5

LANES = 128
HEAD_PAIR = LANES // ATT_HEAD_DIM
VMEM_LIMIT_BYTES = 56 * 1024 * 1024
MASKED_SCORE = -1e30

TOKEN_TILE = 512
IN_PROJ_COLS = 256
POST_TILE = 1024
POST_PARTS = 4
SSM_TIME_TILE = 64
SSM_BATCH_HALF = BATCH // 2
SSM_COL_CHUNK = 512
SSM_STAGE_PITCH = SSM_TIME_TILE + 8
FF_CHUNK = 1024
COMBINE_ROWS = 256
ATT_WORK_STRIDE = 7
ATT_HOME_BRANCH = 1
PREP_ROWS = 128
ATT_FIRST_UNROLL = 4
ATT_LATER_UNROLL = 3

assert RET_HEAD_DIM == ATT_HEAD_DIM and HEAD_PAIR == 2
assert all(w // d == ATT_BLOCK for w, d in DILATED_BRANCHES)
assert DILATED_BRANCHES[0][1] == 1 and all(
    b[1] % a[1] == 0 for a, b in zip(DILATED_BRANCHES, DILATED_BRANCHES[1:]))


def _params(*semantics):
    return pltpu.CompilerParams(dimension_semantics=semantics, vmem_limit_bytes=VMEM_LIMIT_BYTES)


def _layer_block(shape, layer, single_buffer=False):
    mode = dict(pipeline_mode=pl.Buffered(1)) if single_buffer else {}
    return pl.BlockSpec((None,) + tuple(shape), lambda *_: (layer,) + (0,) * len(shape), **mode)


def _sigmoid(x):
    return 1.0 / (1.0 + jnp.exp(-x))


def _layer_norm(r, w, b):
    mu = jnp.mean(r, axis=-1, keepdims=True)
    d = r - mu
    var = jnp.mean(d * d, axis=-1, keepdims=True)
    return d * lax.rsqrt(var + LN_EPS) * w + b


def _in_proj_ret_kernel(x_ref, w_ref, nw_ref, u_ref, y_ref, ya_ref, zr_ref, znew_ref, r_ref, const_ref,
                        ring_ref, *attn_scratch):
    c = RET_CHUNK
    dh = RET_HEAD_DIM
    pairs = RET_WIDTH // LANES
    tm = TOKEN_TILE
    step = pl.program_id(0)
    tiles_per_batch = SEQ // tm
    n_tiles = BATCH * tiles_per_batch
    att_pairs = ATT_WIDTH // LANES
    assert att_pairs == tiles_per_batch
    write_slot = lax.rem(lax.div(step, tiles_per_batch), 2)
    tile_in_batch = lax.rem(step, tiles_per_batch)
    lane = lax.broadcasted_iota(jnp.int32, (c, LANES), 1)
    row = lax.broadcasted_iota(jnp.int32, (c, LANES), 0)
    head1 = lane >= dh
    block_diag = (row >= dh) == head1
    XI, ZETA, G_CHUNK, DMAT = 0, 1, 2, 3

    @pl.when(step == 0)
    def _():
        zr_ref[...] = jnp.zeros_like(zr_ref)
        ring_ref[...] = jnp.zeros_like(ring_ref)
        rowf = row.astype(F32)
        diff = (row - lane).astype(F32)
        for p in range(pairs):
            lg = jnp.log(1.0 - jnp.exp2(-5.0 - (HEAD_PAIR * p + head1.astype(F32))))
            const_ref[p, XI] = jnp.exp((rowf + 1.0) * lg)
            const_ref[p, ZETA] = jnp.exp((c - 1.0 - rowf) * lg) * (dh ** -0.5)
            const_ref[p, G_CHUNK] = jnp.exp(c * lg)
            for j in range(HEAD_PAIR):
                lg_j = math.log(1.0 - 2.0 ** (-5.0 - (HEAD_PAIR * p + j)))
                const_ref[p, DMAT + j] = jnp.where(
                    diff >= 0, jnp.exp(jnp.maximum(diff, 0.0) * lg_j), 0.0) * (dh ** -0.5)

    @pl.when(lax.rem(jnp.maximum(step - 1, 0), tiles_per_batch) == 0)
    def _():
        r_ref[...] = jnp.zeros_like(r_ref)

    def head_mean(t):
        s0 = jnp.sum(jnp.where(head1, 0.0, t), axis=-1, keepdims=True)
        s1 = jnp.sum(jnp.where(head1, t, 0.0), axis=-1, keepdims=True)
        return jnp.where(head1, s1, s0) * (1.0 / dh)

    def retention_chunk(p, n):
        rows = slice(n * c, (n + 1) * c)
        col = lambda part: slice(part * RET_WIDTH + p * LANES, part * RET_WIDTH + (p + 1) * LANES)
        q, k, v = (zr_ref[rows, col(part)] for part in range(3))
        r_prev = r_ref[p]
        o = jnp.zeros((c, LANES), F32)
        for j in range(HEAD_PAIR):
            mine = head1 if j else jnp.logical_not(head1)
            qj = jnp.where(mine, q, jnp.zeros_like(q))
            s = lax.dot_general(qj, k, (((1,), (1,)), ((), ())), preferred_element_type=F32)
            s = s * const_ref[p, DMAT + j]
            vj = jnp.where(mine, v, jnp.zeros_like(v))
            o = o + jnp.dot(s.astype(BF16), vj, preferred_element_type=F32)
        qx = (q.astype(F32) * const_ref[p, XI]).astype(BF16)
        o = o + jnp.dot(qx, r_prev.astype(BF16), preferred_element_type=F32)
        kz = (k.astype(F32) * const_ref[p, ZETA]).T.astype(BF16)
        kv = jnp.dot(kz, v, preferred_element_type=F32)
        r_ref[p] = jnp.where(block_diag, const_ref[p, G_CHUNK] * r_prev + kv, 0.0)
        dlt = o - head_mean(o)
        var = head_mean(dlt * dlt)
        gate = zr_ref[rows, col(3)].astype(F32)
        y = dlt * lax.rsqrt(var + LN_EPS) * nw_ref[:, p * LANES:(p + 1) * LANES] * (gate * _sigmoid(gate))
        dst = (slice(n * c, (n + 1) * c), slice(p * LANES, (p + 1) * LANES))
        y_ref[dst] = jnp.where(step <= n_tiles, y.astype(BF16), y_ref[dst])

    xb = x_ref[...].astype(BF16)

    def project(lo):
        z = jnp.dot(xb, w_ref[:, lo:lo + IN_PROJ_COLS], preferred_element_type=F32).astype(BF16)
        hi = lo + IN_PROJ_COLS
        if hi <= SSM_WIDTH:
            u_ref[:, lo:hi] = z
        elif hi <= SSM_WIDTH + 4 * RET_WIDTH:
            znew_ref[:, lo - SSM_WIDTH:hi - SSM_WIDTH] = z
        else:
            first = (lo - SSM_WIDTH - 4 * RET_WIDTH) // LANES
            rows = pl.ds(pl.multiple_of(tile_in_batch * tm, tm), tm)
            for t in range(IN_PROJ_COLS // LANES):
                part, pair = divmod(first + t, att_pairs)
                ring_ref[write_slot, part, pair, rows, :] = z[:, t * LANES:(t + 1) * LANES]

    col_blocks = list(range(0, IN_WIDTH, IN_PROJ_COLS))
    units = [(p, n) for n in range(tm // c) for p in range(pairs)]
    work = []
    for k in range(max(len(col_blocks), len(units))):
        if k < len(col_blocks):
            work.append(functools.partial(project, col_blocks[k]))
        if k < len(units):
            work.append(functools.partial(retention_chunk, *units[k]))
    calls = [0]

    def between():
        calls[0] += 1
        if work and calls[0] % ATT_WORK_STRIDE == 0:
            work.pop(0)()

    read = lambda part: ring_ref.at[1 - write_slot, part, tile_in_batch]
    _attention_body(tile_in_batch, read(0), read(1), read(2), ya_ref, *attn_scratch, between)
    while work:
        work.pop(0)()
    zr_ref[...] = znew_ref[...]


def _in_proj_ret(x, w_in, ret_norm, layer):
    tm = TOKEN_TILE
    per_batch = SEQ // tm
    n_tiles = BATCH * per_batch
    n_dil = len(DILATED_BRANCHES) - 1
    att_pairs = ATT_WIDTH // LANES

    def tile(width, index):
        return pl.BlockSpec((None, tm, width),
                            lambda s: (lax.div(index(s), per_batch), lax.rem(index(s), per_batch), 0))

    cur = lambda width: tile(width, lambda s: jnp.minimum(s, n_tiles - 1))
    lag = lambda width: tile(width, lambda s: jnp.minimum(jnp.maximum(s - 1, 0), n_tiles - 1))
    att = pl.BlockSpec((None, SEQ, LANES), lambda s: (
        jnp.where(s < per_batch, BATCH, lax.div(s, per_batch) - 1), 0, lax.rem(s, per_batch)))
    return pl.pallas_call(
        _in_proj_ret_kernel,
        grid=(n_tiles + per_batch,),
        in_specs=[cur(D_MODEL),
                  _layer_block((D_MODEL, IN_WIDTH), layer, single_buffer=True),
                  _layer_block((1, RET_WIDTH), layer)],
        out_specs=[cur(SSM_WIDTH), lag(RET_WIDTH), att],
        out_shape=[jax.ShapeDtypeStruct((BATCH, SEQ, SSM_WIDTH), BF16),
                   jax.ShapeDtypeStruct((BATCH, SEQ, RET_WIDTH), BF16),
                   jax.ShapeDtypeStruct((BATCH + 1, SEQ, ATT_WIDTH), BF16)],
        scratch_shapes=[pltpu.VMEM((tm, 4 * RET_WIDTH), BF16),
                        pltpu.VMEM((tm, 4 * RET_WIDTH), BF16),
                        pltpu.VMEM((RET_WIDTH // LANES, LANES, LANES), F32),
                        pltpu.VMEM((RET_WIDTH // LANES, 3 + HEAD_PAIR, RET_CHUNK, LANES), F32),
                        pltpu.VMEM((2, 3, att_pairs, SEQ, LANES), BF16),
                        pltpu.VMEM((2, SEQ, LANES), F32),
                        pltpu.VMEM((n_dil + 1, HEAD_PAIR, SEQ, LANES), BF16),
                        pltpu.VMEM((n_dil, SEQ, LANES), BF16),
                        pltpu.VMEM((n_dil, SEQ, LANES), BF16),
                        pltpu.VMEM((n_dil + 1, SEQ, LANES), F32),
                        pltpu.VMEM((n_dil + 1, SEQ, LANES), F32),
                        pltpu.VMEM((n_dil + 1, SEQ, LANES), F32),
                        pltpu.VMEM((HEAD_PAIR * ATT_BLOCK, 2 * ATT_BLOCK), F32),
                        pltpu.VMEM((HEAD_PAIR * ATT_BLOCK, ATT_BLOCK), F32),
                        pltpu.VMEM((SEQ // ATT_BLOCK, HEAD_PAIR * ATT_BLOCK, 2 * ATT_BLOCK), BF16),
                        pltpu.VMEM((SEQ // ATT_BLOCK, HEAD_PAIR * ATT_BLOCK, 2 * ATT_BLOCK), F32)],
        compiler_params=_params("arbitrary"),
        name="in_proj_retention_attention",
    )(x, w_in, ret_norm)


def _ssm_kernel(u_ref, wb_ref, are_ref, aim_ref, wc_ref, d_ref, wglu_ref, bglu_ref, nw_ref,
                y_ref, bu_ref, st_ref, ut_ref, stage_ref):
    n_st = SSM_STATES
    bh = SSM_BATCH_HALF
    lt = SSM_TIME_TILE
    pitch = SSM_STAGE_PITCH
    lane_slabs = SSM_WIDTH // LANES
    chunks = [(slice(c, c + SSM_COL_CHUNK), slice(n_st + c, n_st + c + SSM_COL_CHUNK))
              for c in range(0, n_st, SSM_COL_CHUNK)]

    @pl.when(pl.program_id(0) == 0)
    def _():
        st_ref[...] = jnp.zeros_like(st_ref)

    def stage_rows(h, t):
        return pl.ds(h * bh * pitch + t, bh, stride=pitch)

    def gather(h):
        for b in range(h * bh, (h + 1) * bh):
            for s in range(lane_slabs):
                stage_ref[s, b * pitch:b * pitch + lt, :] = (
                    u_ref[b, :, s * LANES:(s + 1) * LANES].astype(F32))
        for t in range(0, lt, 2):
            for s in range(lane_slabs):
                pair = jnp.concatenate([stage_ref.at[s][stage_rows(h, t + k), :] for k in range(2)], axis=0)
                ut_ref[h, t * bh:(t + 2) * bh, s * LANES:(s + 1) * LANES] = pair.astype(BF16)

    def scatter(h, y):
        for t in range(lt):
            for s in range(lane_slabs):
                stage_ref.at[s][stage_rows(h, t), :] = y[t * bh:(t + 1) * bh, s * LANES:(s + 1) * LANES]
        for b in range(h * bh, (h + 1) * bh):
            for s in range(lane_slabs):
                y_ref[b, :, s * LANES:(s + 1) * LANES] = (
                    stage_ref[s, b * pitch:b * pitch + lt, :].astype(BF16))

    def project(h, cols):
        for sl in cols:
            bu_ref[h, :, sl] = jnp.dot(ut_ref[h], wb_ref[:, sl], preferred_element_type=F32)

    def scan(h, cols):
        re, im = cols
        ar = jnp.broadcast_to(are_ref[:, re], (bh, SSM_COL_CHUNK))
        ai = jnp.broadcast_to(aim_ref[:, re], (bh, SSM_COL_CHUNK))
        xr, xi = st_ref[h, :, re], st_ref[h, :, im]
        for t in range(SSM_TIME_TILE):
            rows = slice(t * bh, (t + 1) * bh)
            xr, xi = (ar * xr - ai * xi + bu_ref[h, rows, re],
                      ar * xi + ai * xr + bu_ref[h, rows, im])
            bu_ref[h, rows, re] = xr
            bu_ref[h, rows, im] = xi
        st_ref[h, :, re] = xr
        st_ref[h, :, im] = xi

    def readout(h, cols, acc):
        for sl in cols:
            acc = acc + jnp.dot(bu_ref[h, :, sl].astype(BF16), wc_ref[sl, :], preferred_element_type=F32)
        return acc

    def finish(h, y):
        y = y + d_ref[...] * ut_ref[h].astype(F32)
        cdf = 0.5 * (1.0 + jnp.tanh(math.sqrt(2.0 / math.pi) * (y + 0.044715 * (y * y * y))))
        g = y * cdf
        gate = jnp.dot(g.astype(BF16), wglu_ref[...], preferred_element_type=F32) + bglu_ref[...]
        out = g * _sigmoid(gate)
        ms = jnp.mean(out * out, axis=-1, keepdims=True)
        scatter(h, out * lax.rsqrt(ms + LN_EPS) * nw_ref[...])

    zero = jnp.zeros((SSM_TIME_TILE * bh, SSM_WIDTH), F32)
    gather(0)
    for k, cols in enumerate(chunks):
        project(0, cols)
        if k == 0:
            gather(1)
    for cols in chunks:
        project(1, cols)
        scan(0, cols)
    y0 = zero
    for cols in chunks:
        y0 = readout(0, cols, y0)
        scan(1, cols)
    y1 = zero
    for k, cols in enumerate(chunks):
        y1 = readout(1, cols, y1)
        if k == 0:
            finish(0, y0)
    finish(1, y1)


def _ssm(u, wb, a_re, a_im, wc, d_skip, w_glu, b_glu, norm_w, layer):
    rows = SSM_TIME_TILE * SSM_BATCH_HALF
    full = lambda shape: _layer_block(shape, layer)
    steps = pl.BlockSpec((BATCH, SSM_TIME_TILE, SSM_WIDTH), lambda i: (0, i, 0))
    return pl.pallas_call(
        _ssm_kernel,
        grid=(SEQ // SSM_TIME_TILE,),
        in_specs=[steps,
                  full((SSM_WIDTH, 2 * SSM_STATES)), full((1, SSM_STATES)), full((1, SSM_STATES)),
                  full((2 * SSM_STATES, SSM_WIDTH)), full((1, SSM_WIDTH)),
                  full((SSM_WIDTH, SSM_WIDTH)), full((1, SSM_WIDTH)), full((1, SSM_WIDTH))],
        out_specs=steps,
        out_shape=jax.ShapeDtypeStruct((BATCH, SEQ, SSM_WIDTH), BF16),
        scratch_shapes=[pltpu.VMEM((2, rows, 2 * SSM_STATES), F32),
                        pltpu.VMEM((2, SSM_BATCH_HALF, 2 * SSM_STATES), F32),
                        pltpu.VMEM((2, rows, SSM_WIDTH), BF16),
                        pltpu.VMEM((SSM_WIDTH // LANES, BATCH * SSM_STAGE_PITCH, LANES), F32)],
        compiler_params=_params("arbitrary"),
        name="s5_mixer",
    )(u, wb, a_re, a_im, wc, d_skip, w_glu, b_glu, norm_w)


def _ssm_weights(lam_re, lam_im, b_re, b_im, c_re, c_im, log_dt):
    g, p, h = SSM_GROUPS, SSM_STATE, SSM_GROUP
    lr, li = lam_re.astype(F32), lam_im.astype(F32)
    dt = jnp.exp(log_dt.astype(F32))[:, None]
    mag = jnp.exp(lr * dt)
    a_re, a_im = mag * jnp.cos(li * dt), mag * jnp.sin(li * dt)
    den = lr * lr + li * li
    nr, ni = a_re - 1.0, a_im
    f_re = ((nr * lr + ni * li) / den)[..., None]
    f_im = ((ni * lr - nr * li) / den)[..., None]
    br, bi = b_re.astype(F32), b_im.astype(F32)
    bb_re = f_re * br - f_im * bi
    bb_im = f_re * bi + f_im * br
    eye = jnp.eye(g, dtype=F32)
    embed_b = lambda bb: jnp.einsum('gph,gk->ghkp', bb, eye).reshape(g * h, g * p)
    embed_c = lambda cc: jnp.einsum('ghp,gk->gpkh', cc, eye).reshape(g * p, g * h)
    wb = jnp.concatenate([embed_b(bb_re), embed_b(bb_im)], axis=1).astype(BF16)
    wc = jnp.concatenate([embed_c(c_re.astype(F32)), -embed_c(c_im.astype(F32))], axis=0).astype(BF16)
    return wb, a_re.reshape(1, g * p), a_im.reshape(1, g * p), wc


def _attention_body(hp, q_ref, k_ref, v_ref, y_ref, src_ref, dq_ref, dk_ref, dv_ref,
                    oacc_ref, macc_ref, lacc_ref, bias_ref, bias0_ref, p_ref, s_ref, between):
    qb = ATT_BLOCK
    dh = ATT_HEAD_DIM
    d_home = DILATED_BRANCHES[ATT_HOME_BRANCH][1]
    head1 = lax.broadcasted_iota(jnp.int32, (qb, LANES), 1) >= dh

    def emit_q(bi, rows, blk):
        other = lax.broadcasted_iota(jnp.int32, blk.shape, 1) >= dh
        blk = blk * (dh ** -0.5)
        dq_ref[bi, 0, rows, :] = jnp.where(other, 0.0, blk).astype(BF16)
        dq_ref[bi, 1, rows, :] = jnp.where(other, blk, 0.0).astype(BF16)

    def emit_k(bi, rows, blk):
        if bi > 0:
            dk_ref[bi - 1, rows, :] = blk.astype(BF16)

    def emit_v(bi, rows, blk):
        if bi > 0:
            dv_ref[bi - 1, rows, :] = blk.astype(BF16)

    piece = PREP_ROWS
    for src, emit in ((q_ref, emit_q), (k_ref, emit_k), (v_ref, emit_v)):
        for c0 in range(0, SEQ, piece):
            rows = slice(c0, c0 + piece)
            natural = src[rows, :].astype(F32)
            src_ref[0, rows, :] = natural
            emit(0, rows, natural)
        d_prev = 1
        for bi in range(1, len(DILATED_BRANCHES)):
            d = DILATED_BRANCHES[bi][1]
            step = d // d_prev
            sub_prev, sub = SEQ // d_prev, SEQ // d
            cur, nxt = (bi - 1) % 2, bi % 2
            for r_prev in range(d_prev):
                for t in range(step):
                    r = r_prev + d_prev * t
                    for c0 in range(0, sub, piece):
                        n = min(piece, sub - c0)
                        rows = slice(r * sub + c0, r * sub + c0 + n)
                        blk = src_ref.at[cur][pl.ds(r_prev * sub_prev + t + step * c0, n, stride=step), :]
                        emit(bi, rows, blk)
                        if bi + 1 < len(DILATED_BRANCHES):
                            src_ref[nxt, rows, :] = blk
            d_prev = d

    row2 = lax.broadcasted_iota(jnp.int32, (HEAD_PAIR * qb, 2 * qb), 0)
    key2 = lax.broadcasted_iota(jnp.int32, (HEAD_PAIR * qb, 2 * qb), 1)
    row1 = lax.broadcasted_iota(jnp.int32, (HEAD_PAIR * qb, qb), 0)
    key1 = lax.broadcasted_iota(jnp.int32, (HEAD_PAIR * qb, qb), 1)

    def head_slope(row):
        head = (HEAD_PAIR * hp).astype(F32) + (row >= qb).astype(F32)
        return jnp.exp2(-(8.0 / ATT_HEADS) * (head + 1.0))

    dist2 = qb + jnp.where(row2 >= qb, row2 - qb, row2) - key2
    dist1 = jnp.where(row1 >= qb, row1 - qb, row1) - key1
    slope2, slope1 = head_slope(row2), head_slope(row1)

    def aligned(x):
        return x if isinstance(x, int) else pl.multiple_of(x, qb)

    for bi, (window, d) in enumerate(DILATED_BRANCHES):
        sub = SEQ // d
        nb = sub // qb
        ksrc = k_ref if bi == 0 else dk_ref.at[bi - 1]
        vsrc = v_ref if bi == 0 else dv_ref.at[bi - 1]
        bias_ref[...] = jnp.where((dist2 >= 0) & (dist2 <= window // d),
                                  -slope2 * (d * dist2).astype(F32), MASKED_SCORE)
        bias0_ref[...] = jnp.where(dist1 >= 0, -slope1 * (d * dist1).astype(F32), MASKED_SCORE)

        def rows_of(blk, d=d):
            base, first_token, has_prev, _ = blk
            rows = pl.ds(aligned(base), qb)
            krows = pl.ds(aligned(base - qb), 2 * qb) if has_prev else rows
            if d == 1 or d == d_home:
                out_rows = rows
            else:
                ft = first_token
                home = (ft % d_home) * (SEQ // d_home) + ft // d_home if isinstance(ft, int) else (
                    lax.rem(ft, d_home) * (SEQ // d_home) + lax.div(ft, d_home))
                out_rows = pl.ds(home, qb, stride=d // d_home)
            return rows, krows, out_rows, (2 * qb if has_prev else qb)

        def scores(blk, bi=bi, ksrc=ksrc, rows_of=rows_of):
            rows, krows, out_rows, nk = rows_of(blk)
            q2 = jnp.concatenate([dq_ref[bi, j, rows, :] for j in range(HEAD_PAIR)], axis=0)
            s_ref[blk[3], :, 0:nk] = lax.dot_general(
                q2, ksrc[krows, :], (((1,), (1,)), ((), ())), preferred_element_type=F32)

        def softmax(blk, bi=bi, rows_of=rows_of):
            rows, krows, out_rows, nk = rows_of(blk)
            s = s_ref[blk[3], :, 0:nk] + (bias_ref[...] if blk[2] else bias0_ref[...])
            m = jnp.max(s, axis=-1, keepdims=True)
            e = jnp.exp(s - m)
            l = jnp.sum(e, axis=-1, keepdims=True)
            p_ref[blk[3], :, 0:nk] = e.astype(BF16)
            packed = lambda c: jnp.where(head1, jnp.broadcast_to(c[qb:], (qb, LANES)),
                                         jnp.broadcast_to(c[:qb], (qb, LANES)))
            macc_ref.at[bi][out_rows, :] = packed(m)
            lacc_ref.at[bi][out_rows, :] = packed(l)

        def values(blk, bi=bi, vsrc=vsrc, rows_of=rows_of):
            rows, krows, out_rows, nk = rows_of(blk)
            v = vsrc[krows, :]
            o = [jnp.dot(p_ref[blk[3], j * qb:(j + 1) * qb, 0:nk], v, preferred_element_type=F32)
                 for j in range(HEAD_PAIR)]
            oacc_ref.at[bi][out_rows, :] = jnp.where(head1, o[1], o[0])

        def pipelined(n_groups, group, stages=(scores, softmax, values)):
            def step(t, valid):
                for k in reversed(range(len(stages))):
                    if valid(t - k):
                        for blk in group(t - k):
                            stages[k](blk)
                            between()

            for t in range(n_groups + len(stages) - 1):
                step(t, lambda g: 0 <= g < n_groups)

        fu = min(d, ATT_FIRST_UNROLL)
        pipelined(d // fu, lambda it, fu=fu, sub=sub: [
            ((it * fu + u) * sub, it * fu + u, False, it * fu + u) for u in range(fu)])

        if nb > 1:
            lu = ATT_LATER_UNROLL
            per_res = (nb - 1) // lu
            assert per_res * lu == nb - 1

            def later_group(it, lu=lu, per_res=per_res, sub=sub, d=d):
                if per_res == 1:
                    r, n0 = it, 1
                elif d == 1:
                    r, n0 = 0, it * lu + 1
                else:
                    r, n0 = it // per_res, (it % per_res) * lu + 1
                return [(r * sub + (n0 + u) * qb, r + d * qb * (n0 + u), True, d + it * lu + u)
                        for u in range(lu)]

            pipelined(d * per_res, later_group)

    tiles_per_residue = SEQ // d_home // COMBINE_ROWS

    for tb in range(SEQ // COMBINE_ROWS):
        rows = slice(tb * COMBINE_ROWS, (tb + 1) * COMBINE_ROWS)
        tokens = pl.ds(tb // tiles_per_residue + d_home * COMBINE_ROWS * (tb % tiles_per_residue),
                       COMBINE_ROWS, stride=d_home)
        at = lambda ref, bi: ref.at[bi][tokens if DILATED_BRANCHES[bi][1] == 1 else rows, :]
        ms = [at(macc_ref, bi) for bi in range(len(DILATED_BRANCHES))]
        top = functools.reduce(jnp.maximum, ms)
        ws = [jnp.exp(m - top) for m in ms]
        num = sum(w * at(oacc_ref, bi) for bi, w in enumerate(ws))
        den = sum(w * at(lacc_ref, bi) for bi, w in enumerate(ws))
        src_ref.at[0][tokens, :] = num / den
        between()
    for c0 in range(0, SEQ, COMBINE_ROWS):
        y_ref[c0:c0 + COMBINE_ROWS, :] = src_ref[0, c0:c0 + COMBINE_ROWS, :].astype(BF16)


def _post_kernel(ys_ref, yr_ref, ya_ref, x_ref, wo_ref, an_ref, l1w_ref, l1b_ref,
                 w1_ref, w2_ref, l2w_ref, l2b_ref, o_ref, x1_ref, acc_ref):
    part = POST_TILE // POST_PARTS
    n_chunks = D_FF // FF_CHUNK
    slab = part // n_chunks
    parts = [slice(r0, r0 + part) for r0 in range(0, POST_TILE, part)]

    def out_proj(rows):
        ya = ya_ref[rows, :].astype(F32)
        ms = jnp.mean(ya * ya, axis=-1, keepdims=True)
        ya = (ya * lax.rsqrt(ms + LN_EPS) * an_ref[...]).astype(BF16)
        h = jnp.dot(ys_ref[rows, :], wo_ref[0:SSM_WIDTH, :], preferred_element_type=F32)
        h = h + jnp.dot(yr_ref[rows, :], wo_ref[SSM_WIDTH:SSM_WIDTH + RET_WIDTH, :],
                        preferred_element_type=F32)
        h = h + jnp.dot(ya, wo_ref[SSM_WIDTH + RET_WIDTH:, :], preferred_element_type=F32)
        o_ref[rows, :] = DEEPNORM_ALPHA * x_ref[rows, :] + h

    def ln1(rows):
        x1_ref[rows, :] = _layer_norm(o_ref[rows, :], l1w_ref[...], l1b_ref[...])

    def ln2(rows):
        o_ref[rows, :] = _layer_norm(DEEPNORM_ALPHA * x1_ref[rows, :] + acc_ref[rows, :],
                                     l2w_ref[...], l2b_ref[...])

    def mlp(rows, between):
        xb = x1_ref[rows, :].astype(BF16)
        acc = jnp.zeros((part, D_MODEL), F32)
        for k in range(n_chunks):
            c = k * FF_CHUNK
            hid = jnp.dot(xb, w1_ref[:, c:c + FF_CHUNK], preferred_element_type=F32)
            hid = jnp.square(jnp.maximum(hid, 0.0)).astype(BF16)
            acc = acc + jnp.dot(hid, w2_ref[c:c + FF_CHUNK, :], preferred_element_type=F32)
            between(k)
        acc_ref[rows, :] = acc

    slab_of = lambda rows, k: slice(rows.start + k * slab, rows.start + (k + 1) * slab)
    for rows in parts:
        out_proj(rows)
    ln1(parts[0])
    for i, rows in enumerate(parts):
        def between(k, i=i):
            if i + 1 < len(parts):
                ln1(slab_of(parts[i + 1], k))
            if i > 0:
                ln2(slab_of(parts[i - 1], k))
        mlp(rows, between)
    ln2(parts[-1])


def _post(y_ssm, y_ret, y_att, x, w_out, attn_norm, ln1_w, ln1_b, w1, w2, ln2_w, ln2_b, layer):
    tm = POST_TILE
    row = lambda width: pl.BlockSpec((None, tm, width), lambda b, i: (b, i, 0))
    full = lambda shape: _layer_block(shape, layer, single_buffer=True)
    return pl.pallas_call(
        _post_kernel,
        grid=(BATCH, SEQ // tm),
        in_specs=[row(SSM_WIDTH), row(RET_WIDTH), row(ATT_WIDTH), row(D_MODEL),
                  full((D_MODEL, D_MODEL)), full((1, ATT_WIDTH)), full((1, D_MODEL)), full((1, D_MODEL)),
                  full((D_MODEL, D_FF)), full((D_FF, D_MODEL)), full((1, D_MODEL)), full((1, D_MODEL))],
        out_specs=row(D_MODEL),
        out_shape=jax.ShapeDtypeStruct((BATCH, SEQ, D_MODEL), F32),
        scratch_shapes=[pltpu.VMEM((tm, D_MODEL), F32), pltpu.VMEM((tm, D_MODEL), F32)],
        compiler_params=_params("parallel", "parallel"),
        name="out_proj_mlp",
    )(y_ssm, y_ret, y_att, x, w_out, attn_norm, ln1_w, ln1_b, w1, w2, ln2_w, ln2_b)


def kernel(x, w_in, ssm_lambda_re, ssm_lambda_im, ssm_b_re, ssm_b_im, ssm_c_re, ssm_c_im, ssm_d, ssm_log_dt, ssm_w_glu, ssm_b_glu, ssm_out_norm, ret_out_norm, attn_out_norm, w_out, ln1_w, ln1_b, mlp_w1, mlp_w2, ln2_w, ln2_b):
    vec = lambda p: p.astype(F32)[:, None, :]
    bf16 = lambda p: p.astype(BF16)
    wb, a_re, a_im, wc = jax.vmap(_ssm_weights)(ssm_lambda_re, ssm_lambda_im, ssm_b_re, ssm_b_im,
                                                ssm_c_re, ssm_c_im, ssm_log_dt)
    w_in, ssm_w_glu, w_out, mlp_w1, mlp_w2 = map(bf16, (w_in, ssm_w_glu, w_out, mlp_w1, mlp_w2))
    (ssm_d, ssm_b_glu, ssm_out_norm, ret_out_norm, attn_out_norm, ln1_w, ln1_b, ln2_w, ln2_b) = map(
        vec, (ssm_d, ssm_b_glu, ssm_out_norm, ret_out_norm, attn_out_norm, ln1_w, ln1_b, ln2_w, ln2_b))
    x = x.astype(F32)
    for i in range(DEPTH):
        u, y_ret, y_att = _in_proj_ret(x, w_in, ret_out_norm, i)
        y_ssm = _ssm(u, wb, a_re, a_im, wc, ssm_d, ssm_w_glu, ssm_b_glu, ssm_out_norm, i)
        x = _post(y_ssm, y_ret, y_att, x, w_out, attn_out_norm, ln1_w, ln1_b, mlp_w1, mlp_w2, ln2_w, ln2_b, i)
    return x
```

```python
import functools
import math

import jax
import jax.numpy as jnp
from jax import lax
from jax.experimental import pallas as pl
from jax.experimental.pallas import tpu as pltpu

F32 = jnp.float32
BF16 = jnp.bfloat16

D_MODEL = 1024
BATCH = 16
SEQ = 2048
DEPTH = 2
SSM_WIDTH = 256
SSM_GROUP = 16
SSM_GROUPS = 16
SSM_STATE = 64
SSM_STATES = SSM_GROUPS * SSM_STATE
RET_HEAD_DIM = 64
RET_WIDTH = 256
RET_CHUNK = 128
ATT_HEAD_DIM = 64
ATT_WIDTH = 512
ATT_HEADS = 8
DILATED_BRANCHES = ((128, 1), (512, 4), (2048, 16))
ATT_BLOCK = 128
IN_WIDTH = SSM_WIDTH + 4 * RET_WIDTH + 3 * ATT_WIDTH
D_FF = 4 * D_MODEL
DEEPNORM_ALPHA = (2 * DEPTH) ** 0.25
LN_EPS = 1e-5

LANES = 128
HEAD_PAIR = LANES // ATT_HEAD_DIM
VMEM_LIMIT_BYTES = 56 * 1024 * 1024
MASKED_SCORE = -1e30

TOKEN_TILE = 512
IN_PROJ_COLS = 256
POST_TILE = 1024
POST_PARTS = 4
SSM_TIME_TILE = 64
SSM_BATCH_HALF = BATCH // 2
SSM_COL_CHUNK = 512
SSM_STAGE_PITCH = SSM_TIME_TILE + 8
FF_CHUNK = 1024
COMBINE_ROWS = 256
ATT_WORK_STRIDE = 7
ATT_HOME_BRANCH = 1
PREP_ROWS = 128
ATT_FIRST_UNROLL = 4
ATT_LATER_UNROLL = 3

assert RET_HEAD_DIM == ATT_HEAD_DIM and HEAD_PAIR == 2
assert all(w // d == ATT_BLOCK for w, d in DILATED_BRANCHES)
assert DILATED_BRANCHES[0][1] == 1 and all(
    b[1] % a[1] == 0 for a, b in zip(DILATED_BRANCHES, DILATED_BRANCHES[1:]))


def _params(*semantics):
    return pltpu.CompilerParams(dimension_semantics=semantics, vmem_limit_bytes=VMEM_LIMIT_BYTES)


def _layer_block(shape, layer, single_buffer=False):
    mode = dict(pipeline_mode=pl.Buffered(1)) if single_buffer else {}
    return pl.BlockSpec((None,) + tuple(shape), lambda *_: (layer,) + (0,) * len(shape), **mode)


def _sigmoid(x):
    return 1.0 / (1.0 + jnp.exp(-x))


def _layer_norm(r, w, b):
    mu = jnp.mean(r, axis=-1, keepdims=True)
    d = r - mu
    var = jnp.mean(d * d, axis=-1, keepdims=True)
    return d * lax.rsqrt(var + LN_EPS) * w + b


def _in_proj_ret_kernel(x_ref, w_ref, nw_ref, u_ref, y_ref, ya_ref, zr_ref, znew_ref, r_ref, const_ref,
                        ring_ref, *attn_scratch):
    c = RET_CHUNK
    dh = RET_HEAD_DIM
    pairs = RET_WIDTH // LANES
    tm = TOKEN_TILE
    step = pl.program_id(0)
    tiles_per_batch = SEQ // tm
    n_tiles = BATCH * tiles_per_batch
    att_pairs = ATT_WIDTH // LANES
    assert att_pairs == tiles_per_batch
    write_slot = lax.rem(lax.div(step, tiles_per_batch), 2)
    tile_in_batch = lax.rem(step, tiles_per_batch)
    lane = lax.broadcasted_iota(jnp.int32, (c, LANES), 1)
    row = lax.broadcasted_iota(jnp.int32, (c, LANES), 0)
    head1 = lane >= dh
    block_diag = (row >= dh) == head1
    XI, ZETA, G_CHUNK, DMAT = 0, 1, 2, 3

    @pl.when(step == 0)
    def _():
        zr_ref[...] = jnp.zeros_like(zr_ref)
        rowf = row.astype(F32)
        diff = (row - lane).astype(F32)
        for p in range(pairs):
            lg = jnp.log(1.0 - jnp.exp2(-5.0 - (HEAD_PAIR * p + head1.astype(F32))))
            const_ref[p, XI] = jnp.exp((rowf + 1.0) * lg)
            const_ref[p, ZETA] = jnp.exp((c - 1.0 - rowf) * lg) * (dh ** -0.5)
            const_ref[p, G_CHUNK] = jnp.exp(c * lg)
            for j in range(HEAD_PAIR):
                lg_j = math.log(1.0 - 2.0 ** (-5.0 - (HEAD_PAIR * p + j)))
                const_ref[p, DMAT + j] = jnp.where(
                    diff >= 0, jnp.exp(jnp.maximum(diff, 0.0) * lg_j), 0.0) * (dh ** -0.5)

    @pl.when(lax.rem(jnp.maximum(step - 1, 0), tiles_per_batch) == 0)
    def _():
        r_ref[...] = jnp.zeros_like(r_ref)

    def head_mean(t):
        s0 = jnp.sum(jnp.where(head1, 0.0, t), axis=-1, keepdims=True)
        s1 = jnp.sum(jnp.where(head1, t, 0.0), axis=-1, keepdims=True)
        return jnp.where(head1, s1, s0) * (1.0 / dh)

    def retention_chunk(p, n):
        rows = slice(n * c, (n + 1) * c)
        col = lambda part: slice(part * RET_WIDTH + p * LANES, part * RET_WIDTH + (p + 1) * LANES)
        q, k, v = (zr_ref[rows, col(part)] for part in range(3))
        r_prev = r_ref[p]
        o = jnp.zeros((c, LANES), F32)
        for j in range(HEAD_PAIR):
            mine = head1 if j else jnp.logical_not(head1)
            qj = jnp.where(mine, q, jnp.zeros_like(q))
            s = lax.dot_general(qj, k, (((1,), (1,)), ((), ())), preferred_element_type=F32)
            s = s * const_ref[p, DMAT + j]
            vj = jnp.where(mine, v, jnp.zeros_like(v))
            o = o + jnp.dot(s.astype(BF16), vj, preferred_element_type=F32)
        qx = (q.astype(F32) * const_ref[p, XI]).astype(BF16)
        o = o + jnp.dot(qx, r_prev.astype(BF16), preferred_element_type=F32)
        kz = (k.astype(F32) * const_ref[p, ZETA]).T.astype(BF16)
        kv = jnp.dot(kz, v, preferred_element_type=F32)
        r_ref[p] = jnp.where(block_diag, const_ref[p, G_CHUNK] * r_prev + kv, 0.0)
        dlt = o - head_mean(o)
        var = head_mean(dlt * dlt)
        gate = zr_ref[rows, col(3)].astype(F32)
        y = dlt * lax.rsqrt(var + LN_EPS) * nw_ref[:, p * LANES:(p + 1) * LANES] * (gate * _sigmoid(gate))
        dst = (slice(n * c, (n + 1) * c), slice(p * LANES, (p + 1) * LANES))
        y_ref[dst] = jnp.where(step <= n_tiles, y.astype(BF16), y_ref[dst])

    def run_step(with_attention):
        xb = x_ref[...].astype(BF16)

        def project(lo):
            z = jnp.dot(xb, w_ref[:, lo:lo + IN_PROJ_COLS], preferred_element_type=F32).astype(BF16)
            hi = lo + IN_PROJ_COLS
            if hi <= SSM_WIDTH:
                u_ref[:, lo:hi] = z
            elif hi <= SSM_WIDTH + 4 * RET_WIDTH:
                znew_ref[:, lo - SSM_WIDTH:hi - SSM_WIDTH] = z
            else:
                first = (lo - SSM_WIDTH - 4 * RET_WIDTH) // LANES
                rows = pl.ds(pl.multiple_of(tile_in_batch * tm, tm), tm)
                for t in range(IN_PROJ_COLS // LANES):
                    part, pair = divmod(first + t, att_pairs)
                    ring_ref[write_slot, part, pair, rows, :] = z[:, t * LANES:(t + 1) * LANES]

        col_blocks = list(range(0, IN_WIDTH, IN_PROJ_COLS))
        units = [(p, n) for n in range(tm // c) for p in range(pairs)]
        work = []
        for k in range(max(len(col_blocks), len(units))):
            if k < len(col_blocks):
                work.append(functools.partial(project, col_blocks[k]))
            if k < len(units):
                work.append(functools.partial(retention_chunk, *units[k]))
        if with_attention:
            calls = [0]

            def between():
                calls[0] += 1
                if work and calls[0] % ATT_WORK_STRIDE == 0:
                    work.pop(0)()

            read = lambda part: ring_ref.at[1 - write_slot, part, tile_in_batch]
            _attention_body(tile_in_batch, read(0), read(1), read(2), ya_ref, *attn_scratch, between)
        else:
            ya_ref[...] = jnp.zeros_like(ya_ref)
        while work:
            work.pop(0)()
        zr_ref[...] = znew_ref[...]

    pl.when(step < tiles_per_batch)(lambda: run_step(False))
    pl.when(step >= tiles_per_batch)(lambda: run_step(True))


def _in_proj_ret(x, w_in, ret_norm, layer):
    tm = TOKEN_TILE
    per_batch = SEQ // tm
    n_tiles = BATCH * per_batch
    n_dil = len(DILATED_BRANCHES) - 1
    att_pairs = ATT_WIDTH // LANES

    def tile(width, index):
        return pl.BlockSpec((None, tm, width),
                            lambda s: (lax.div(index(s), per_batch), lax.rem(index(s), per_batch), 0))

    cur = lambda width: tile(width, lambda s: jnp.minimum(s, n_tiles - 1))
    lag = lambda width: tile(width, lambda s: jnp.minimum(jnp.maximum(s - 1, 0), n_tiles - 1))
    att = pl.BlockSpec((None, SEQ, LANES), lambda s: (
        jnp.where(s < per_batch, BATCH, lax.div(s, per_batch) - 1), 0, lax.rem(s, per_batch)))
    return pl.pallas_call(
        _in_proj_ret_kernel,
        grid=(n_tiles + per_batch,),
        in_specs=[cur(D_MODEL),
                  _layer_block((D_MODEL, IN_WIDTH), layer, single_buffer=True),
                  _layer_block((1, RET_WIDTH), layer)],
        out_specs=[cur(SSM_WIDTH), lag(RET_WIDTH), att],
        out_shape=[jax.ShapeDtypeStruct((BATCH, SEQ, SSM_WIDTH), BF16),
                   jax.ShapeDtypeStruct((BATCH, SEQ, RET_WIDTH), BF16),
                   jax.ShapeDtypeStruct((BATCH + 1, SEQ, ATT_WIDTH), BF16)],
        scratch_shapes=[pltpu.VMEM((tm, 4 * RET_WIDTH), BF16),
                        pltpu.VMEM((tm, 4 * RET_WIDTH), BF16),
                        pltpu.VMEM((RET_WIDTH // LANES, LANES, LANES), F32),
                        pltpu.VMEM((RET_WIDTH // LANES, 3 + HEAD_PAIR, RET_CHUNK, LANES), F32),
                        pltpu.VMEM((2, 3, att_pairs, SEQ, LANES), BF16),
                        pltpu.VMEM((2, SEQ, LANES), F32),
                        pltpu.VMEM((n_dil + 1, HEAD_PAIR, SEQ, LANES), BF16),
                        pltpu.VMEM((n_dil, SEQ, LANES), BF16),
                        pltpu.VMEM((n_dil, SEQ, LANES), BF16),
                        pltpu.VMEM((n_dil + 1, SEQ, LANES), F32),
                        pltpu.VMEM((n_dil + 1, SEQ, LANES), F32),
                        pltpu.VMEM((n_dil + 1, SEQ, LANES), F32),
                        pltpu.VMEM((HEAD_PAIR * ATT_BLOCK, 2 * ATT_BLOCK), F32),
                        pltpu.VMEM((HEAD_PAIR * ATT_BLOCK, ATT_BLOCK), F32),
                        pltpu.VMEM((SEQ // ATT_BLOCK, HEAD_PAIR * ATT_BLOCK, 2 * ATT_BLOCK), BF16),
                        pltpu.VMEM((SEQ // ATT_BLOCK, HEAD_PAIR * ATT_BLOCK, 2 * ATT_BLOCK), F32)],
        compiler_params=_params("arbitrary"),
        name="in_proj_retention_attention",
    )(x, w_in, ret_norm)


def _ssm_kernel(u_ref, wb_ref, are_ref, aim_ref, wc_ref, d_ref, wglu_ref, bglu_ref, nw_ref,
                y_ref, bu_ref, st_ref, ut_ref, stage_ref):
    n_st = SSM_STATES
    bh = SSM_BATCH_HALF
    lt = SSM_TIME_TILE
    pitch = SSM_STAGE_PITCH
    lane_slabs = SSM_WIDTH // LANES
    chunks = [(slice(c, c + SSM_COL_CHUNK), slice(n_st + c, n_st + c + SSM_COL_CHUNK))
              for c in range(0, n_st, SSM_COL_CHUNK)]

    @pl.when(pl.program_id(0) == 0)
    def _():
        st_ref[...] = jnp.zeros_like(st_ref)

    def stage_rows(h, t):
        return pl.ds(h * bh * pitch + t, bh, stride=pitch)

    def gather(h):
        for b in range(h * bh, (h + 1) * bh):
            for s in range(lane_slabs):
                stage_ref[s, b * pitch:b * pitch + lt, :] = (
                    u_ref[b, :, s * LANES:(s + 1) * LANES].astype(F32))
        for t in range(0, lt, 2):
            for s in range(lane_slabs):
                pair = jnp.concatenate([stage_ref.at[s][stage_rows(h, t + k), :] for k in range(2)], axis=0)
                ut_ref[h, t * bh:(t + 2) * bh, s * LANES:(s + 1) * LANES] = pair.astype(BF16)

    def scatter(h, y):
        for t in range(lt):
            for s in range(lane_slabs):
                stage_ref.at[s][stage_rows(h, t), :] = y[t * bh:(t + 1) * bh, s * LANES:(s + 1) * LANES]
        for b in range(h * bh, (h + 1) * bh):
            for s in range(lane_slabs):
                y_ref[b, :, s * LANES:(s + 1) * LANES] = (
                    stage_ref[s, b * pitch:b * pitch + lt, :].astype(BF16))

    def project(h, cols):
        for sl in cols:
            bu_ref[h, :, sl] = jnp.dot(ut_ref[h], wb_ref[:, sl], preferred_element_type=F32)

    def scan(h, cols):
        re, im = cols
        ar = jnp.broadcast_to(are_ref[:, re], (bh, SSM_COL_CHUNK))
        ai = jnp.broadcast_to(aim_ref[:, re], (bh, SSM_COL_CHUNK))
        xr, xi = st_ref[h, :, re], st_ref[h, :, im]
        for t in range(SSM_TIME_TILE):
            rows = slice(t * bh, (t + 1) * bh)
            xr, xi = (ar * xr - ai * xi + bu_ref[h, rows, re],
                      ar * xi + ai * xr + bu_ref[h, rows, im])
            bu_ref[h, rows, re] = xr
            bu_ref[h, rows, im] = xi
        st_ref[h, :, re] = xr
        st_ref[h, :, im] = xi

    def readout(h, cols, acc):
        for sl in cols:
            acc = acc + jnp.dot(bu_ref[h, :, sl].astype(BF16), wc_ref[sl, :], preferred_element_type=F32)
        return acc

    def finish(h, y):
        y = y + d_ref[...] * ut_ref[h].astype(F32)
        cdf = 0.5 * (1.0 + jnp.tanh(math.sqrt(2.0 / math.pi) * (y + 0.044715 * (y * y * y))))
        g = y * cdf
        gate = jnp.dot(g.astype(BF16), wglu_ref[...], preferred_element_type=F32) + bglu_ref[...]
        out = g * _sigmoid(gate)
        ms = jnp.mean(out * out, axis=-1, keepdims=True)
        scatter(h, out * lax.rsqrt(ms + LN_EPS) * nw_ref[...])

    zero = jnp.zeros((SSM_TIME_TILE * bh, SSM_WIDTH), F32)
    gather(0)
    for k, cols in enumerate(chunks):
        project(0, cols)
        if k == 0:
            gather(1)
    for cols in chunks:
        project(1, cols)
        scan(0, cols)
    y0 = zero
    for cols in chunks:
        y0 = readout(0, cols, y0)
        scan(1, cols)
    y1 = zero
    for k, cols in enumerate(chunks):
        y1 = readout(1, cols, y1)
        if k == 0:
            finish(0, y0)
    finish(1, y1)


def _ssm(u, wb, a_re, a_im, wc, d_skip, w_glu, b_glu, norm_w, layer):
    rows = SSM_TIME_TILE * SSM_BATCH_HALF
    full = lambda shape: _layer_block(shape, layer)
    steps = pl.BlockSpec((BATCH, SSM_TIME_TILE, SSM_WIDTH), lambda i: (0, i, 0))
    return pl.pallas_call(
        _ssm_kernel,
        grid=(SEQ // SSM_TIME_TILE,),
        in_specs=[steps,
                  full((SSM_WIDTH, 2 * SSM_STATES)), full((1, SSM_STATES)), full((1, SSM_STATES)),
                  full((2 * SSM_STATES, SSM_WIDTH)), full((1, SSM_WIDTH)),
                  full((SSM_WIDTH, SSM_WIDTH)), full((1, SSM_WIDTH)), full((1, SSM_WIDTH))],
        out_specs=steps,
        out_shape=jax.ShapeDtypeStruct((BATCH, SEQ, SSM_WIDTH), BF16),
        scratch_shapes=[pltpu.VMEM((2, rows, 2 * SSM_STATES), F32),
                        pltpu.VMEM((2, SSM_BATCH_HALF, 2 * SSM_STATES), F32),
                        pltpu.VMEM((2, rows, SSM_WIDTH), BF16),
                        pltpu.VMEM((SSM_WIDTH // LANES, BATCH * SSM_STAGE_PITCH, LANES), F32)],
        compiler_params=_params("arbitrary"),
        name="s5_mixer",
    )(u, wb, a_re, a_im, wc, d_skip, w_glu, b_glu, norm_w)


def _ssm_weights(lam_re, lam_im, b_re, b_im, c_re, c_im, log_dt):
    g, p, h = SSM_GROUPS, SSM_STATE, SSM_GROUP
    lr, li = lam_re.astype(F32), lam_im.astype(F32)
    dt = jnp.exp(log_dt.astype(F32))[:, None]
    mag = jnp.exp(lr * dt)
    a_re, a_im = mag * jnp.cos(li * dt), mag * jnp.sin(li * dt)
    den = lr * lr + li * li
    nr, ni = a_re - 1.0, a_im
    f_re = ((nr * lr + ni * li) / den)[..., None]
    f_im = ((ni * lr - nr * li) / den)[..., None]
    br, bi = b_re.astype(F32), b_im.astype(F32)
    bb_re = f_re * br - f_im * bi
    bb_im = f_re * bi + f_im * br
    eye = jnp.eye(g, dtype=F32)
    embed_b = lambda bb: jnp.einsum('gph,gk->ghkp', bb, eye).reshape(g * h, g * p)
    embed_c = lambda cc: jnp.einsum('ghp,gk->gpkh', cc, eye).reshape(g * p, g * h)
    wb = jnp.concatenate([embed_b(bb_re), embed_b(bb_im)], axis=1).astype(BF16)
    wc = jnp.concatenate([embed_c(c_re.astype(F32)), -embed_c(c_im.astype(F32))], axis=0).astype(BF16)
    return wb, a_re.reshape(1, g * p), a_im.reshape(1, g * p), wc


def _attention_body(hp, q_ref, k_ref, v_ref, y_ref, src_ref, dq_ref, dk_ref, dv_ref,
                    oacc_ref, macc_ref, lacc_ref, bias_ref, bias0_ref, p_ref, s_ref, between):
    qb = ATT_BLOCK
    dh = ATT_HEAD_DIM
    d_home = DILATED_BRANCHES[ATT_HOME_BRANCH][1]
    head1 = lax.broadcasted_iota(jnp.int32, (qb, LANES), 1) >= dh

    def emit_q(bi, rows, blk):
        other = lax.broadcasted_iota(jnp.int32, blk.shape, 1) >= dh
        blk = blk * (dh ** -0.5)
        dq_ref[bi, 0, rows, :] = jnp.where(other, 0.0, blk).astype(BF16)
        dq_ref[bi, 1, rows, :] = jnp.where(other, blk, 0.0).astype(BF16)

    def emit_k(bi, rows, blk):
        if bi > 0:
            dk_ref[bi - 1, rows, :] = blk.astype(BF16)

    def emit_v(bi, rows, blk):
        if bi > 0:
            dv_ref[bi - 1, rows, :] = blk.astype(BF16)

    piece = PREP_ROWS
    for src, emit in ((q_ref, emit_q), (k_ref, emit_k), (v_ref, emit_v)):
        for c0 in range(0, SEQ, piece):
            rows = slice(c0, c0 + piece)
            natural = src[rows, :].astype(F32)
            src_ref[0, rows, :] = natural
            emit(0, rows, natural)
        d_prev = 1
        for bi in range(1, len(DILATED_BRANCHES)):
            d = DILATED_BRANCHES[bi][1]
            step = d // d_prev
            sub_prev, sub = SEQ // d_prev, SEQ // d
            cur, nxt = (bi - 1) % 2, bi % 2
            for r_prev in range(d_prev):
                for t in range(step):
                    r = r_prev + d_prev * t
                    for c0 in range(0, sub, piece):
                        n = min(piece, sub - c0)
                        rows = slice(r * sub + c0, r * sub + c0 + n)
                        blk = src_ref.at[cur][pl.ds(r_prev * sub_prev + t + step * c0, n, stride=step), :]
                        emit(bi, rows, blk)
                        if bi + 1 < len(DILATED_BRANCHES):
                            src_ref[nxt, rows, :] = blk
            d_prev = d

    row2 = lax.broadcasted_iota(jnp.int32, (HEAD_PAIR * qb, 2 * qb), 0)
    key2 = lax.broadcasted_iota(jnp.int32, (HEAD_PAIR * qb, 2 * qb), 1)
    row1 = lax.broadcasted_iota(jnp.int32, (HEAD_PAIR * qb, qb), 0)
    key1 = lax.broadcasted_iota(jnp.int32, (HEAD_PAIR * qb, qb), 1)

    def head_slope(row):
        head = (HEAD_PAIR * hp).astype(F32) + (row >= qb).astype(F32)
        return jnp.exp2(-(8.0 / ATT_HEADS) * (head + 1.0))

    dist2 = qb + jnp.where(row2 >= qb, row2 - qb, row2) - key2
    dist1 = jnp.where(row1 >= qb, row1 - qb, row1) - key1
    slope2, slope1 = head_slope(row2), head_slope(row1)

    def aligned(x):
        return x if isinstance(x, int) else pl.multiple_of(x, qb)

    for bi, (window, d) in enumerate(DILATED_BRANCHES):
        sub = SEQ // d
        nb = sub // qb
        ksrc = k_ref if bi == 0 else dk_ref.at[bi - 1]
        vsrc = v_ref if bi == 0 else dv_ref.at[bi - 1]
        bias_ref[...] = jnp.where((dist2 >= 0) & (dist2 <= window // d),
                                  -slope2 * (d * dist2).astype(F32), MASKED_SCORE)
        bias0_ref[...] = jnp.where(dist1 >= 0, -slope1 * (d * dist1).astype(F32), MASKED_SCORE)

        def rows_of(blk, d=d):
            base, first_token, has_prev, _ = blk
            rows = pl.ds(aligned(base), qb)
            krows = pl.ds(aligned(base - qb), 2 * qb) if has_prev else rows
            if d == 1 or d == d_home:
                out_rows = rows
            else:
                ft = first_token
                home = (ft % d_home) * (SEQ // d_home) + ft // d_home if isinstance(ft, int) else (
                    lax.rem(ft, d_home) * (SEQ // d_home) + lax.div(ft, d_home))
                out_rows = pl.ds(home, qb, stride=d // d_home)
            return rows, krows, out_rows, (2 * qb if has_prev else qb)

        def scores(blk, bi=bi, ksrc=ksrc, rows_of=rows_of):
            rows, krows, out_rows, nk = rows_of(blk)
            q2 = jnp.concatenate([dq_ref[bi, j, rows, :] for j in range(HEAD_PAIR)], axis=0)
            s_ref[blk[3], :, 0:nk] = lax.dot_general(
                q2, ksrc[krows, :], (((1,), (1,)), ((), ())), preferred_element_type=F32)

        def softmax(blk, bi=bi, rows_of=rows_of):
            rows, krows, out_rows, nk = rows_of(blk)
            s = s_ref[blk[3], :, 0:nk] + (bias_ref[...] if blk[2] else bias0_ref[...])
            m = jnp.max(s, axis=-1, keepdims=True)
            e = jnp.exp(s - m)
            l = jnp.sum(e, axis=-1, keepdims=True)
            p_ref[blk[3], :, 0:nk] = e.astype(BF16)
            packed = lambda c: jnp.where(head1, jnp.broadcast_to(c[qb:], (qb, LANES)),
                                         jnp.broadcast_to(c[:qb], (qb, LANES)))
            macc_ref.at[bi][out_rows, :] = packed(m)
            lacc_ref.at[bi][out_rows, :] = packed(l)

        def values(blk, bi=bi, vsrc=vsrc, rows_of=rows_of):
            rows, krows, out_rows, nk = rows_of(blk)
            v = vsrc[krows, :]
            o = [jnp.dot(p_ref[blk[3], j * qb:(j + 1) * qb, 0:nk], v, preferred_element_type=F32)
                 for j in range(HEAD_PAIR)]
            oacc_ref.at[bi][out_rows, :] = jnp.where(head1, o[1], o[0])

        def pipelined(n_groups, group, stages=(scores, softmax, values)):
            def step(t, valid):
                for k in reversed(range(len(stages))):
                    if valid(t - k):
                        for blk in group(t - k):
                            stages[k](blk)
                            between()

            for t in range(n_groups + len(stages) - 1):
                step(t, lambda g: 0 <= g < n_groups)

        fu = min(d, ATT_FIRST_UNROLL)
        pipelined(d // fu, lambda it, fu=fu, sub=sub: [
            ((it * fu + u) * sub, it * fu + u, False, it * fu + u) for u in range(fu)])

        if nb > 1:
            lu = ATT_LATER_UNROLL
            per_res = (nb - 1) // lu
            assert per_res * lu == nb - 1

            def later_group(it, lu=lu, per_res=per_res, sub=sub, d=d):
                if per_res == 1:
                    r, n0 = it, 1
                elif d == 1:
                    r, n0 = 0, it * lu + 1
                else:
                    r, n0 = it // per_res, (it % per_res) * lu + 1
                return [(r * sub + (n0 + u) * qb, r + d * qb * (n0 + u), True, d + it * lu + u)
                        for u in range(lu)]

            pipelined(d * per_res, later_group)

    tiles_per_residue = SEQ // d_home // COMBINE_ROWS

    for tb in range(SEQ // COMBINE_ROWS):
        rows = slice(tb * COMBINE_ROWS, (tb + 1) * COMBINE_ROWS)
        tokens = pl.ds(tb // tiles_per_residue + d_home * COMBINE_ROWS * (tb % tiles_per_residue),
                       COMBINE_ROWS, stride=d_home)
        at = lambda ref, bi: ref.at[bi][tokens if DILATED_BRANCHES[bi][1] == 1 else rows, :]
        ms = [at(macc_ref, bi) for bi in range(len(DILATED_BRANCHES))]
        top = functools.reduce(jnp.maximum, ms)
        ws = [jnp.exp(m - top) for m in ms]
        num = sum(w * at(oacc_ref, bi) for bi, w in enumerate(ws))
        den = sum(w * at(lacc_ref, bi) for bi, w in enumerate(ws))
        src_ref.at[0][tokens, :] = num / den
        between()
    for c0 in range(0, SEQ, COMBINE_ROWS):
        y_ref[c0:c0 + COMBINE_ROWS, :] = src_ref[0, c0:c0 + COMBINE_ROWS, :].astype(BF16)


def _post_kernel(ys_ref, yr_ref, ya_ref, x_ref, wo_ref, an_ref, l1w_ref, l1b_ref,
                 w1_ref, w2_ref, l2w_ref, l2b_ref, o_ref, x1_ref, acc_ref):
    part = POST_TILE // POST_PARTS
    n_chunks = D_FF // FF_CHUNK
    slab = part // n_chunks
    parts = [slice(r0, r0 + part) for r0 in range(0, POST_TILE, part)]

    def out_proj(rows):
        ya = ya_ref[rows, :].astype(F32)
        ms = jnp.mean(ya * ya, axis=-1, keepdims=True)
        ya = (ya * lax.rsqrt(ms + LN_EPS) * an_ref[...]).astype(BF16)
        h = jnp.dot(ys_ref[rows, :], wo_ref[0:SSM_WIDTH, :], preferred_element_type=F32)
        h = h + jnp.dot(yr_ref[rows, :], wo_ref[SSM_WIDTH:SSM_WIDTH + RET_WIDTH, :],
                        preferred_element_type=F32)
        h = h + jnp.dot(ya, wo_ref[SSM_WIDTH + RET_WIDTH:, :], preferred_element_type=F32)
        o_ref[rows, :] = DEEPNORM_ALPHA * x_ref[rows, :] + h

    def ln1(rows):
        x1_ref[rows, :] = _layer_norm(o_ref[rows, :], l1w_ref[...], l1b_ref[...])

    def ln2(rows):
        o_ref[rows, :] = _layer_norm(DEEPNORM_ALPHA * x1_ref[rows, :] + acc_ref[rows, :],
                                     l2w_ref[...], l2b_ref[...])

    def mlp(rows, between):
        xb = x1_ref[rows, :].astype(BF16)
        acc = jnp.zeros((part, D_MODEL), F32)
        for k in range(n_chunks):
            c = k * FF_CHUNK
            hid = jnp.dot(xb, w1_ref[:, c:c + FF_CHUNK], preferred_element_type=F32)
            hid = jnp.square(jnp.maximum(hid, 0.0)).astype(BF16)
            acc = acc + jnp.dot(hid, w2_ref[c:c + FF_CHUNK, :], preferred_element_type=F32)
            between(k)
        acc_ref[rows, :] = acc

    slab_of = lambda rows, k: slice(rows.start + k * slab, rows.start + (k + 1) * slab)
    for rows in parts:
        out_proj(rows)
    ln1(parts[0])
    for i, rows in enumerate(parts):
        def between(k, i=i):
            if i + 1 < len(parts):
                ln1(slab_of(parts[i + 1], k))
            if i > 0:
                ln2(slab_of(parts[i - 1], k))
        mlp(rows, between)
    ln2(parts[-1])


def _post(y_ssm, y_ret, y_att, x, w_out, attn_norm, ln1_w, ln1_b, w1, w2, ln2_w, ln2_b, layer):
    tm = POST_TILE
    row = lambda width: pl.BlockSpec((None, tm, width), lambda b, i: (b, i, 0))
    full = lambda shape: _layer_block(shape, layer, single_buffer=True)
    return pl.pallas_call(
        _post_kernel,
        grid=(BATCH, SEQ // tm),
        in_specs=[row(SSM_WIDTH), row(RET_WIDTH), row(ATT_WIDTH), row(D_MODEL),
                  full((D_MODEL, D_MODEL)), full((1, ATT_WIDTH)), full((1, D_MODEL)), full((1, D_MODEL)),
                  full((D_MODEL, D_FF)), full((D_FF, D_MODEL)), full((1, D_MODEL)), full((1, D_MODEL))],
        out_specs=row(D_MODEL),
        out_shape=jax.ShapeDtypeStruct((BATCH, SEQ, D_MODEL), F32),
        scratch_shapes=[pltpu.VMEM((tm, D_MODEL), F32), pltpu.VMEM((tm, D_MODEL), F32)],
        compiler_params=_params("parallel", "parallel"),
        name="out_proj_mlp",
    )(y_ssm, y_ret, y_att, x, w_out, attn_norm, ln1_w, ln1_b, w1, w2, ln2_w, ln2_b)


def kernel(x, w_in, ssm_lambda_re, ssm_lambda_im, ssm_b_re, ssm_b_im, ssm_c_re, ssm_c_im, ssm_d, ssm_log_dt, ssm_w_glu, ssm_b_glu, ssm_out_norm, ret_out_norm, attn_out_norm, w_out, ln1_w, ln1_b, mlp_w1, mlp_w2, ln2_w, ln2_b):
    vec = lambda p: p.astype(F32)[:, None, :]
    bf16 = lambda p: p.astype(BF16)
    wb, a_re, a_im, wc = jax.vmap(_ssm_weights)(ssm_lambda_re, ssm_lambda_im, ssm_b_re, ssm_b_im,
                                                ssm_c_re, ssm_c_im, ssm_log_dt)
    w_in, ssm_w_glu, w_out, mlp_w1, mlp_w2 = map(bf16, (w_in, ssm_w_glu, w_out, mlp_w1, mlp_w2))
    (ssm_d, ssm_b_glu, ssm_out_norm, ret_out_norm, attn_out_norm, ln1_w, ln1_b, ln2_w, ln2_b) = map(
        vec, (ssm_d, ssm_b_glu, ssm_out_norm, ret_out_norm, attn_out_norm, ln1_w, ln1_b, ln2_w, ln2_b))
    x = x.astype(F32)
    for i in range(DEPTH):
        u, y_ret, y_att = _in_proj_ret(x, w_in, ret_out_norm, i)
        y_ssm = _ssm(u, wb, a_re, a_im, wc, ssm_d, ssm_w_glu, ssm_b_glu, ssm_out_norm, i)
        x = _post(y_ssm, y_ret, y_att, x, w_out, attn_out_norm, ln1_w, ln1_b, mlp_w1, mlp_w2, ln2_w, ln2_b, i)
    return x
```

```python
import functools
import math

import jax
import jax.numpy as jnp
from jax import lax
from jax.experimental import pallas as pl
from jax.experimental.pallas import tpu as pltpu

F32 = jnp.float32
BF16 = jnp.bfloat16

D_MODEL = 1024
BATCH = 16
SEQ = 2048
DEPTH = 2
SSM_WIDTH = 256
SSM_GROUP = 16
SSM_GROUPS = 16
SSM_STATE = 64
SSM_STATES = SSM_GROUPS * SSM_STATE
RET_HEAD_DIM = 64
RET_WIDTH = 256
RET_CHUNK = 128
ATT_HEAD_DIM = 64
ATT_WIDTH = 512
ATT_HEADS = 8
DILATED_BRANCHES = ((128, 1), (512, 4), (2048, 16))
ATT_BLOCK = 128
IN_WIDTH = SSM_WIDTH + 4 * RET_WIDTH + 3 * ATT_WIDTH
D_FF = 4 * D_MODEL
DEEPNORM_ALPHA = (2 * DEPTH) ** 0.25
LN_EPS = 1e-5

LANES = 128
HEAD_PAIR = LANES // ATT_HEAD_DIM
VMEM_LIMIT_BYTES = 56 * 1024 * 1024
MASKED_SCORE = -1e30

TOKEN_TILE = 512
IN_PROJ_COLS = 256
POST_TILE = 1024
POST_PARTS = 4
SSM_TIME_TILE = 64
SSM_BATCH_HALF = BATCH // 2
SSM_COL_CHUNK = 512
SSM_STAGE_PITCH = SSM_TIME_TILE + 8
FF_CHUNK = 1024
COMBINE_ROWS = 256
ATT_WORK_STRIDE = 7
ATT_HOME_BRANCH = 1
PREP_ROWS = 128
ATT_FIRST_UNROLL = 4
ATT_LATER_UNROLL = 3

assert RET_HEAD_DIM == ATT_HEAD_DIM and HEAD_PAIR == 2
assert all(w // d == ATT_BLOCK for w, d in DILATED_BRANCHES)
assert DILATED_BRANCHES[0][1] == 1 and all(
    b[1] % a[1] == 0 for a, b in zip(DILATED_BRANCHES, DILATED_BRANCHES[1:]))


def _params(*semantics):
    return pltpu.CompilerParams(dimension_semantics=semantics, vmem_limit_bytes=VMEM_LIMIT_BYTES)


def _layer_block(shape, layer, single_buffer=False):
    mode = dict(pipeline_mode=pl.Buffered(1)) if single_buffer else {}
    return pl.BlockSpec((None,) + tuple(shape), lambda *_: (layer,) + (0,) * len(shape), **mode)


def _sigmoid(x):
    return 1.0 / (1.0 + jnp.exp(-x))


def _layer_norm(r, w, b):
    mu = jnp.mean(r, axis=-1, keepdims=True)
    d = r - mu
    var = jnp.mean(d * d, axis=-1, keepdims=True)
    return d * lax.rsqrt(var + LN_EPS) * w + b


def _in_proj_ret_kernel(x_ref, w_ref, nw_ref, u_ref, y_ref, ya_ref, zr_ref, znew_ref, r_ref, const_ref,
                        ring_ref, *attn_scratch):
    c = RET_CHUNK
    dh = RET_HEAD_DIM
    pairs = RET_WIDTH // LANES
    tm = TOKEN_TILE
    step = pl.program_id(0)
    tiles_per_batch = SEQ // tm
    n_tiles = BATCH * tiles_per_batch
    att_pairs = ATT_WIDTH // LANES
    assert att_pairs == tiles_per_batch
    write_slot = lax.rem(lax.div(step, tiles_per_batch), 2)
    tile_in_batch = lax.rem(step, tiles_per_batch)
    lane = lax.broadcasted_iota(jnp.int32, (c, LANES), 1)
    row = lax.broadcasted_iota(jnp.int32, (c, LANES), 0)
    head1 = lane >= dh
    block_diag = (row >= dh) == head1
    XI, ZETA, G_CHUNK, DMAT = 0, 1, 2, 3

    @pl.when(step == 0)
    def _():
        zr_ref[...] = jnp.zeros_like(zr_ref)
        rowf = row.astype(F32)
        diff = (row - lane).astype(F32)
        for p in range(pairs):
            lg = jnp.log(1.0 - jnp.exp2(-5.0 - (HEAD_PAIR * p + head1.astype(F32))))
            const_ref[p, XI] = jnp.exp((rowf + 1.0) * lg)
            const_ref[p, ZETA] = jnp.exp((c - 1.0 - rowf) * lg) * (dh ** -0.5)
            const_ref[p, G_CHUNK] = jnp.exp(c * lg)
            for j in range(HEAD_PAIR):
                lg_j = math.log(1.0 - 2.0 ** (-5.0 - (HEAD_PAIR * p + j)))
                const_ref[p, DMAT + j] = jnp.where(
                    diff >= 0, jnp.exp(jnp.maximum(diff, 0.0) * lg_j), 0.0) * (dh ** -0.5)

    @pl.when(lax.rem(jnp.maximum(step - 1, 0), tiles_per_batch) == 0)
    def _():
        r_ref[...] = jnp.zeros_like(r_ref)

    def head_mean(t):
        s0 = jnp.sum(jnp.where(head1, 0.0, t), axis=-1, keepdims=True)
        s1 = jnp.sum(jnp.where(head1, t, 0.0), axis=-1, keepdims=True)
        return jnp.where(head1, s1, s0) * (1.0 / dh)

    def retention_chunk(p, n):
        rows = slice(n * c, (n + 1) * c)
        col = lambda part: slice(part * RET_WIDTH + p * LANES, part * RET_WIDTH + (p + 1) * LANES)
        q, k, v = (zr_ref[rows, col(part)] for part in range(3))
        r_prev = r_ref[p]
        o = jnp.zeros((c, LANES), F32)
        for j in range(HEAD_PAIR):
            mine = head1 if j else jnp.logical_not(head1)
            qj = jnp.where(mine, q, jnp.zeros_like(q))
            s = lax.dot_general(qj, k, (((1,), (1,)), ((), ())), preferred_element_type=F32)
            s = s * const_ref[p, DMAT + j]
            vj = jnp.where(mine, v, jnp.zeros_like(v))
            o = o + jnp.dot(s.astype(BF16), vj, preferred_element_type=F32)
        qx = (q.astype(F32) * const_ref[p, XI]).astype(BF16)
        o = o + jnp.dot(qx, r_prev.astype(BF16), preferred_element_type=F32)
        kz = (k.astype(F32) * const_ref[p, ZETA]).T.astype(BF16)
        kv = jnp.dot(kz, v, preferred_element_type=F32)
        r_ref[p] = jnp.where(block_diag, const_ref[p, G_CHUNK] * r_prev + kv, 0.0)
        dlt = o - head_mean(o)
        var = head_mean(dlt * dlt)
        gate = zr_ref[rows, col(3)].astype(F32)
        y = dlt * lax.rsqrt(var + LN_EPS) * nw_ref[:, p * LANES:(p + 1) * LANES] * (gate * _sigmoid(gate))
        dst = (slice(n * c, (n + 1) * c), slice(p * LANES, (p + 1) * LANES))
        y_ref[dst] = jnp.where(step <= n_tiles, y.astype(BF16), y_ref[dst])

    def run_step(with_attention):
        xb = x_ref[...].astype(BF16)

        def project(lo):
            z = jnp.dot(xb, w_ref[:, lo:lo + IN_PROJ_COLS], preferred_element_type=F32).astype(BF16)
            hi = lo + IN_PROJ_COLS
            if hi <= SSM_WIDTH:
                u_ref[:, lo:hi] = z
            elif hi <= SSM_WIDTH + 4 * RET_WIDTH:
                znew_ref[:, lo - SSM_WIDTH:hi - SSM_WIDTH] = z
            else:
                first = (lo - SSM_WIDTH - 4 * RET_WIDTH) // LANES
                rows = pl.ds(pl.multiple_of(tile_in_batch * tm, tm), tm)
                for t in range(IN_PROJ_COLS // LANES):
                    part, pair = divmod(first + t, att_pairs)
                    ring_ref[write_slot, part, pair, rows, :] = z[:, t * LANES:(t + 1) * LANES]

        col_blocks = list(range(0, IN_WIDTH, IN_PROJ_COLS))
        units = [(p, n) for n in range(tm // c) for p in range(pairs)]
        work = []
        for k in range(max(len(col_blocks), len(units))):
            if k < len(col_blocks):
                work.append(functools.partial(project, col_blocks[k]))
            if k < len(units):
                work.append(functools.partial(retention_chunk, *units[k]))
        if with_attention:
            calls = [0]

            def between():
                calls[0] += 1
                if work and calls[0] % ATT_WORK_STRIDE == 0:
                    work.pop(0)()

            read = lambda part: ring_ref.at[1 - write_slot, part, tile_in_batch]
            _attention_body(tile_in_batch, read(0), read(1), read(2), ya_ref, *attn_scratch, between)
        else:
            ya_ref[...] = jnp.zeros_like(ya_ref)
        while work:
            work.pop(0)()
        zr_ref[...] = znew_ref[...]

    pl.when(step < tiles_per_batch)(lambda: run_step(False))
    pl.when(step >= tiles_per_batch)(lambda: run_step(True))


def _in_proj_ret(x, w_in, ret_norm, layer):
    tm = TOKEN_TILE
    per_batch = SEQ // tm
    n_tiles = BATCH * per_batch
    n_dil = len(DILATED_BRANCHES) - 1
    att_pairs = ATT_WIDTH // LANES

    def tile(width, index):
        return pl.BlockSpec((None, tm, width),
                            lambda s: (lax.div(index(s), per_batch), lax.rem(index(s), per_batch), 0))

    cur = lambda width: tile(width, lambda s: jnp.minimum(s, n_tiles - 1))
    lag = lambda width: tile(width, lambda s: jnp.minimum(jnp.maximum(s - 1, 0), n_tiles - 1))
    att = pl.BlockSpec((None, SEQ, LANES), lambda s: (
        jnp.where(s < per_batch, BATCH, lax.div(s, per_batch) - 1), 0, lax.rem(s, per_batch)))
    return pl.pallas_call(
        _in_proj_ret_kernel,
        grid=(n_tiles + per_batch,),
        in_specs=[cur(D_MODEL),
                  _layer_block((D_MODEL, IN_WIDTH), layer, single_buffer=True),
                  _layer_block((1, RET_WIDTH), layer)],
        out_specs=[cur(SSM_WIDTH), lag(RET_WIDTH), att],
        out_shape=[jax.ShapeDtypeStruct((BATCH, SEQ, SSM_WIDTH), BF16),
                   jax.ShapeDtypeStruct((BATCH, SEQ, RET_WIDTH), BF16),
                   jax.ShapeDtypeStruct((BATCH + 1, SEQ, ATT_WIDTH), BF16)],
        scratch_shapes=[pltpu.VMEM((tm, 4 * RET_WIDTH), BF16),
                        pltpu.VMEM((tm, 4 * RET_WIDTH), BF16),
                        pltpu.VMEM((RET_WIDTH // LANES, LANES, LANES), F32),
                        pltpu.VMEM((RET_WIDTH // LANES, 3 + HEAD_PAIR, RET_CHUNK, LANES), F32),
                        pltpu.VMEM((2, 3, att_pairs, SEQ, LANES), BF16),
                        pltpu.VMEM((2, SEQ, LANES), F32),
                        pltpu.VMEM((n_dil + 1, HEAD_PAIR, SEQ, LANES), BF16),
                        pltpu.VMEM((n_dil, SEQ, LANES), BF16),
                        pltpu.VMEM((n_dil, SEQ, LANES), BF16),
                        pltpu.VMEM((n_dil + 1, SEQ, LANES), F32),
                        pltpu.VMEM((n_dil + 1, SEQ, LANES), F32),
                        pltpu.VMEM((n_dil + 1, SEQ, LANES), F32),
                        pltpu.VMEM((HEAD_PAIR * ATT_BLOCK, 2 * ATT_BLOCK), F32),
                        pltpu.VMEM((HEAD_PAIR * ATT_BLOCK, ATT_BLOCK), F32),
                        pltpu.VMEM((SEQ // ATT_BLOCK, HEAD_PAIR * ATT_BLOCK, 2 * ATT_BLOCK), BF16),
                        pltpu.VMEM((SEQ // ATT_BLOCK, HEAD_PAIR * ATT_BLOCK, 2 * ATT_BLOCK), F32)],
        compiler_params=_params("arbitrary"),
        name="in_proj_retention_attention",
    )(x, w_in, ret_norm)


def _ssm_kernel(u_ref, wb_ref, are_ref, aim_ref, wc_ref, d_ref, wglu_ref, bglu_ref, nw_ref,
                y_ref, bu_ref, st_ref, ut_ref, stage_ref, out_stage_ref, ypre_ref):
    n_st = SSM_STATES
    bh = SSM_BATCH_HALF
    lt = SSM_TIME_TILE
    pitch = SSM_STAGE_PITCH
    lane_slabs = SSM_WIDTH // LANES
    chunks = [(slice(c, c + SSM_COL_CHUNK), slice(n_st + c, n_st + c + SSM_COL_CHUNK))
              for c in range(0, n_st, SSM_COL_CHUNK)]

    step = pl.program_id(0)
    n_tiles = SEQ // lt
    cur = lax.rem(step, 2)
    prev = 1 - cur

    @pl.when(step == 0)
    def _():
        st_ref[...] = jnp.zeros_like(st_ref)

    def stage_rows(h, t):
        return pl.ds(h * bh * pitch + t, bh, stride=pitch)

    def gather(h):
        for b in range(h * bh, (h + 1) * bh):
            for s in range(lane_slabs):
                stage_ref[s, b * pitch:b * pitch + lt, :] = (
                    u_ref[b, :, s * LANES:(s + 1) * LANES].astype(F32))
        for t in range(0, lt, 2):
            for s in range(lane_slabs):
                pair = jnp.concatenate([stage_ref.at[s][stage_rows(h, t + k), :] for k in range(2)], axis=0)
                ut_ref[cur, h, t * bh:(t + 2) * bh, s * LANES:(s + 1) * LANES] = pair.astype(BF16)

    def scatter(h, y):
        for t in range(lt):
            for s in range(lane_slabs):
                out_stage_ref.at[s][stage_rows(h, t), :] = y[t * bh:(t + 1) * bh, s * LANES:(s + 1) * LANES]
        for b in range(h * bh, (h + 1) * bh):
            for s in range(lane_slabs):
                y_ref[b, :, s * LANES:(s + 1) * LANES] = (
                    out_stage_ref[s, b * pitch:b * pitch + lt, :].astype(BF16))

    def project(h, cols):
        for sl in cols:
            bu_ref[h, :, sl] = jnp.dot(ut_ref[cur, h], wb_ref[:, sl], preferred_element_type=F32)

    def scan(h, cols):
        re, im = cols
        ar = jnp.broadcast_to(are_ref[:, re], (bh, SSM_COL_CHUNK))
        ai = jnp.broadcast_to(aim_ref[:, re], (bh, SSM_COL_CHUNK))
        xr, xi = st_ref[h, :, re], st_ref[h, :, im]
        for t in range(SSM_TIME_TILE):
            rows = slice(t * bh, (t + 1) * bh)
            xr, xi = (ar * xr - ai * xi + bu_ref[h, rows, re],
                      ar * xi + ai * xr + bu_ref[h, rows, im])
            bu_ref[h, rows, re] = xr
            bu_ref[h, rows, im] = xi
        st_ref[h, :, re] = xr
        st_ref[h, :, im] = xi

    def readout(h, cols, acc):
        for sl in cols:
            acc = acc + jnp.dot(bu_ref[h, :, sl].astype(BF16), wc_ref[sl, :], preferred_element_type=F32)
        return acc

    def finish_previous(h):
        y = ypre_ref[prev, h] + d_ref[...] * ut_ref[prev, h].astype(F32)
        cdf = 0.5 * (1.0 + jnp.tanh(math.sqrt(2.0 / math.pi) * (y + 0.044715 * (y * y * y))))
        g = y * cdf
        gate = jnp.dot(g.astype(BF16), wglu_ref[...], preferred_element_type=F32) + bglu_ref[...]
        out = g * _sigmoid(gate)
        ms = jnp.mean(out * out, axis=-1, keepdims=True)
        scatter(h, out * lax.rsqrt(ms + LN_EPS) * nw_ref[...])

    def run_step(with_scan, with_finish):
        if not with_scan:
            finish_previous(0)
            finish_previous(1)
            return
        zero = jnp.zeros((SSM_TIME_TILE * bh, SSM_WIDTH), F32)
        gather(0)
        for k, cols in enumerate(chunks):
            project(0, cols)
            if k == 0:
                if with_finish:
                    finish_previous(0)
                gather(1)
        for cols in chunks:
            project(1, cols)
            scan(0, cols)
        y0 = zero
        for cols in chunks:
            y0 = readout(0, cols, y0)
            scan(1, cols)
        ypre_ref[cur, 0] = y0
        y1 = zero
        for k, cols in enumerate(chunks):
            y1 = readout(1, cols, y1)
            if k == 0 and with_finish:
                finish_previous(1)
        ypre_ref[cur, 1] = y1

    pl.when(step == 0)(lambda: run_step(True, False))
    pl.when((step > 0) & (step < n_tiles))(lambda: run_step(True, True))
    pl.when(step == n_tiles)(lambda: run_step(False, True))


def _ssm(u, wb, a_re, a_im, wc, d_skip, w_glu, b_glu, norm_w, layer):
    rows = SSM_TIME_TILE * SSM_BATCH_HALF
    full = lambda shape: _layer_block(shape, layer)
    n_tiles = SEQ // SSM_TIME_TILE
    tile = lambda index: pl.BlockSpec((BATCH, SSM_TIME_TILE, SSM_WIDTH), lambda i: (0, index(i), 0))
    stage = pltpu.VMEM((SSM_WIDTH // LANES, BATCH * SSM_STAGE_PITCH, LANES), F32)
    return pl.pallas_call(
        _ssm_kernel,
        grid=(n_tiles + 1,),
        in_specs=[tile(lambda i: jnp.minimum(i, n_tiles - 1)),
                  full((SSM_WIDTH, 2 * SSM_STATES)), full((1, SSM_STATES)), full((1, SSM_STATES)),
                  full((2 * SSM_STATES, SSM_WIDTH)), full((1, SSM_WIDTH)),
                  full((SSM_WIDTH, SSM_WIDTH)), full((1, SSM_WIDTH)), full((1, SSM_WIDTH))],
        out_specs=tile(lambda i: jnp.maximum(i - 1, 0)),
        out_shape=jax.ShapeDtypeStruct((BATCH, SEQ, SSM_WIDTH), BF16),
        scratch_shapes=[pltpu.VMEM((2, rows, 2 * SSM_STATES), F32),
                        pltpu.VMEM((2, SSM_BATCH_HALF, 2 * SSM_STATES), F32),
                        pltpu.VMEM((2, 2, rows, SSM_WIDTH), BF16),
                        stage, stage,
                        pltpu.VMEM((2, 2, rows, SSM_WIDTH), F32)],
        compiler_params=_params("arbitrary"),
        name="s5_mixer",
    )(u, wb, a_re, a_im, wc, d_skip, w_glu, b_glu, norm_w)


def _ssm_weights(lam_re, lam_im, b_re, b_im, c_re, c_im, log_dt):
    g, p, h = SSM_GROUPS, SSM_STATE, SSM_GROUP
    lr, li = lam_re.astype(F32), lam_im.astype(F32)
    dt = jnp.exp(log_dt.astype(F32))[:, None]
    mag = jnp.exp(lr * dt)
    a_re, a_im = mag * jnp.cos(li * dt), mag * jnp.sin(li * dt)
    den = lr * lr + li * li
    nr, ni = a_re - 1.0, a_im
    f_re = ((nr * lr + ni * li) / den)[..., None]
    f_im = ((ni * lr - nr * li) / den)[..., None]
    br, bi = b_re.astype(F32), b_im.astype(F32)
    bb_re = f_re * br - f_im * bi
    bb_im = f_re * bi + f_im * br
    eye = jnp.eye(g, dtype=F32)
    embed_b = lambda bb: jnp.einsum('gph,gk->ghkp', bb, eye).reshape(g * h, g * p)
    embed_c = lambda cc: jnp.einsum('ghp,gk->gpkh', cc, eye).reshape(g * p, g * h)
    wb = jnp.concatenate([embed_b(bb_re), embed_b(bb_im)], axis=1).astype(BF16)
    wc = jnp.concatenate([embed_c(c_re.astype(F32)), -embed_c(c_im.astype(F32))], axis=0).astype(BF16)
    return wb, a_re.reshape(1, g * p), a_im.reshape(1, g * p), wc


def _attention_body(hp, q_ref, k_ref, v_ref, y_ref, src_ref, dq_ref, dk_ref, dv_ref,
                    oacc_ref, macc_ref, lacc_ref, bias_ref, bias0_ref, p_ref, s_ref, between):
    qb = ATT_BLOCK
    dh = ATT_HEAD_DIM
    d_home = DILATED_BRANCHES[ATT_HOME_BRANCH][1]
    head1 = lax.broadcasted_iota(jnp.int32, (qb, LANES), 1) >= dh

    def emit_q(bi, rows, blk):
        other = lax.broadcasted_iota(jnp.int32, blk.shape, 1) >= dh
        blk = blk * (dh ** -0.5)
        dq_ref[bi, 0, rows, :] = jnp.where(other, 0.0, blk).astype(BF16)
        dq_ref[bi, 1, rows, :] = jnp.where(other, blk, 0.0).astype(BF16)

    def emit_k(bi, rows, blk):
        if bi > 0:
            dk_ref[bi - 1, rows, :] = blk.astype(BF16)

    def emit_v(bi, rows, blk):
        if bi > 0:
            dv_ref[bi - 1, rows, :] = blk.astype(BF16)

    piece = PREP_ROWS
    for src, emit in ((q_ref, emit_q), (k_ref, emit_k), (v_ref, emit_v)):
        for c0 in range(0, SEQ, piece):
            rows = slice(c0, c0 + piece)
            natural = src[rows, :].astype(F32)
            src_ref[0, rows, :] = natural
            emit(0, rows, natural)
        d_prev = 1
        for bi in range(1, len(DILATED_BRANCHES)):
            d = DILATED_BRANCHES[bi][1]
            step = d // d_prev
            sub_prev, sub = SEQ // d_prev, SEQ // d
            cur, nxt = (bi - 1) % 2, bi % 2
            for r_prev in range(d_prev):
                for t in range(step):
                    r = r_prev + d_prev * t
                    for c0 in range(0, sub, piece):
                        n = min(piece, sub - c0)
                        rows = slice(r * sub + c0, r * sub + c0 + n)
                        blk = src_ref.at[cur][pl.ds(r_prev * sub_prev + t + step * c0, n, stride=step), :]
                        emit(bi, rows, blk)
                        if bi + 1 < len(DILATED_BRANCHES):
                            src_ref[nxt, rows, :] = blk
            d_prev = d

    row2 = lax.broadcasted_iota(jnp.int32, (HEAD_PAIR * qb, 2 * qb), 0)
    key2 = lax.broadcasted_iota(jnp.int32, (HEAD_PAIR * qb, 2 * qb), 1)
    row1 = lax.broadcasted_iota(jnp.int32, (HEAD_PAIR * qb, qb), 0)
    key1 = lax.broadcasted_iota(jnp.int32, (HEAD_PAIR * qb, qb), 1)

    def head_slope(row):
        head = (HEAD_PAIR * hp).astype(F32) + (row >= qb).astype(F32)
        return jnp.exp2(-(8.0 / ATT_HEADS) * (head + 1.0))

    dist2 = qb + jnp.where(row2 >= qb, row2 - qb, row2) - key2
    dist1 = jnp.where(row1 >= qb, row1 - qb, row1) - key1
    slope2, slope1 = head_slope(row2), head_slope(row1)

    def aligned(x):
        return x if isinstance(x, int) else pl.multiple_of(x, qb)

    for bi, (window, d) in enumerate(DILATED_BRANCHES):
        sub = SEQ // d
        nb = sub // qb
        ksrc = k_ref if bi == 0 else dk_ref.at[bi - 1]
        vsrc = v_ref if bi == 0 else dv_ref.at[bi - 1]
        bias_ref[...] = jnp.where((dist2 >= 0) & (dist2 <= window // d),
                                  -slope2 * (d * dist2).astype(F32), MASKED_SCORE)
        bias0_ref[...] = jnp.where(dist1 >= 0, -slope1 * (d * dist1).astype(F32), MASKED_SCORE)

        def rows_of(blk, d=d):
            base, first_token, has_prev, _ = blk
            rows = pl.ds(aligned(base), qb)
            krows = pl.ds(aligned(base - qb), 2 * qb) if has_prev else rows
            if d == 1 or d == d_home:
                out_rows = rows
            else:
                ft = first_token
                home = (ft % d_home) * (SEQ // d_home) + ft // d_home if isinstance(ft, int) else (
                    lax.rem(ft, d_home) * (SEQ // d_home) + lax.div(ft, d_home))
                out_rows = pl.ds(home, qb, stride=d // d_home)
            return rows, krows, out_rows, (2 * qb if has_prev else qb)

        def scores(blk, bi=bi, ksrc=ksrc, rows_of=rows_of):
            rows, krows, out_rows, nk = rows_of(blk)
            q2 = jnp.concatenate([dq_ref[bi, j, rows, :] for j in range(HEAD_PAIR)], axis=0)
            s_ref[blk[3], :, 0:nk] = lax.dot_general(
                q2, ksrc[krows, :], (((1,), (1,)), ((), ())), preferred_element_type=F32)

        def softmax(blk, bi=bi, rows_of=rows_of):
            rows, krows, out_rows, nk = rows_of(blk)
            s = s_ref[blk[3], :, 0:nk] + (bias_ref[...] if blk[2] else bias0_ref[...])
            m = jnp.max(s, axis=-1, keepdims=True)
            e = jnp.exp(s - m)
            l = jnp.sum(e, axis=-1, keepdims=True)
            p_ref[blk[3], :, 0:nk] = e.astype(BF16)
            packed = lambda c: jnp.where(head1, jnp.broadcast_to(c[qb:], (qb, LANES)),
                                         jnp.broadcast_to(c[:qb], (qb, LANES)))
            macc_ref.at[bi][out_rows, :] = packed(m)
            lacc_ref.at[bi][out_rows, :] = packed(l)

        def values(blk, bi=bi, vsrc=vsrc, rows_of=rows_of):
            rows, krows, out_rows, nk = rows_of(blk)
            v = vsrc[krows, :]
            o = [jnp.dot(p_ref[blk[3], j * qb:(j + 1) * qb, 0:nk], v, preferred_element_type=F32)
                 for j in range(HEAD_PAIR)]
            oacc_ref.at[bi][out_rows, :] = jnp.where(head1, o[1], o[0])

        def pipelined(n_groups, group, stages=(scores, softmax, values)):
            def step(t, valid):
                for k in reversed(range(len(stages))):
                    if valid(t - k):
                        for blk in group(t - k):
                            stages[k](blk)
                            between()

            for t in range(n_groups + len(stages) - 1):
                step(t, lambda g: 0 <= g < n_groups)

        fu = min(d, ATT_FIRST_UNROLL)
        pipelined(d // fu, lambda it, fu=fu, sub=sub: [
            ((it * fu + u) * sub, it * fu + u, False, it * fu + u) for u in range(fu)])

        if nb > 1:
            lu = ATT_LATER_UNROLL
            per_res = (nb - 1) // lu
            assert per_res * lu == nb - 1

            def later_group(it, lu=lu, per_res=per_res, sub=sub, d=d):
                if per_res == 1:
                    r, n0 = it, 1
                elif d == 1:
                    r, n0 = 0, it * lu + 1
                else:
                    r, n0 = it // per_res, (it % per_res) * lu + 1
                return [(r * sub + (n0 + u) * qb, r + d * qb * (n0 + u), True, d + it * lu + u)
                        for u in range(lu)]

            pipelined(d * per_res, later_group)

    tiles_per_residue = SEQ // d_home // COMBINE_ROWS

    for tb in range(SEQ // COMBINE_ROWS):
        rows = slice(tb * COMBINE_ROWS, (tb + 1) * COMBINE_ROWS)
        tokens = pl.ds(tb // tiles_per_residue + d_home * COMBINE_ROWS * (tb % tiles_per_residue),
                       COMBINE_ROWS, stride=d_home)
        at = lambda ref, bi: ref.at[bi][tokens if DILATED_BRANCHES[bi][1] == 1 else rows, :]
        ms = [at(macc_ref, bi) for bi in range(len(DILATED_BRANCHES))]
        top = functools.reduce(jnp.maximum, ms)
        ws = [jnp.exp(m - top) for m in ms]
        num = sum(w * at(oacc_ref, bi) for bi, w in enumerate(ws))
        den = sum(w * at(lacc_ref, bi) for bi, w in enumerate(ws))
        src_ref.at[0][tokens, :] = num / den
        between()
    for c0 in range(0, SEQ, COMBINE_ROWS):
        y_ref[c0:c0 + COMBINE_ROWS, :] = src_ref[0, c0:c0 + COMBINE_ROWS, :].astype(BF16)


def _post_kernel(ys_ref, yr_ref, ya_ref, x_ref, wo_ref, an_ref, l1w_ref, l1b_ref,
                 w1_ref, w2_ref, l2w_ref, l2b_ref, o_ref, x1_ref, acc_ref):
    part = POST_TILE // POST_PARTS
    n_chunks = D_FF // FF_CHUNK
    slab = part // n_chunks
    parts = [slice(r0, r0 + part) for r0 in range(0, POST_TILE, part)]

    def out_proj(rows):
        ya = ya_ref[rows, :].astype(F32)
        ms = jnp.mean(ya * ya, axis=-1, keepdims=True)
        ya = (ya * lax.rsqrt(ms + LN_EPS) * an_ref[...]).astype(BF16)
        h = jnp.dot(ys_ref[rows, :], wo_ref[0:SSM_WIDTH, :], preferred_element_type=F32)
        h = h + jnp.dot(yr_ref[rows, :], wo_ref[SSM_WIDTH:SSM_WIDTH + RET_WIDTH, :],
                        preferred_element_type=F32)
        h = h + jnp.dot(ya, wo_ref[SSM_WIDTH + RET_WIDTH:, :], preferred_element_type=F32)
        o_ref[rows, :] = DEEPNORM_ALPHA * x_ref[rows, :] + h

    def ln1(rows):
        x1_ref[rows, :] = _layer_norm(o_ref[rows, :], l1w_ref[...], l1b_ref[...])

    def ln2(rows):
        o_ref[rows, :] = _layer_norm(DEEPNORM_ALPHA * x1_ref[rows, :] + acc_ref[rows, :],
                                     l2w_ref[...], l2b_ref[...])

    def mlp(rows, between):
        xb = x1_ref[rows, :].astype(BF16)
        acc = jnp.zeros((part, D_MODEL), F32)
        for k in range(n_chunks):
            c = k * FF_CHUNK
            hid = jnp.dot(xb, w1_ref[:, c:c + FF_CHUNK], preferred_element_type=F32)
            hid = jnp.square(jnp.maximum(hid, 0.0)).astype(BF16)
            acc = acc + jnp.dot(hid, w2_ref[c:c + FF_CHUNK, :], preferred_element_type=F32)
            between(k)
        acc_ref[rows, :] = acc

    slab_of = lambda rows, k: slice(rows.start + k * slab, rows.start + (k + 1) * slab)
    for rows in parts:
        out_proj(rows)
    ln1(parts[0])
    for i, rows in enumerate(parts):
        def between(k, i=i):
            if i + 1 < len(parts):
                ln1(slab_of(parts[i + 1], k))
            if i > 0:
                ln2(slab_of(parts[i - 1], k))
        mlp(rows, between)
    ln2(parts[-1])


def _post(y_ssm, y_ret, y_att, x, w_out, attn_norm, ln1_w, ln1_b, w1, w2, ln2_w, ln2_b, layer):
    tm = POST_TILE
    row = lambda width: pl.BlockSpec((None, tm, width), lambda b, i: (b, i, 0))
    full = lambda shape: _layer_block(shape, layer, single_buffer=True)
    return pl.pallas_call(
        _post_kernel,
        grid=(BATCH, SEQ // tm),
        in_specs=[row(SSM_WIDTH), row(RET_WIDTH), row(ATT_WIDTH), row(D_MODEL),
                  full((D_MODEL, D_MODEL)), full((1, ATT_WIDTH)), full((1, D_MODEL)), full((1, D_MODEL)),
                  full((D_MODEL, D_FF)), full((D_FF, D_MODEL)), full((1, D_MODEL)), full((1, D_MODEL))],
        out_specs=row(D_MODEL),
        out_shape=jax.ShapeDtypeStruct((BATCH, SEQ, D_MODEL), F32),
        scratch_shapes=[pltpu.VMEM((tm, D_MODEL), F32), pltpu.VMEM((tm, D_MODEL), F32)],
        compiler_params=_params("parallel", "parallel"),
        name="out_proj_mlp",
    )(y_ssm, y_ret, y_att, x, w_out, attn_norm, ln1_w, ln1_b, w1, w2, ln2_w, ln2_b)


def kernel(x, w_in, ssm_lambda_re, ssm_lambda_im, ssm_b_re, ssm_b_im, ssm_c_re, ssm_c_im, ssm_d, ssm_log_dt, ssm_w_glu, ssm_b_glu, ssm_out_norm, ret_out_norm, attn_out_norm, w_out, ln1_w, ln1_b, mlp_w1, mlp_w2, ln2_w, ln2_b):
    vec = lambda p: p.astype(F32)[:, None, :]
    bf16 = lambda p: p.astype(BF16)
    wb, a_re, a_im, wc = jax.vmap(_ssm_weights)(ssm_lambda_re, ssm_lambda_im, ssm_b_re, ssm_b_im,
                                                ssm_c_re, ssm_c_im, ssm_log_dt)
    w_in, ssm_w_glu, w_out, mlp_w1, mlp_w2 = map(bf16, (w_in, ssm_w_glu, w_out, mlp_w1, mlp_w2))
    (ssm_d, ssm_b_glu, ssm_out_norm, ret_out_norm, attn_out_norm, ln1_w, ln1_b, ln2_w, ln2_b) = map(
        vec, (ssm_d, ssm_b_glu, ssm_out_norm, ret_out_norm, attn_out_norm, ln1_w, ln1_b, ln2_w, ln2_b))
    x = x.astype(F32)
    for i in range(DEPTH):
        u, y_ret, y_att = _in_proj_ret(x, w_in, ret_out_norm, i)
        y_ssm = _ssm(u, wb, a_re, a_im, wc, ssm_d, ssm_w_glu, ssm_b_glu, ssm_out_norm, i)
        x = _post(y_ssm, y_ret, y_att, x, w_out, attn_out_norm, ln1_w, ln1_b, mlp_w1, mlp_w2, ln2_w, ln2_b, i)
    return x
```

```python
import functools
import math

import jax
import jax.numpy as jnp
from jax import lax
from jax.experimental import pallas as pl
from jax.experimental.pallas import tpu as pltpu

F32 = jnp.float32
BF16 = jnp.bfloat16

D_MODEL = 1024
BATCH = 16
SEQ = 2048
DEPTH = 2
SSM_WIDTH = 256
SSM_GROUP = 16
SSM_GROUPS = 16
SSM_STATE = 64
SSM_STATES = SSM_GROUPS * SSM_STATE
RET_HEAD_DIM = 64
RET_WIDTH = 256
RET_CHUNK = 128
ATT_HEAD_DIM = 64
ATT_WIDTH = 512
ATT_HEADS = 8
DILATED_BRANCHES = ((128, 1), (512, 4), (2048, 16))
ATT_BLOCK = 128
IN_WIDTH = SSM_WIDTH + 4 * RET_WIDTH + 3 * ATT_WIDTH
D_FF = 4 * D_MODEL
DEEPNORM_ALPHA = (2 * DEPTH) ** 0.25
LN_EPS = 1e-5

LANES = 128
HEAD_PAIR = LANES // ATT_HEAD_DIM
VMEM_LIMIT_BYTES = 56 * 1024 * 1024
MASKED_SCORE = -1e30

TOKEN_TILE = 512
IN_PROJ_COLS = 256
POST_TILE = 1024
POST_PARTS = 4
SSM_TIME_TILE = 64
SSM_BATCH_HALF = BATCH // 2
SSM_COL_CHUNK = 512
SSM_STAGE_PITCH = SSM_TIME_TILE + 8
FF_CHUNK = 1024
COMBINE_ROWS = 256
ATT_WORK_STRIDE = 7
ATT_HOME_BRANCH = 1
PREP_ROWS = 128
ATT_FIRST_UNROLL = 16
ATT_LATER_UNROLL = 3

assert RET_HEAD_DIM == ATT_HEAD_DIM and HEAD_PAIR == 2
assert all(w // d == ATT_BLOCK for w, d in DILATED_BRANCHES)
assert DILATED_BRANCHES[0][1] == 1 and all(
    b[1] % a[1] == 0 for a, b in zip(DILATED_BRANCHES, DILATED_BRANCHES[1:]))


def _params(*semantics):
    return pltpu.CompilerParams(dimension_semantics=semantics, vmem_limit_bytes=VMEM_LIMIT_BYTES)


def _layer_block(shape, layer, single_buffer=False):
    mode = dict(pipeline_mode=pl.Buffered(1)) if single_buffer else {}
    return pl.BlockSpec((None,) + tuple(shape), lambda *_: (layer,) + (0,) * len(shape), **mode)


def _sigmoid(x):
    return 1.0 / (1.0 + jnp.exp(-x))


def _layer_norm(r, w, b):
    mu = jnp.mean(r, axis=-1, keepdims=True)
    d = r - mu
    var = jnp.mean(d * d, axis=-1, keepdims=True)
    return d * lax.rsqrt(var + LN_EPS) * w + b


def _in_proj_ret_kernel(x_ref, w_ref, nw_ref, u_ref, y_ref, ya_ref, zr_ref, znew_ref, r_ref, const_ref,
                        ring_ref, *attn_scratch):
    c = RET_CHUNK
    dh = RET_HEAD_DIM
    pairs = RET_WIDTH // LANES
    tm = TOKEN_TILE
    step = pl.program_id(0)
    tiles_per_batch = SEQ // tm
    n_tiles = BATCH * tiles_per_batch
    att_pairs = ATT_WIDTH // LANES
    assert att_pairs == tiles_per_batch
    write_slot = lax.rem(lax.div(step, tiles_per_batch), 2)
    tile_in_batch = lax.rem(step, tiles_per_batch)
    lane = lax.broadcasted_iota(jnp.int32, (c, LANES), 1)
    row = lax.broadcasted_iota(jnp.int32, (c, LANES), 0)
    head1 = lane >= dh
    block_diag = (row >= dh) == head1
    XI, ZETA, G_CHUNK, DMAT = 0, 1, 2, 3

    @pl.when(step == 0)
    def _():
        zr_ref[...] = jnp.zeros_like(zr_ref)
        rowf = row.astype(F32)
        diff = (row - lane).astype(F32)
        for p in range(pairs):
            lg = jnp.log(1.0 - jnp.exp2(-5.0 - (HEAD_PAIR * p + head1.astype(F32))))
            const_ref[p, XI] = jnp.exp((rowf + 1.0) * lg)
            const_ref[p, ZETA] = jnp.exp((c - 1.0 - rowf) * lg) * (dh ** -0.5)
            const_ref[p, G_CHUNK] = jnp.exp(c * lg)
            for j in range(HEAD_PAIR):
                lg_j = math.log(1.0 - 2.0 ** (-5.0 - (HEAD_PAIR * p + j)))
                const_ref[p, DMAT + j] = jnp.where(
                    diff >= 0, jnp.exp(jnp.maximum(diff, 0.0) * lg_j), 0.0) * (dh ** -0.5)

    @pl.when(lax.rem(jnp.maximum(step - 1, 0), tiles_per_batch) == 0)
    def _():
        r_ref[...] = jnp.zeros_like(r_ref)

    def head_mean(t):
        s0 = jnp.sum(jnp.where(head1, 0.0, t), axis=-1, keepdims=True)
        s1 = jnp.sum(jnp.where(head1, t, 0.0), axis=-1, keepdims=True)
        return jnp.where(head1, s1, s0) * (1.0 / dh)

    def retention_chunk(p, n):
        rows = slice(n * c, (n + 1) * c)
        col = lambda part: slice(part * RET_WIDTH + p * LANES, part * RET_WIDTH + (p + 1) * LANES)
        q, k, v = (zr_ref[rows, col(part)] for part in range(3))
        r_prev = r_ref[p]
        o = jnp.zeros((c, LANES), F32)
        for j in range(HEAD_PAIR):
            mine = head1 if j else jnp.logical_not(head1)
            qj = jnp.where(mine, q, jnp.zeros_like(q))
            s = lax.dot_general(qj, k, (((1,), (1,)), ((), ())), preferred_element_type=F32)
            s = s * const_ref[p, DMAT + j]
            vj = jnp.where(mine, v, jnp.zeros_like(v))
            o = o + jnp.dot(s.astype(BF16), vj, preferred_element_type=F32)
        qx = (q.astype(F32) * const_ref[p, XI]).astype(BF16)
        o = o + jnp.dot(qx, r_prev.astype(BF16), preferred_element_type=F32)
        kz = (k.astype(F32) * const_ref[p, ZETA]).T.astype(BF16)
        kv = jnp.dot(kz, v, preferred_element_type=F32)
        r_ref[p] = jnp.where(block_diag, const_ref[p, G_CHUNK] * r_prev + kv, 0.0)
        dlt = o - head_mean(o)
        var = head_mean(dlt * dlt)
        gate = zr_ref[rows, col(3)].astype(F32)
        y = dlt * lax.rsqrt(var + LN_EPS) * nw_ref[:, p * LANES:(p + 1) * LANES] * (gate * _sigmoid(gate))
        dst = (slice(n * c, (n + 1) * c), slice(p * LANES, (p + 1) * LANES))
        y_ref[dst] = jnp.where(step <= n_tiles, y.astype(BF16), y_ref[dst])

    def run_step(with_attention):
        xb = x_ref[...].astype(BF16)

        def project(lo):
            z = jnp.dot(xb, w_ref[:, lo:lo + IN_PROJ_COLS], preferred_element_type=F32).astype(BF16)
            hi = lo + IN_PROJ_COLS
            if hi <= SSM_WIDTH:
                u_ref[:, lo:hi] = z
            elif hi <= SSM_WIDTH + 4 * RET_WIDTH:
                znew_ref[:, lo - SSM_WIDTH:hi - SSM_WIDTH] = z
            else:
                first = (lo - SSM_WIDTH - 4 * RET_WIDTH) // LANES
                rows = pl.ds(pl.multiple_of(tile_in_batch * tm, tm), tm)
                for t in range(IN_PROJ_COLS // LANES):
                    part, pair = divmod(first + t, att_pairs)
                    ring_ref[write_slot, part, pair, rows, :] = z[:, t * LANES:(t + 1) * LANES]

        col_blocks = list(range(0, IN_WIDTH, IN_PROJ_COLS))
        units = [(p, n) for n in range(tm // c) for p in range(pairs)]
        work = []
        for k in range(max(len(col_blocks), len(units))):
            if k < len(col_blocks):
                work.append(functools.partial(project, col_blocks[k]))
            if k < len(units):
                work.append(functools.partial(retention_chunk, *units[k]))
        if with_attention:
            calls = [0]

            def between():
                calls[0] += 1
                if work and calls[0] % ATT_WORK_STRIDE == 0:
                    work.pop(0)()

            read = lambda part: ring_ref.at[1 - write_slot, part, tile_in_batch]
            _attention_body(tile_in_batch, read(0), read(1), read(2), ya_ref, *attn_scratch, between)
        else:
            ya_ref[...] = jnp.zeros_like(ya_ref)
        while work:
            work.pop(0)()
        zr_ref[...] = znew_ref[...]

    pl.when(step < tiles_per_batch)(lambda: run_step(False))
    pl.when(step >= tiles_per_batch)(lambda: run_step(True))


def _in_proj_ret(x, w_in, ret_norm, layer):
    tm = TOKEN_TILE
    per_batch = SEQ // tm
    n_tiles = BATCH * per_batch
    n_dil = len(DILATED_BRANCHES) - 1
    att_pairs = ATT_WIDTH // LANES

    def tile(width, index):
        return pl.BlockSpec((None, tm, width),
                            lambda s: (lax.div(index(s), per_batch), lax.rem(index(s), per_batch), 0))

    cur = lambda width: tile(width, lambda s: jnp.minimum(s, n_tiles - 1))
    lag = lambda width: tile(width, lambda s: jnp.minimum(jnp.maximum(s - 1, 0), n_tiles - 1))
    att = pl.BlockSpec((None, SEQ, LANES), lambda s: (
        jnp.where(s < per_batch, BATCH, lax.div(s, per_batch) - 1), 0, lax.rem(s, per_batch)))
    return pl.pallas_call(
        _in_proj_ret_kernel,
        grid=(n_tiles + per_batch,),
        in_specs=[cur(D_MODEL),
                  _layer_block((D_MODEL, IN_WIDTH), layer, single_buffer=True),
                  _layer_block((1, RET_WIDTH), layer)],
        out_specs=[cur(SSM_WIDTH), lag(RET_WIDTH), att],
        out_shape=[jax.ShapeDtypeStruct((BATCH, SEQ, SSM_WIDTH), BF16),
                   jax.ShapeDtypeStruct((BATCH, SEQ, RET_WIDTH), BF16),
                   jax.ShapeDtypeStruct((BATCH + 1, SEQ, ATT_WIDTH), BF16)],
        scratch_shapes=[pltpu.VMEM((tm, 4 * RET_WIDTH), BF16),
                        pltpu.VMEM((tm, 4 * RET_WIDTH), BF16),
                        pltpu.VMEM((RET_WIDTH // LANES, LANES, LANES), F32),
                        pltpu.VMEM((RET_WIDTH // LANES, 3 + HEAD_PAIR, RET_CHUNK, LANES), F32),
                        pltpu.VMEM((2, 3, att_pairs, SEQ, LANES), BF16),
                        pltpu.VMEM((2, SEQ, LANES), F32),
                        pltpu.VMEM((n_dil + 1, HEAD_PAIR, SEQ, LANES), BF16),
                        pltpu.VMEM((n_dil, SEQ, LANES), BF16),
                        pltpu.VMEM((n_dil, SEQ, LANES), BF16),
                        pltpu.VMEM((n_dil + 1, SEQ, LANES), F32),
                        pltpu.VMEM((n_dil + 1, SEQ, LANES), F32),
                        pltpu.VMEM((n_dil + 1, SEQ, LANES), F32),
                        pltpu.VMEM((HEAD_PAIR * ATT_BLOCK, 2 * ATT_BLOCK), F32),
                        pltpu.VMEM((HEAD_PAIR * ATT_BLOCK, ATT_BLOCK), F32),
                        pltpu.VMEM((SEQ // ATT_BLOCK, HEAD_PAIR * ATT_BLOCK, 2 * ATT_BLOCK), BF16),
                        pltpu.VMEM((SEQ // ATT_BLOCK, HEAD_PAIR * ATT_BLOCK, 2 * ATT_BLOCK), F32)],
        compiler_params=_params("arbitrary"),
        name="in_proj_retention_attention",
    )(x, w_in, ret_norm)


def _ssm_kernel(u_ref, wb_ref, are_ref, aim_ref, wc_ref, d_ref, wglu_ref, bglu_ref, nw_ref,
                y_ref, bu_ref, st_ref, ut_ref, stage_ref, out_stage_ref, ypre_ref):
    n_st = SSM_STATES
    bh = SSM_BATCH_HALF
    lt = SSM_TIME_TILE
    pitch = SSM_STAGE_PITCH
    lane_slabs = SSM_WIDTH // LANES
    chunks = [(slice(c, c + SSM_COL_CHUNK), slice(n_st + c, n_st + c + SSM_COL_CHUNK))
              for c in range(0, n_st, SSM_COL_CHUNK)]

    step = pl.program_id(0)
    n_tiles = SEQ // lt
    cur = lax.rem(step, 2)
    prev = 1 - cur

    @pl.when(step == 0)
    def _():
        st_ref[...] = jnp.zeros_like(st_ref)

    def stage_rows(h, t):
        return pl.ds(h * bh * pitch + t, bh, stride=pitch)

    def gather(h):
        for b in range(h * bh, (h + 1) * bh):
            for s in range(lane_slabs):
                stage_ref[s, b * pitch:b * pitch + lt, :] = (
                    u_ref[b, :, s * LANES:(s + 1) * LANES].astype(F32))
        for t in range(0, lt, 2):
            for s in range(lane_slabs):
                pair = jnp.concatenate([stage_ref.at[s][stage_rows(h, t + k), :] for k in range(2)], axis=0)
                ut_ref[cur, h, t * bh:(t + 2) * bh, s * LANES:(s + 1) * LANES] = pair.astype(BF16)

    def scatter(h, y):
        for t in range(lt):
            for s in range(lane_slabs):
                out_stage_ref.at[s][stage_rows(h, t), :] = y[t * bh:(t + 1) * bh, s * LANES:(s + 1) * LANES]
        for b in range(h * bh, (h + 1) * bh):
            for s in range(lane_slabs):
                y_ref[b, :, s * LANES:(s + 1) * LANES] = (
                    out_stage_ref[s, b * pitch:b * pitch + lt, :].astype(BF16))

    def project(h, cols):
        for sl in cols:
            bu_ref[h, :, sl] = jnp.dot(ut_ref[cur, h], wb_ref[:, sl], preferred_element_type=F32)

    def scan(h, cols):
        re, im = cols
        ar = jnp.broadcast_to(are_ref[:, re], (bh, SSM_COL_CHUNK))
        ai = jnp.broadcast_to(aim_ref[:, re], (bh, SSM_COL_CHUNK))
        xr, xi = st_ref[h, :, re], st_ref[h, :, im]
        for t in range(SSM_TIME_TILE):
            rows = slice(t * bh, (t + 1) * bh)
            xr, xi = (ar * xr - ai * xi + bu_ref[h, rows, re],
                      ar * xi + ai * xr + bu_ref[h, rows, im])
            bu_ref[h, rows, re] = xr
            bu_ref[h, rows, im] = xi
        st_ref[h, :, re] = xr
        st_ref[h, :, im] = xi

    def readout(h, cols, acc):
        for sl in cols:
            acc = acc + jnp.dot(bu_ref[h, :, sl].astype(BF16), wc_ref[sl, :], preferred_element_type=F32)
        return acc

    def finish_previous(h):
        y = ypre_ref[prev, h] + d_ref[...] * ut_ref[prev, h].astype(F32)
        cdf = 0.5 * (1.0 + jnp.tanh(math.sqrt(2.0 / math.pi) * (y + 0.044715 * (y * y * y))))
        g = y * cdf
        gate = jnp.dot(g.astype(BF16), wglu_ref[...], preferred_element_type=F32) + bglu_ref[...]
        out = g * _sigmoid(gate)
        ms = jnp.mean(out * out, axis=-1, keepdims=True)
        scatter(h, out * lax.rsqrt(ms + LN_EPS) * nw_ref[...])

    def run_step(with_scan, with_finish):
        if not with_scan:
            finish_previous(0)
            finish_previous(1)
            return
        zero = jnp.zeros((SSM_TIME_TILE * bh, SSM_WIDTH), F32)
        gather(0)
        for k, cols in enumerate(chunks):
            project(0, cols)
            if k == 0:
                if with_finish:
                    finish_previous(0)
                gather(1)
        for cols in chunks:
            project(1, cols)
            scan(0, cols)
        y0 = zero
        for cols in chunks:
            y0 = readout(0, cols, y0)
            scan(1, cols)
        ypre_ref[cur, 0] = y0
        y1 = zero
        for k, cols in enumerate(chunks):
            y1 = readout(1, cols, y1)
            if k == 0 and with_finish:
                finish_previous(1)
        ypre_ref[cur, 1] = y1

    pl.when(step == 0)(lambda: run_step(True, False))
    pl.when((step > 0) & (step < n_tiles))(lambda: run_step(True, True))
    pl.when(step == n_tiles)(lambda: run_step(False, True))


def _ssm(u, wb, a_re, a_im, wc, d_skip, w_glu, b_glu, norm_w, layer):
    rows = SSM_TIME_TILE * SSM_BATCH_HALF
    full = lambda shape: _layer_block(shape, layer)
    n_tiles = SEQ // SSM_TIME_TILE
    tile = lambda index: pl.BlockSpec((BATCH, SSM_TIME_TILE, SSM_WIDTH), lambda i: (0, index(i), 0))
    stage = pltpu.VMEM((SSM_WIDTH // LANES, BATCH * SSM_STAGE_PITCH, LANES), F32)
    return pl.pallas_call(
        _ssm_kernel,
        grid=(n_tiles + 1,),
        in_specs=[tile(lambda i: jnp.minimum(i, n_tiles - 1)),
                  full((SSM_WIDTH, 2 * SSM_STATES)), full((1, SSM_STATES)), full((1, SSM_STATES)),
                  full((2 * SSM_STATES, SSM_WIDTH)), full((1, SSM_WIDTH)),
                  full((SSM_WIDTH, SSM_WIDTH)), full((1, SSM_WIDTH)), full((1, SSM_WIDTH))],
        out_specs=tile(lambda i: jnp.maximum(i - 1, 0)),
        out_shape=jax.ShapeDtypeStruct((BATCH, SEQ, SSM_WIDTH), BF16),
        scratch_shapes=[pltpu.VMEM((2, rows, 2 * SSM_STATES), F32),
                        pltpu.VMEM((2, SSM_BATCH_HALF, 2 * SSM_STATES), F32),
                        pltpu.VMEM((2, 2, rows, SSM_WIDTH), BF16),
                        stage, stage,
                        pltpu.VMEM((2, 2, rows, SSM_WIDTH), F32)],
        compiler_params=_params("arbitrary"),
        name="s5_mixer",
    )(u, wb, a_re, a_im, wc, d_skip, w_glu, b_glu, norm_w)


def _ssm_weights(lam_re, lam_im, b_re, b_im, c_re, c_im, log_dt):
    g, p, h = SSM_GROUPS, SSM_STATE, SSM_GROUP
    lr, li = lam_re.astype(F32), lam_im.astype(F32)
    dt = jnp.exp(log_dt.astype(F32))[:, None]
    mag = jnp.exp(lr * dt)
    a_re, a_im = mag * jnp.cos(li * dt), mag * jnp.sin(li * dt)
    den = lr * lr + li * li
    nr, ni = a_re - 1.0, a_im
    f_re = ((nr * lr + ni * li) / den)[..., None]
    f_im = ((ni * lr - nr * li) / den)[..., None]
    br, bi = b_re.astype(F32), b_im.astype(F32)
    bb_re = f_re * br - f_im * bi
    bb_im = f_re * bi + f_im * br
    eye = jnp.eye(g, dtype=F32)
    embed_b = lambda bb: jnp.einsum('gph,gk->ghkp', bb, eye).reshape(g * h, g * p)
    embed_c = lambda cc: jnp.einsum('ghp,gk->gpkh', cc, eye).reshape(g * p, g * h)
    wb = jnp.concatenate([embed_b(bb_re), embed_b(bb_im)], axis=1).astype(BF16)
    wc = jnp.concatenate([embed_c(c_re.astype(F32)), -embed_c(c_im.astype(F32))], axis=0).astype(BF16)
    return wb, a_re.reshape(1, g * p), a_im.reshape(1, g * p), wc


def _attention_body(hp, q_ref, k_ref, v_ref, y_ref, src_ref, dq_ref, dk_ref, dv_ref,
                    oacc_ref, macc_ref, lacc_ref, bias_ref, bias0_ref, p_ref, s_ref, between):
    qb = ATT_BLOCK
    dh = ATT_HEAD_DIM
    d_home = DILATED_BRANCHES[ATT_HOME_BRANCH][1]
    head1 = lax.broadcasted_iota(jnp.int32, (qb, LANES), 1) >= dh

    def emit_q(bi, rows, blk):
        other = lax.broadcasted_iota(jnp.int32, blk.shape, 1) >= dh
        blk = blk * (dh ** -0.5)
        dq_ref[bi, 0, rows, :] = jnp.where(other, 0.0, blk).astype(BF16)
        dq_ref[bi, 1, rows, :] = jnp.where(other, blk, 0.0).astype(BF16)

    def emit_k(bi, rows, blk):
        if bi > 0:
            dk_ref[bi - 1, rows, :] = blk.astype(BF16)

    def emit_v(bi, rows, blk):
        if bi > 0:
            dv_ref[bi - 1, rows, :] = blk.astype(BF16)

    piece = PREP_ROWS
    for src, emit in ((q_ref, emit_q), (k_ref, emit_k), (v_ref, emit_v)):
        for c0 in range(0, SEQ, piece):
            rows = slice(c0, c0 + piece)
            natural = src[rows, :].astype(F32)
            src_ref[0, rows, :] = natural
            emit(0, rows, natural)
        d_prev = 1
        for bi in range(1, len(DILATED_BRANCHES)):
            d = DILATED_BRANCHES[bi][1]
            step = d // d_prev
            sub_prev, sub = SEQ // d_prev, SEQ // d
            cur, nxt = (bi - 1) % 2, bi % 2
            for r_prev in range(d_prev):
                for t in range(step):
                    r = r_prev + d_prev * t
                    for c0 in range(0, sub, piece):
                        n = min(piece, sub - c0)
                        rows = slice(r * sub + c0, r * sub + c0 + n)
                        blk = src_ref.at[cur][pl.ds(r_prev * sub_prev + t + step * c0, n, stride=step), :]
                        emit(bi, rows, blk)
                        if bi + 1 < len(DILATED_BRANCHES):
                            src_ref[nxt, rows, :] = blk
            d_prev = d

    row2 = lax.broadcasted_iota(jnp.int32, (HEAD_PAIR * qb, 2 * qb), 0)
    key2 = lax.broadcasted_iota(jnp.int32, (HEAD_PAIR * qb, 2 * qb), 1)
    row1 = lax.broadcasted_iota(jnp.int32, (HEAD_PAIR * qb, qb), 0)
    key1 = lax.broadcasted_iota(jnp.int32, (HEAD_PAIR * qb, qb), 1)

    def head_slope(row):
        head = (HEAD_PAIR * hp).astype(F32) + (row >= qb).astype(F32)
        return jnp.exp2(-(8.0 / ATT_HEADS) * (head + 1.0))

    dist2 = qb + jnp.where(row2 >= qb, row2 - qb, row2) - key2
    dist1 = jnp.where(row1 >= qb, row1 - qb, row1) - key1
    slope2, slope1 = head_slope(row2), head_slope(row1)

    def aligned(x):
        return x if isinstance(x, int) else pl.multiple_of(x, qb)

    for bi, (window, d) in enumerate(DILATED_BRANCHES):
        sub = SEQ // d
        nb = sub // qb
        ksrc = k_ref if bi == 0 else dk_ref.at[bi - 1]
        vsrc = v_ref if bi == 0 else dv_ref.at[bi - 1]
        bias_ref[...] = jnp.where((dist2 >= 0) & (dist2 <= window // d),
                                  -slope2 * (d * dist2).astype(F32), MASKED_SCORE)
        bias0_ref[...] = jnp.where(dist1 >= 0, -slope1 * (d * dist1).astype(F32), MASKED_SCORE)

        def rows_of(blk, d=d):
            base, first_token, has_prev, _ = blk
            rows = pl.ds(aligned(base), qb)
            krows = pl.ds(aligned(base - qb), 2 * qb) if has_prev else rows
            if d == 1 or d == d_home:
                out_rows = rows
            else:
                ft = first_token
                home = (ft % d_home) * (SEQ // d_home) + ft // d_home if isinstance(ft, int) else (
                    lax.rem(ft, d_home) * (SEQ // d_home) + lax.div(ft, d_home))
                out_rows = pl.ds(home, qb, stride=d // d_home)
            return rows, krows, out_rows, (2 * qb if has_prev else qb)

        def scores(blk, bi=bi, ksrc=ksrc, rows_of=rows_of):
            rows, krows, out_rows, nk = rows_of(blk)
            q2 = jnp.concatenate([dq_ref[bi, j, rows, :] for j in range(HEAD_PAIR)], axis=0)
            s_ref[blk[3], :, 0:nk] = lax.dot_general(
                q2, ksrc[krows, :], (((1,), (1,)), ((), ())), preferred_element_type=F32)

        def softmax(blk, bi=bi, rows_of=rows_of):
            rows, krows, out_rows, nk = rows_of(blk)
            s = s_ref[blk[3], :, 0:nk] + (bias_ref[...] if blk[2] else bias0_ref[...])
            m = jnp.max(s, axis=-1, keepdims=True)
            e = jnp.exp(s - m)
            l = jnp.sum(e, axis=-1, keepdims=True)
            p_ref[blk[3], :, 0:nk] = e.astype(BF16)
            packed = lambda c: jnp.where(head1, jnp.broadcast_to(c[qb:], (qb, LANES)),
                                         jnp.broadcast_to(c[:qb], (qb, LANES)))
            macc_ref.at[bi][out_rows, :] = packed(m)
            lacc_ref.at[bi][out_rows, :] = packed(l)

        def values(blk, bi=bi, vsrc=vsrc, rows_of=rows_of):
            rows, krows, out_rows, nk = rows_of(blk)
            v = vsrc[krows, :]
            o = [jnp.dot(p_ref[blk[3], j * qb:(j + 1) * qb, 0:nk], v, preferred_element_type=F32)
                 for j in range(HEAD_PAIR)]
            oacc_ref.at[bi][out_rows, :] = jnp.where(head1, o[1], o[0])

        def pipelined(n_groups, group, stages=(scores, softmax, values)):
            def step(t, valid):
                for k in reversed(range(len(stages))):
                    if valid(t - k):
                        for blk in group(t - k):
                            stages[k](blk)
                            between()

            for t in range(n_groups + len(stages) - 1):
                step(t, lambda g: 0 <= g < n_groups)

        fu = min(d, ATT_FIRST_UNROLL)
        pipelined(d // fu, lambda it, fu=fu, sub=sub: [
            ((it * fu + u) * sub, it * fu + u, False, it * fu + u) for u in range(fu)])

        if nb > 1:
            lu = ATT_LATER_UNROLL
            per_res = (nb - 1) // lu
            assert per_res * lu == nb - 1

            def later_group(it, lu=lu, per_res=per_res, sub=sub, d=d):
                if per_res == 1:
                    r, n0 = it, 1
                elif d == 1:
                    r, n0 = 0, it * lu + 1
                else:
                    r, n0 = it // per_res, (it % per_res) * lu + 1
                return [(r * sub + (n0 + u) * qb, r + d * qb * (n0 + u), True, d + it * lu + u)
                        for u in range(lu)]

            pipelined(d * per_res, later_group)

    tiles_per_residue = SEQ // d_home // COMBINE_ROWS

    for tb in range(SEQ // COMBINE_ROWS):
        rows = slice(tb * COMBINE_ROWS, (tb + 1) * COMBINE_ROWS)
        tokens = pl.ds(tb // tiles_per_residue + d_home * COMBINE_ROWS * (tb % tiles_per_residue),
                       COMBINE_ROWS, stride=d_home)
        at = lambda ref, bi: ref.at[bi][tokens if DILATED_BRANCHES[bi][1] == 1 else rows, :]
        ms = [at(macc_ref, bi) for bi in range(len(DILATED_BRANCHES))]
        top = functools.reduce(jnp.maximum, ms)
        ws = [jnp.exp(m - top) for m in ms]
        num = sum(w * at(oacc_ref, bi) for bi, w in enumerate(ws))
        den = sum(w * at(lacc_ref, bi) for bi, w in enumerate(ws))
        src_ref.at[0][tokens, :] = num / den
        between()
    for c0 in range(0, SEQ, COMBINE_ROWS):
        y_ref[c0:c0 + COMBINE_ROWS, :] = src_ref[0, c0:c0 + COMBINE_ROWS, :].astype(BF16)


def _post_kernel(ys_ref, yr_ref, ya_ref, x_ref, wo_ref, an_ref, l1w_ref, l1b_ref,
                 w1_ref, w2_ref, l2w_ref, l2b_ref, o_ref, x1_ref, acc_ref):
    part = POST_TILE // POST_PARTS
    n_chunks = D_FF // FF_CHUNK
    slab = part // n_chunks
    parts = [slice(r0, r0 + part) for r0 in range(0, POST_TILE, part)]

    def out_proj(rows):
        ya = ya_ref[rows, :].astype(F32)
        ms = jnp.mean(ya * ya, axis=-1, keepdims=True)
        ya = (ya * lax.rsqrt(ms + LN_EPS) * an_ref[...]).astype(BF16)
        h = jnp.dot(ys_ref[rows, :], wo_ref[0:SSM_WIDTH, :], preferred_element_type=F32)
        h = h + jnp.dot(yr_ref[rows, :], wo_ref[SSM_WIDTH:SSM_WIDTH + RET_WIDTH, :],
                        preferred_element_type=F32)
        h = h + jnp.dot(ya, wo_ref[SSM_WIDTH + RET_WIDTH:, :], preferred_element_type=F32)
        o_ref[rows, :] = DEEPNORM_ALPHA * x_ref[rows, :] + h

    def ln1(rows):
        x1_ref[rows, :] = _layer_norm(o_ref[rows, :], l1w_ref[...], l1b_ref[...])

    def ln2(rows):
        o_ref[rows, :] = _layer_norm(DEEPNORM_ALPHA * x1_ref[rows, :] + acc_ref[rows, :],
                                     l2w_ref[...], l2b_ref[...])

    def mlp(rows, between):
        xb = x1_ref[rows, :].astype(BF16)
        acc = jnp.zeros((part, D_MODEL), F32)
        for k in range(n_chunks):
            c = k * FF_CHUNK
            hid = jnp.dot(xb, w1_ref[:, c:c + FF_CHUNK], preferred_element_type=F32)
            hid = jnp.square(jnp.maximum(hid, 0.0)).astype(BF16)
            acc = acc + jnp.dot(hid, w2_ref[c:c + FF_CHUNK, :], preferred_element_type=F32)
            between(k)
        acc_ref[rows, :] = acc

    slab_of = lambda rows, k: slice(rows.start + k * slab, rows.start + (k + 1) * slab)
    for rows in parts:
        out_proj(rows)
    ln1(parts[0])
    for i, rows in enumerate(parts):
        def between(k, i=i):
            if i + 1 < len(parts):
                ln1(slab_of(parts[i + 1], k))
            if i > 0:
                ln2(slab_of(parts[i - 1], k))
        mlp(rows, between)
    ln2(parts[-1])


def _post(y_ssm, y_ret, y_att, x, w_out, attn_norm, ln1_w, ln1_b, w1, w2, ln2_w, ln2_b, layer):
    tm = POST_TILE
    row = lambda width: pl.BlockSpec((None, tm, width), lambda b, i: (b, i, 0))
    full = lambda shape: _layer_block(shape, layer, single_buffer=True)
    return pl.pallas_call(
        _post_kernel,
        grid=(BATCH, SEQ // tm),
        in_specs=[row(SSM_WIDTH), row(RET_WIDTH), row(ATT_WIDTH), row(D_MODEL),
                  full((D_MODEL, D_MODEL)), full((1, ATT_WIDTH)), full((1, D_MODEL)), full((1, D_MODEL)),
                  full((D_MODEL, D_FF)), full((D_FF, D_MODEL)), full((1, D_MODEL)), full((1, D_MODEL))],
        out_specs=row(D_MODEL),
        out_shape=jax.ShapeDtypeStruct((BATCH, SEQ, D_MODEL), F32),
        scratch_shapes=[pltpu.VMEM((tm, D_MODEL), F32), pltpu.VMEM((tm, D_MODEL), F32)],
        compiler_params=_params("parallel", "parallel"),
        name="out_proj_mlp",
    )(y_ssm, y_ret, y_att, x, w_out, attn_norm, ln1_w, ln1_b, w1, w2, ln2_w, ln2_b)


def kernel(x, w_in, ssm_lambda_re, ssm_lambda_im, ssm_b_re, ssm_b_im, ssm_c_re, ssm_c_im, ssm_d, ssm_log_dt, ssm_w_glu, ssm_b_glu, ssm_out_norm, ret_out_norm, attn_out_norm, w_out, ln1_w, ln1_b, mlp_w1, mlp_w2, ln2_w, ln2_b):
    vec = lambda p: p.astype(F32)[:, None, :]
    bf16 = lambda p: p.astype(BF16)
    wb, a_re, a_im, wc = jax.vmap(_ssm_weights)(ssm_lambda_re, ssm_lambda_im, ssm_b_re, ssm_b_im,
                                                ssm_c_re, ssm_c_im, ssm_log_dt)
    w_in, ssm_w_glu, w_out, mlp_w1, mlp_w2 = map(bf16, (w_in, ssm_w_glu, w_out, mlp_w1, mlp_w2))
    (ssm_d, ssm_b_glu, ssm_out_norm, ret_out_norm, attn_out_norm, ln1_w, ln1_b, ln2_w, ln2_b) = map(
        vec, (ssm_d, ssm_b_glu, ssm_out_norm, ret_out_norm, attn_out_norm, ln1_w, ln1_b, ln2_w, ln2_b))
    x = x.astype(F32)
    for i in range(DEPTH):
        u, y_ret, y_att = _in_proj_ret(x, w_in, ret_out_norm, i)
        y_ssm = _ssm(u, wb, a_re, a_im, wc, ssm_d, ssm_w_glu, ssm_b_glu, ssm_out_norm, i)
        x = _post(y_ssm, y_ret, y_att, x, w_out, attn_out_norm, ln1_w, ln1_b, mlp_w1, mlp_w2, ln2_w, ln2_b, i)
    return x
```
